```python
import jax
import jax.numpy as jnp
from jax import lax
import numpy as np

D_MODEL = 1024
BATCH = 4
SEQ = 4096
DEPTH = 2

GRID_W = 64
CTX_LEN = 256
N_HEADS = 8
HEAD_DIM = 64
ATTN_W = N_HEADS * HEAD_DIM
WIN_ROWS_MAX = 8
WIN_COLS = 16
FOURIER_GROUPS = 4
FOURIER_GROUP_DIM = 64
FOURIER_W = FOURIER_GROUPS * FOURIER_GROUP_DIM
CONV_W = 256
CONV_K = 31
N_BRANCHES = 3
NORM_EPS = 1e-6
IN_SIZES = (ATTN_W,) * 4 + (FOURIER_W,) * 2 + (CONV_W,) * 3 + (D_MODEL,) * N_BRANCHES
IN_W = sum(IN_SIZES)

kernel_name = 'hybrid_natten_fourier_conformer_prefix_block'


def rms_norm(x, g):
    xf = x.astype(jnp.float32)
    y = xf * lax.rsqrt(jnp.mean(xf * xf, axis=-1, keepdims=True) + NORM_EPS)
    return (y * g.astype(jnp.float32)).astype(x.dtype)


def layer_norm(x, g, b):
    xf = x.astype(jnp.float32)
    xc = xf - jnp.mean(xf, axis=-1, keepdims=True)
    var = jnp.mean(xc * xc, axis=-1, keepdims=True)
    y = xc * lax.rsqrt(var + NORM_EPS) * g.astype(jnp.float32) + b.astype(jnp.float32)
    return y.astype(x.dtype)


def split_in(p):
    return jnp.split(p, np.cumsum(IN_SIZES)[:-1].tolist(), axis=-1)


def heads(t):
    return t.reshape(t.shape[0], t.shape[1], N_HEADS, HEAD_DIM)


def context_attention(q, k, v):
    s = jnp.einsum('bqhd,bkhd->bhqk', q, k, preferred_element_type=jnp.float32) * (HEAD_DIM ** -0.5)
    p = jax.nn.softmax(s, axis=-1).astype(v.dtype)
    return jnp.einsum('bhqk,bkhd->bqhd', p, v)


def neighbourhood_attention(q, k, v, k_ctx, v_ctx, rpb):
    B, S, H, Dh = q.shape
    rows = S // GRID_W
    kr = min(WIN_ROWS_MAX, rows)
    nw = kr * WIN_COLS
    scale = Dh ** -0.5
    cols = jnp.arange(GRID_W)
    col_start = jnp.clip(cols - WIN_COLS // 2, 0, GRID_W - WIN_COLS)
    key_cols = col_start[:, None] + jnp.arange(WIN_COLS)
    dc_idx = key_cols - cols[:, None] + (WIN_COLS - 1)
    q_rows = jnp.moveaxis(q.reshape(B, rows, GRID_W, H, Dh), 1, 0)

    def one_row(args):
        r, q_r = args
        row_start = jnp.clip(r - kr // 2, 0, rows - kr)
        key_rows = row_start + jnp.arange(kr)
        tok = (key_rows[None, :, None] * GRID_W + key_cols[:, None, :]).reshape(-1)
        k_w = jnp.take(k, tok, axis=1).reshape(B, GRID_W, nw, H, Dh)
        v_w = jnp.take(v, tok, axis=1).reshape(B, GRID_W, nw, H, Dh)
        dr_idx = key_rows - r + (WIN_ROWS_MAX - 1)
        bias = rpb[:, dr_idx[None, :, None], dc_idx[:, None, :]].reshape(H, GRID_W, nw)
        s_win = jnp.einsum('bqhd,bqkhd->bhqk', q_r, k_w, preferred_element_type=jnp.float32) * scale + bias
        s_ctx = jnp.einsum('bqhd,blhd->bhql', q_r, k_ctx, preferred_element_type=jnp.float32) * scale
        p = jax.nn.softmax(jnp.concatenate([s_win, s_ctx], axis=-1), axis=-1).astype(v.dtype)
        return (jnp.einsum('bhqk,bqkhd->bqhd', p[..., :nw], v_w)
                + jnp.einsum('bhql,blhd->bqhd', p[..., nw:], v_ctx))

    out = lax.map(one_row, (jnp.arange(rows), q_rows))
    return jnp.moveaxis(out, 0, 1).reshape(B, S, H, Dh)


def fourier_mix(u):
    B, N, _ = u.shape
    ug = u.astype(jnp.float32).reshape(B, N, FOURIER_GROUPS, FOURIER_GROUP_DIM)
    f = jnp.fft.fftn(ug, axes=(1, 3), norm='ortho').real
    return f.reshape(B, N, FOURIER_W).astype(u.dtype)


def conv_module(c_a, c_b, w_dw, b_dw, ln_g, ln_b, w_pw):
    u = c_a * jax.nn.sigmoid(c_b)
    u = lax.conv_general_dilated(u, w_dw[:, None, :], window_strides=(1,),
                                 padding=[(CONV_K // 2, CONV_K // 2)],
                                 dimension_numbers=('NWC', 'WIO', 'NWC'),
                                 feature_group_count=CONV_W) + b_dw
    u = jax.nn.silu(layer_norm(u, ln_g, ln_b))
    return u @ w_pw


def branch_tail(attn_o, g_a, u_f, g_f, c_a, c_b, g_c, s_a, s_f, s_c,
                w_four, conv_dw, conv_db, ln_g, ln_b, w_pw, p_attn, p_four, p_conv, w_out):
    B, N = attn_o.shape[:2]
    y_a = (attn_o.reshape(B, N, ATTN_W) * jax.nn.silu(g_a)) @ p_attn
    y_f = ((fourier_mix(u_f) @ w_four) * jax.nn.silu(g_f)) @ p_four
    y_c = (conv_module(c_a, c_b, conv_dw, conv_db, ln_g, ln_b, w_pw) * jax.nn.silu(g_c)) @ p_conv
    merged = jax.nn.sigmoid(s_a) * y_a + jax.nn.sigmoid(s_f) * y_f + jax.nn.sigmoid(s_c) * y_c
    return merged @ w_out


def setup_inputs(seed: int = 0) -> dict:
    key = jax.random.key(seed)
    ks = jax.random.split(key, 20)

    def nrm(k, shape, s):
        return jax.random.normal(k, shape, jnp.float32) * s

    return {
        'x': nrm(ks[0], (BATCH, SEQ, D_MODEL), 1.0),
        'c': nrm(ks[1], (BATCH, D_MODEL), 1.0),
        'ctx': nrm(ks[2], (BATCH, CTX_LEN, D_MODEL), 1.0),
        'c_ctx': nrm(ks[3], (D_MODEL,), 1.0),
        'w_mod': nrm(ks[4], (DEPTH, D_MODEL, 3 * D_MODEL), 0.5 * D_MODEL ** -0.5),
        'b_mod': nrm(ks[5], (DEPTH, 3 * D_MODEL), 0.01),
        'g_pre': 1.0 + nrm(ks[6], (DEPTH, D_MODEL), 0.02),
        'g_post': 1.0 + nrm(ks[7], (DEPTH, D_MODEL), 0.02),
        'w_in': nrm(ks[8], (DEPTH, D_MODEL, IN_W), D_MODEL ** -0.5),
        'rpb': nrm(ks[9], (DEPTH, N_HEADS, 2 * WIN_ROWS_MAX - 1, 2 * WIN_COLS - 1), 0.5),
        'w_four': nrm(ks[10], (DEPTH, FOURIER_W, FOURIER_W), FOURIER_W ** -0.5),
        'conv_dw': nrm(ks[11], (DEPTH, CONV_K, CONV_W), CONV_K ** -0.5),
        'conv_db': nrm(ks[12], (DEPTH, CONV_W), 0.01),
        'conv_ln_g': 1.0 + nrm(ks[13], (DEPTH, CONV_W), 0.02),
        'conv_ln_b': nrm(ks[14], (DEPTH, CONV_W), 0.01),
        'w_pw': nrm(ks[15], (DEPTH, CONV_W, CONV_W), CONV_W ** -0.5),
        'p_attn': nrm(ks[16], (DEPTH, ATTN_W, D_MODEL), ATTN_W ** -0.5),
        'p_four': nrm(ks[17], (DEPTH, FOURIER_W, D_MODEL), FOURIER_W ** -0.5),
        'p_conv': nrm(ks[18], (DEPTH, CONV_W, D_MODEL), CONV_W ** -0.5),
        'w_out': nrm(ks[19], (DEPTH, D_MODEL, D_MODEL), D_MODEL ** -0.5),
    }


def reference(x, c, ctx, c_ctx, w_mod, b_mod, g_pre, g_post, w_in, rpb, w_four,
              conv_dw, conv_db, conv_ln_g, conv_ln_b, w_pw, p_attn, p_four, p_conv, w_out):
    sc = jax.nn.silu(c)
    sc_ctx = jax.nn.silu(c_ctx)
    for l in range(DEPTH):
        tail_w = (w_four[l], conv_dw[l], conv_db[l], conv_ln_g[l], conv_ln_b[l], w_pw[l],
                  p_attn[l], p_four[l], p_conv[l], w_out[l])
        shift, scale, gate = jnp.split(sc @ w_mod[l] + b_mod[l], 3, axis=-1)
        shift_c, scale_c, gate_c = jnp.split(sc_ctx @ w_mod[l] + b_mod[l], 3, axis=-1)
        h = rms_norm(x, g_pre[l]) * (1.0 + scale[:, None, :]) + shift[:, None, :]
        hc = rms_norm(ctx, g_pre[l]) * (1.0 + scale_c) + shift_c
        update_ctx = l < DEPTH - 1
        if update_ctx:
            qc, kc, vc, *rest_c = split_in(hc @ w_in[l])
        else:
            kc, vc = jnp.split(hc @ w_in[l][:, ATTN_W:3 * ATTN_W], 2, axis=-1)
        kc, vc = heads(kc), heads(vc)
        q, k, v, *rest = split_in(h @ w_in[l])
        attn = neighbourhood_attention(heads(q), heads(k), heads(v), kc, vc, rpb[l])
        y = branch_tail(attn, *rest, *tail_w)
        x = x + gate[:, None, :] * rms_norm(y, g_post[l])
        if update_ctx:
            attn_c = context_attention(heads(qc), kc, vc)
            yc = branch_tail(attn_c, *rest_c, *tail_w)
            ctx = ctx + gate_c * rms_norm(yc, g_post[l])
    return x
```

```python
import functools

import jax
import jax.numpy as jnp
import numpy as np
from jax import lax
from jax.experimental import pallas as pl
from jax.experimental.pallas import tpu as pltpu

F32 = jnp.float32
BF16 = jnp.bfloat16

GRID_W = 64
N_HEADS = 8
HEAD_DIM = 64
ATTN_W = N_HEADS * HEAD_DIM
LANES = 128
N_PAIRS = ATTN_W // LANES
WIN_ROWS = 8
WIN_COLS = 16
FOURIER_GROUPS = 4
FOURIER_GROUP_DIM = 64
FOURIER_W = FOURIER_GROUPS * FOURIER_GROUP_DIM
CONV_W = 256
CONV_K = 31
NORM_EPS = 1e-6
MASKED = -1e30

Q_ROWS = 4
K_ROWS = 12
MOD_ROWS = 8

VMEM_LIMIT = 56 * 1024 * 1024


def _sigmoid(x):
    return 1.0 / (1.0 + jnp.exp(-x))


def _silu(x):
    return x * _sigmoid(x)


def _params(n_axes):
    return pltpu.CompilerParams(
        dimension_semantics=("arbitrary",) * n_axes, vmem_limit_bytes=VMEM_LIMIT)


def _mod_kernel(c_ref, w_ref, b_ref, o_ref):
    sc = _silu(c_ref[...]).astype(BF16)
    o_ref[0] = jnp.dot(sc, w_ref[0].astype(BF16), preferred_element_type=F32) + b_ref[0]


def _modulation(c_all, w_mod, b_mod):
    depth, d, d3 = w_mod.shape
    tn = d
    return pl.pallas_call(
        _mod_kernel,
        grid=(depth, d3 // tn),
        in_specs=[
            pl.BlockSpec((MOD_ROWS, d), lambda l, j: (0, 0)),
            pl.BlockSpec((1, d, tn), lambda l, j: (l, 0, j)),
            pl.BlockSpec((1, 1, tn), lambda l, j: (l, 0, j)),
        ],
        out_specs=pl.BlockSpec((1, MOD_ROWS, tn), lambda l, j: (l, 0, j)),
        out_shape=jax.ShapeDtypeStruct((depth, MOD_ROWS, d3), F32),
        compiler_params=_params(2),
        name="modulation",
    )(c_all, w_mod, b_mod.reshape(depth, 1, d3))


_SEC_Q = ("q", 0, ATTN_W, "qscale", ATTN_W, BF16, True)
_SEC_K = ("k", ATTN_W, 2 * ATTN_W, None, ATTN_W, BF16, True)
_SEC_V = ("v", 2 * ATTN_W, 3 * ATTN_W, None, ATTN_W, BF16, True)
_O = 4 * ATTN_W
_SEC_REST = (
    ("ga", 3 * ATTN_W, _O, "silu", ATTN_W, BF16, False),
    ("uf", _O, _O + FOURIER_W, None, FOURIER_W, F32, False),
    ("gf", _O + FOURIER_W, _O + 2 * FOURIER_W, "silu", FOURIER_W, BF16, False),
    ("u", _O + 2 * FOURIER_W, _O + 2 * FOURIER_W + 2 * CONV_W, "glu", CONV_W, BF16, False),
    ("gc", _O + 2 * FOURIER_W + 2 * CONV_W, _O + 2 * FOURIER_W + 3 * CONV_W, "silu", CONV_W, BF16, False),
)
_S0 = _O + 2 * FOURIER_W + 3 * CONV_W
ALL_SECTIONS = (_SEC_Q, _SEC_K, _SEC_V) + _SEC_REST
KV_SECTIONS = (_SEC_K, _SEC_V)


def _merge_sections(d):
    return tuple((n, _S0 + i * d, _S0 + (i + 1) * d, "sigmoid", d, BF16, False)
                 for i, n in enumerate(("sa", "sf", "sc")))


def _inproj_kernel(x_ref, mod_ref, g_ref, w_ref, *o_refs, sections):
    x = x_ref[0]
    d = x.shape[-1]
    m = mod_ref[0]
    shift, scale = m[:, :d], m[:, d:2 * d]
    y = x * lax.rsqrt(jnp.mean(x * x, axis=-1, keepdims=True) + NORM_EPS) * g_ref[...]
    hb = (y * (1.0 + scale) + shift).astype(BF16)
    for (_, lo, hi, act, _, dtype, paired), o_ref in zip(sections, o_refs):
        acc = jnp.dot(hb, w_ref[:, lo:hi], preferred_element_type=F32)
        if act == "qscale":
            acc = acc * (HEAD_DIM ** -0.5)
        elif act == "silu":
            acc = _silu(acc)
        elif act == "sigmoid":
            acc = _sigmoid(acc)
        elif act == "glu":
            half = (hi - lo) // 2
            acc = acc[:, :half] * _sigmoid(acc[:, half:])
        if paired:
            for p in range(N_PAIRS):
                o_ref[0, p] = acc[:, p * LANES:(p + 1) * LANES].astype(dtype)
        else:
            o_ref[0] = acc.astype(dtype)


def _inproj(x, mod_l, g_pre_l, w_in_bf, sections, mod_row, tm):
    b, s, d = x.shape
    tm = min(tm, s)
    in_w = w_in_bf.shape[1]
    out_shapes, out_specs = [], []
    for (_, _, _, _, width, dtype, paired) in sections:
        if paired:
            out_shapes.append(jax.ShapeDtypeStruct((b, N_PAIRS, s, LANES), dtype))
            out_specs.append(pl.BlockSpec((1, N_PAIRS, tm, LANES), lambda bi, i: (bi, 0, i, 0)))
        else:
            out_shapes.append(jax.ShapeDtypeStruct((b, s, width), dtype))
            out_specs.append(pl.BlockSpec((1, tm, width), lambda bi, i: (bi, i, 0)))
    return pl.pallas_call(
        functools.partial(_inproj_kernel, sections=sections),
        grid=(b, s // tm),
        in_specs=[
            pl.BlockSpec((1, tm, d), lambda bi, i: (bi, i, 0)),
            pl.BlockSpec((1, 1, 3 * d), lambda bi, i: (mod_row(bi), 0, 0)),
            pl.BlockSpec((1, d), lambda bi, i: (0, 0)),
            pl.BlockSpec((d, in_w), lambda bi, i: (0, 0)),
        ],
        out_specs=out_specs,
        out_shape=out_shapes,
        compiler_params=_params(2),
        name="inproj",
    )(x, mod_l.reshape(MOD_ROWS, 1, 3 * d), g_pre_l.reshape(1, d), w_in_bf)


def _bias_tables(rpb_l, rows):
    kr = min(WIN_ROWS, rows)
    n_blk = rows // Q_ROWS
    cols = np.arange(GRID_W)
    col_start = np.clip(cols - WIN_COLS // 2, 0, GRID_W - WIN_COLS)
    kc = np.arange(GRID_W)
    valid_c = (kc[None, :] >= col_start[:, None]) & (kc[None, :] < col_start[:, None] + WIN_COLS)
    dc_idx = np.clip(kc[None, :] - cols[:, None] + (WIN_COLS - 1), 0, 2 * WIN_COLS - 2)
    dr_idx = np.zeros((3, Q_ROWS, K_ROWS), np.int32)
    valid_r = np.zeros((3, Q_ROWS, K_ROWS), bool)
    for cfg, i in enumerate((0, 1, n_blk - 1)):
        ws = int(np.clip(Q_ROWS * i - kr // 2, 0, rows - K_ROWS))
        for ri in range(Q_ROWS):
            r = Q_ROWS * i + ri
            rs = int(np.clip(r - kr // 2, 0, rows - kr))
            for j in range(K_ROWS):
                krow = ws + j
                valid_r[cfg, ri, j] = rs <= krow < rs + kr
                dr_idx[cfg, ri, j] = np.clip(krow - r + (WIN_ROWS - 1), 0, 2 * WIN_ROWS - 2)
    bias = rpb_l[:, dr_idx[:, :, None, :, None], dc_idx[None, None, :, None, :]]
    mask = valid_r[:, :, None, :, None] & valid_c[None, None, :, None, :]
    bias = jnp.where(mask[None], bias, MASKED)
    bias = jnp.transpose(bias, (1, 0, 2, 3, 4, 5))
    return bias.reshape(3, N_HEADS, Q_ROWS * GRID_W, K_ROWS * GRID_W)


def _head_masks():
    lane = lax.broadcasted_iota(jnp.int32, (1, LANES), 1)
    return lane < HEAD_DIM


def _softmax_pv(q_h, keys, vals, biases):
    dn = (((1,), (1,)), ((), ()))
    scores = []
    for k, bias in zip(keys, biases):
        s = lax.dot_general(q_h, k, dn, preferred_element_type=F32)
        scores.append(s if bias is None else s + bias)
    m = functools.reduce(jnp.maximum, [jnp.max(s, axis=-1, keepdims=True) for s in scores])
    probs = [jnp.exp(s - m) for s in scores]
    denom = functools.reduce(jnp.add, [jnp.sum(p, axis=-1, keepdims=True) for p in probs])
    o = functools.reduce(jnp.add, [jnp.dot(p.astype(BF16), v, preferred_element_type=F32)
                                   for p, v in zip(probs, vals)])
    return o / denom


def _attn_kernel(q_ref, k_ref, v_ref, kc_ref, vc_ref, bias_ref, o_ref, *, rows):
    i = pl.program_id(2)
    ws = jnp.clip(Q_ROWS * i - min(WIN_ROWS, rows) // 2, 0, rows - K_ROWS)
    start = pl.multiple_of(ws * GRID_W, GRID_W)
    q = q_ref[0, 0]
    kw = k_ref[0, 0, pl.ds(start, K_ROWS * GRID_W), :]
    vw = v_ref[0, 0, pl.ds(start, K_ROWS * GRID_W), :]
    kc, vc = kc_ref[0, 0], vc_ref[0, 0]
    low = _head_masks()
    zero = jnp.zeros_like(q)
    o_lo = _softmax_pv(jnp.where(low, q, zero), (kw, kc), (vw, vc), (bias_ref[0, 0], None))
    o_hi = _softmax_pv(jnp.where(low, zero, q), (kw, kc), (vw, vc), (bias_ref[0, 1], None))
    o_ref[0, 0] = jnp.where(low, o_lo, o_hi).astype(o_ref.dtype)


def _attention(q, k, v, kc, vc, bias):
    b, _, s, _ = q.shape
    l = kc.shape[2]
    rows = s // GRID_W
    n_blk = rows // Q_ROWS
    tq, tk = Q_ROWS * GRID_W, K_ROWS * GRID_W

    def cfg(i):
        return jnp.where(i == 0, 0, jnp.where(i == n_blk - 1, 2, 1))

    return pl.pallas_call(
        functools.partial(_attn_kernel, rows=rows),
        grid=(b, N_PAIRS, n_blk),
        in_specs=[
            pl.BlockSpec((1, 1, tq, LANES), lambda bi, p, i: (bi, p, i, 0)),
            pl.BlockSpec((1, 1, s, LANES), lambda bi, p, i: (bi, p, 0, 0)),
            pl.BlockSpec((1, 1, s, LANES), lambda bi, p, i: (bi, p, 0, 0)),
            pl.BlockSpec((1, 1, l, LANES), lambda bi, p, i: (bi, p, 0, 0)),
            pl.BlockSpec((1, 1, l, LANES), lambda bi, p, i: (bi, p, 0, 0)),
            pl.BlockSpec((1, 2, tq, tk), lambda bi, p, i: (cfg(i), p, 0, 0)),
        ],
        out_specs=pl.BlockSpec((1, 1, tq, LANES), lambda bi, p, i: (bi, p, i, 0)),
        out_shape=jax.ShapeDtypeStruct((b, N_PAIRS, s, LANES), BF16),
        compiler_params=_params(3),
        name="attention",
    )(q, k, v, kc, vc, bias)


def _ctx_attn_kernel(q_ref, k_ref, v_ref, o_ref):
    q, k, v = q_ref[0, 0], k_ref[0, 0], v_ref[0, 0]
    low = _head_masks()
    zero = jnp.zeros_like(q)
    o_lo = _softmax_pv(jnp.where(low, q, zero), (k,), (v,), (None,))
    o_hi = _softmax_pv(jnp.where(low, zero, q), (k,), (v,), (None,))
    o_ref[0, 0] = jnp.where(low, o_lo, o_hi).astype(o_ref.dtype)


def _ctx_attention(q, k, v):
    b, _, l, _ = q.shape
    spec = pl.BlockSpec((1, 1, l, LANES), lambda bi, p: (bi, p, 0, 0))
    return pl.pallas_call(
        _ctx_attn_kernel,
        grid=(b, N_PAIRS),
        in_specs=[spec, spec, spec],
        out_specs=spec,
        out_shape=jax.ShapeDtypeStruct((b, N_PAIRS, l, LANES), BF16),
        compiler_params=_params(2),
        name="ctx_attention",
    )(q, k, v)


def _dft_cos_sin(k, n, period):
    ang = (2.0 * np.pi / period) * ((k[:, None] * n[None, :]) % period)
    return np.cos(ang), np.sin(ang)


def _table(a):
    return jnp.asarray(a, F32).astype(BF16)


def _channel_dft():
    m = np.arange(FOURIER_GROUP_DIM)
    c, s = _dft_cos_sin(m, m, FOURIER_GROUP_DIM)
    eye = np.eye(FOURIER_GROUPS)
    scale = FOURIER_GROUP_DIM ** -0.5
    return np.concatenate([np.kron(eye, c.T), np.kron(eye, s.T)], axis=0) * scale


def _channel_mix(xr, xi, cs_ref, wf_ref):
    x = jnp.concatenate([xr, xi], axis=-1).astype(BF16)
    y = jnp.dot(x, cs_ref[...], preferred_element_type=F32)
    return jnp.dot(y.astype(BF16), wf_ref[...], preferred_element_type=F32)


def _fft_kernel(x_ref, w1_ref, m2_ref, cs_ref, wf_ref, o_ref, ar_ref, ai_ref, *, n1, group):
    for g in range(GRID_W // group):
        xs = jnp.concatenate([x_ref[0, :, g * group + j, :] for j in range(group)], axis=-1)
        a = jnp.dot(w1_ref[...], xs.astype(BF16), preferred_element_type=F32)
        for j in range(group):
            blk = a[:, j * FOURIER_W:(j + 1) * FOURIER_W]
            ar_ref[:, g * group + j, :] = blk[:n1]
            ai_ref[:, g * group + j, :] = blk[n1:]
    for g in range(n1 // group):
        xr, xi = [], []
        for j in range(group):
            k1 = g * group + j
            a = jnp.concatenate([ar_ref[k1], ai_ref[k1]], axis=0).astype(BF16)
            x = jnp.dot(m2_ref[k1], a, preferred_element_type=F32)
            xr.append(x[:GRID_W])
            xi.append(x[GRID_W:])
        z = _channel_mix(jnp.concatenate(xr, axis=0), jnp.concatenate(xi, axis=0), cs_ref, wf_ref)
        for j in range(group):
            o_ref[0, :, g * group + j, :] = z[j * GRID_W:(j + 1) * GRID_W]


def _fourier(uf, wf_bf):
    b, s, fw = uf.shape
    n1 = s // GRID_W
    group = 4
    k1 = np.arange(n1)
    c1, s1 = _dft_cos_sin(k1, k1, n1)
    w1 = np.concatenate([c1, -s1], axis=0) * n1 ** -0.5
    n2 = np.arange(GRID_W)
    kk = k1[:, None] + n1 * np.arange(GRID_W)[None, :]
    ang = (2.0 * np.pi / s) * ((kk[:, :, None] * n2[None, None, :]) % s)
    c2, s2 = np.cos(ang), np.sin(ang)
    m2 = np.concatenate([np.concatenate([c2, s2], axis=2),
                         np.concatenate([-s2, c2], axis=2)], axis=1) * GRID_W ** -0.5
    out = pl.pallas_call(
        functools.partial(_fft_kernel, n1=n1, group=group),
        grid=(b,),
        in_specs=[
            pl.BlockSpec((1, n1, GRID_W, fw), lambda bi: (bi, 0, 0, 0)),
            pl.BlockSpec((2 * n1, n1), lambda bi: (0, 0)),
            pl.BlockSpec((n1, 2 * GRID_W, 2 * GRID_W), lambda bi: (0, 0, 0)),
            pl.BlockSpec((2 * fw, fw), lambda bi: (0, 0)),
            pl.BlockSpec((fw, fw), lambda bi: (0, 0)),
        ],
        out_specs=pl.BlockSpec((1, GRID_W, n1, fw), lambda bi: (bi, 0, 0, 0)),
        out_shape=jax.ShapeDtypeStruct((b, GRID_W, n1, fw), F32),
        scratch_shapes=[pltpu.VMEM((n1, GRID_W, fw), F32), pltpu.VMEM((n1, GRID_W, fw), F32)],
        compiler_params=_params(1),
        name="fourier",
    )(uf.reshape(b, n1, GRID_W, fw), _table(w1), _table(m2), _table(_channel_dft()), wf_bf)
    return out.reshape(b, s, fw)


def _ctx_fft_kernel(x_ref, wd_ref, cs_ref, wf_ref, o_ref):
    n = x_ref.shape[1]
    x = jnp.dot(wd_ref[...], x_ref[0].astype(BF16), preferred_element_type=F32)
    o_ref[0] = _channel_mix(x[:n], x[n:], cs_ref, wf_ref)


def _ctx_fourier(uf, wf_bf):
    b, n, fw = uf.shape
    k = np.arange(n)
    c, s = _dft_cos_sin(k, k, n)
    wd = np.concatenate([c, -s], axis=0) * n ** -0.5
    return pl.pallas_call(
        _ctx_fft_kernel,
        grid=(b,),
        in_specs=[
            pl.BlockSpec((1, n, fw), lambda bi: (bi, 0, 0)),
            pl.BlockSpec((2 * n, n), lambda bi: (0, 0)),
            pl.BlockSpec((2 * fw, fw), lambda bi: (0, 0)),
            pl.BlockSpec((fw, fw), lambda bi: (0, 0)),
        ],
        out_specs=pl.BlockSpec((1, n, fw), lambda bi: (bi, 0, 0)),
        out_shape=jax.ShapeDtypeStruct((b, n, fw), F32),
        compiler_params=_params(1),
        name="ctx_fourier",
    )(uf, _table(wd), _table(_channel_dft()), wf_bf)


CONV_PAD = 16
CONV_TILE = 32
SUBLANES = 8
PW_TILE = 256


def _conv_kernel(u_ref, gc_ref, dw_ref, db_ref, lg_ref, lb_ref, pw_ref, o_ref, up_ref, y_ref):
    n = u_ref.shape[1]
    zeros = jnp.zeros((CONV_PAD, CONV_W), F32)
    up_ref[0:CONV_PAD, :] = zeros
    up_ref[CONV_PAD + n:CONV_PAD + n + CONV_PAD, :] = zeros
    up_ref[CONV_PAD:CONV_PAD + n, :] = u_ref[0].astype(F32)
    first = CONV_PAD - CONV_K // 2
    halo = CONV_TILE + 2 * CONV_PAD

    def conv_step(it, carry):
        base = pl.multiple_of(it * CONV_TILE, CONV_TILE)
        win = up_ref[pl.ds(base, halo), :]
        acc = jnp.zeros((CONV_TILE, CONV_W), F32)
        for ph in range(SUBLANES):
            taps = [t for t in range(CONV_K) if (first + t) % SUBLANES == ph]
            if not taps:
                continue
            span = (first + taps[-1]) - ph + CONV_TILE
            shifted = win[ph:ph + span]
            for t in taps:
                off = first + t - ph
                acc = acc + shifted[off:off + CONV_TILE] * dw_ref[t:t + 1, :]
        acc = acc + db_ref[...]
        xc = acc - jnp.mean(acc, axis=-1, keepdims=True)
        var = jnp.mean(xc * xc, axis=-1, keepdims=True)
        y = xc * lax.rsqrt(var + NORM_EPS) * lg_ref[...] + lb_ref[...]
        y_ref[pl.ds(base, CONV_TILE), :] = _silu(y).astype(BF16)
        return carry

    lax.fori_loop(0, n // CONV_TILE, conv_step, 0)

    pw_tile = min(PW_TILE, n)

    def pw_step(it, carry):
        base = pl.multiple_of(it * pw_tile, pw_tile)
        z = jnp.dot(y_ref[pl.ds(base, pw_tile), :], pw_ref[...], preferred_element_type=F32)
        o_ref[0, pl.ds(base, pw_tile), :] = (z * gc_ref[0, pl.ds(base, pw_tile), :].astype(F32)).astype(BF16)
        return carry

    lax.fori_loop(0, n // pw_tile, pw_step, 0)


def _conv_module(u, gc, conv_dw_l, conv_db_l, ln_g_l, ln_b_l, w_pw_bf):
    b, n, cw = u.shape
    seq = pl.BlockSpec((1, n, cw), lambda bi: (bi, 0, 0))
    row = pl.BlockSpec((1, cw), lambda bi: (0, 0))
    return pl.pallas_call(
        _conv_kernel,
        grid=(b,),
        in_specs=[seq, seq, pl.BlockSpec((CONV_K, cw), lambda bi: (0, 0)), row, row, row,
                  pl.BlockSpec((cw, cw), lambda bi: (0, 0))],
        out_specs=seq,
        out_shape=jax.ShapeDtypeStruct((b, n, cw), BF16),
        scratch_shapes=[pltpu.VMEM((n + 2 * CONV_PAD, cw), F32), pltpu.VMEM((n, cw), BF16)],
        compiler_params=_params(1),
        name="conv_module",
    )(u, gc, conv_dw_l, conv_db_l.reshape(1, cw), ln_g_l.reshape(1, cw), ln_b_l.reshape(1, cw), w_pw_bf)


def _tail_kernel(x_ref, ao_ref, ga_ref, fw_ref, gf_ref, cw_ref, sa_ref, sf_ref, sc_ref, mod_ref, gp_ref,
                 pa_ref, pf_ref, pc_ref, wo_ref, o_ref):
    d = x_ref.shape[-1]
    ao = jnp.concatenate([ao_ref[0, p] for p in range(N_PAIRS)], axis=-1)
    ya = jnp.dot(ao * ga_ref[0], pa_ref[...], preferred_element_type=F32)
    yf = jnp.dot((fw_ref[0] * gf_ref[0].astype(F32)).astype(BF16), pf_ref[...], preferred_element_type=F32)
    yc = jnp.dot(cw_ref[0], pc_ref[...], preferred_element_type=F32)
    merged = (sa_ref[0].astype(F32) * ya + sf_ref[0].astype(F32) * yf + sc_ref[0].astype(F32) * yc)
    y = jnp.dot(merged.astype(BF16), wo_ref[...], preferred_element_type=F32)
    yn = y * lax.rsqrt(jnp.mean(y * y, axis=-1, keepdims=True) + NORM_EPS) * gp_ref[...]
    gate = mod_ref[0][:, 2 * d:]
    o_ref[0] = x_ref[0] + gate * yn


def _tail(x, ao, ga, fw, gf, cw, sa, sf, sc, mod_l, g_post_l, pa_bf, pf_bf, pc_bf, wo_bf, mod_row, tm):
    b, s, d = x.shape
    tm = min(tm, s)

    def tok(width):
        return pl.BlockSpec((1, tm, width), lambda bi, i: (bi, i, 0))

    def full(shape):
        return pl.BlockSpec(shape, lambda bi, i: (0,) * len(shape))

    return pl.pallas_call(
        _tail_kernel,
        grid=(b, s // tm),
        in_specs=[
            tok(d),
            pl.BlockSpec((1, N_PAIRS, tm, LANES), lambda bi, i: (bi, 0, i, 0)),
            tok(ATTN_W), tok(FOURIER_W), tok(FOURIER_W), tok(CONV_W), tok(d), tok(d), tok(d),
            pl.BlockSpec((1, 1, 3 * d), lambda bi, i: (mod_row(bi), 0, 0)),
            full((1, d)),
            full(pa_bf.shape), full(pf_bf.shape), full(pc_bf.shape), full(wo_bf.shape),
        ],
        out_specs=tok(d),
        out_shape=jax.ShapeDtypeStruct((b, s, d), F32),
        compiler_params=_params(2),
        name="tail",
    )(x, ao, ga, fw, gf, cw, sa, sf, sc, mod_l.reshape(MOD_ROWS, 1, 3 * d), g_post_l.reshape(1, d),
      pa_bf, pf_bf, pc_bf, wo_bf)


INPROJ_TILE = 256
TAIL_TILE = 512


def kernel(x, c, ctx, c_ctx, w_mod, b_mod, g_pre, g_post, w_in, rpb, w_four, conv_dw, conv_db,
           conv_ln_g, conv_ln_b, w_pw, p_attn, p_four, p_conv, w_out):
    batch, seq, d = x.shape
    depth = w_mod.shape[0]
    rows = seq // GRID_W
    assert batch < MOD_ROWS and rows % Q_ROWS == 0 and rows >= K_ROWS
    ctx_row = batch
    c_all = jnp.zeros((MOD_ROWS, d), F32).at[:batch].set(c).at[ctx_row].set(c_ctx)
    mod = _modulation(c_all, w_mod, b_mod)
    merge_sections = _merge_sections(d)

    def latent_row(bi):
        return bi

    def context_row(bi):
        return ctx_row

    for l in range(depth):
        w_in_bf = w_in[l].astype(BF16)
        wf_bf, pw_bf = w_four[l].astype(BF16), w_pw[l].astype(BF16)
        pa_bf, pf_bf, pc_bf, wo_bf = (p_attn[l].astype(BF16), p_four[l].astype(BF16),
                                      p_conv[l].astype(BF16), w_out[l].astype(BF16))
        conv_w = (conv_dw[l], conv_db[l], conv_ln_g[l], conv_ln_b[l], pw_bf)
        tail_w = (mod[l], g_post[l], pa_bf, pf_bf, pc_bf, wo_bf)
        update_ctx = l < depth - 1

        ctx_sections = ALL_SECTIONS + merge_sections if update_ctx else KV_SECTIONS
        ctx_out = dict(zip([sec[0] for sec in ctx_sections],
                           _inproj(ctx, mod[l], g_pre[l], w_in_bf, ctx_sections, context_row, INPROJ_TILE)))

        lat = dict(zip([sec[0] for sec in ALL_SECTIONS + merge_sections],
                       _inproj(x, mod[l], g_pre[l], w_in_bf, ALL_SECTIONS + merge_sections, latent_row,
                               INPROJ_TILE)))
        ao = _attention(lat["q"], lat["k"], lat["v"], ctx_out["k"], ctx_out["v"], _bias_tables(rpb[l], rows))
        fw = _fourier(lat["uf"], wf_bf)
        cw = _conv_module(lat["u"], lat["gc"], *conv_w)
        x = _tail(x, ao, lat["ga"], fw, lat["gf"], cw, lat["sa"], lat["sf"], lat["sc"], *tail_w,
                  latent_row, TAIL_TILE)

        if update_ctx:
            co = ctx_out
            ao_c = _ctx_attention(co["q"], co["k"], co["v"])
            fw_c = _ctx_fourier(co["uf"], wf_bf)
            cw_c = _conv_module(co["u"], co["gc"], *conv_w)
            ctx = _tail(ctx, ao_c, co["ga"], fw_c, co["gf"], cw_c, co["sa"], co["sf"], co["sc"], *tail_w,
                        context_row, TAIL_TILE)
    return x
```

```python
import functools

import jax
import jax.numpy as jnp
import numpy as np
from jax import lax
from jax.experimental import pallas as pl
from jax.experimental.pallas import tpu as pltpu

F32 = jnp.float32
BF16 = jnp.bfloat16

GRID_W = 64
N_HEADS = 8
HEAD_DIM = 64
ATTN_W = N_HEADS * HEAD_DIM
LANES = 128
N_PAIRS = ATTN_W // LANES
WIN_ROWS = 8
WIN_COLS = 16
FOURIER_GROUPS = 4
FOURIER_GROUP_DIM = 64
FOURIER_W = FOURIER_GROUPS * FOURIER_GROUP_DIM
CONV_W = 256
CONV_K = 31
NORM_EPS = 1e-6
MASKED = -1e30

Q_ROWS = 4
K_ROWS = 12
MOD_ROWS = 8

VMEM_LIMIT = 56 * 1024 * 1024


def _sigmoid(x):
    return 1.0 / (1.0 + jnp.exp(-x))


def _silu(x):
    return x * _sigmoid(x)


def _params(n_axes):
    return pltpu.CompilerParams(
        dimension_semantics=("arbitrary",) * n_axes, vmem_limit_bytes=VMEM_LIMIT)


def _mod_kernel(c_ref, w_ref, b_ref, o_ref):
    sc = _silu(c_ref[...]).astype(BF16)
    o_ref[0] = jnp.dot(sc, w_ref[0].astype(BF16), preferred_element_type=F32) + b_ref[0]


def _modulation(c_all, w_mod, b_mod):
    depth, d, d3 = w_mod.shape
    tn = d
    return pl.pallas_call(
        _mod_kernel,
        grid=(depth, d3 // tn),
        in_specs=[
            pl.BlockSpec((MOD_ROWS, d), lambda l, j: (0, 0)),
            pl.BlockSpec((1, d, tn), lambda l, j: (l, 0, j)),
            pl.BlockSpec((1, 1, tn), lambda l, j: (l, 0, j)),
        ],
        out_specs=pl.BlockSpec((1, MOD_ROWS, tn), lambda l, j: (l, 0, j)),
        out_shape=jax.ShapeDtypeStruct((depth, MOD_ROWS, d3), F32),
        compiler_params=_params(2),
        name="modulation",
    )(c_all, w_mod, b_mod.reshape(depth, 1, d3))


_SEC_Q = ("q", 0, ATTN_W, "qscale", ATTN_W, BF16, True)
_SEC_K = ("k", ATTN_W, 2 * ATTN_W, None, ATTN_W, BF16, True)
_SEC_V = ("v", 2 * ATTN_W, 3 * ATTN_W, None, ATTN_W, BF16, True)
_O = 4 * ATTN_W
_SEC_REST = (
    ("ga", 3 * ATTN_W, _O, "silu", ATTN_W, BF16, False),
    ("uf", _O, _O + FOURIER_W, None, FOURIER_W, F32, False),
    ("gf", _O + FOURIER_W, _O + 2 * FOURIER_W, "silu", FOURIER_W, BF16, False),
    ("u", _O + 2 * FOURIER_W, _O + 2 * FOURIER_W + 2 * CONV_W, "glu", CONV_W, BF16, False),
    ("gc", _O + 2 * FOURIER_W + 2 * CONV_W, _O + 2 * FOURIER_W + 3 * CONV_W, "silu", CONV_W, BF16, False),
)
_S0 = _O + 2 * FOURIER_W + 3 * CONV_W
ALL_SECTIONS = (_SEC_Q, _SEC_K, _SEC_V) + _SEC_REST
KV_SECTIONS = (_SEC_K, _SEC_V)


def _merge_sections(d):
    return tuple((n, _S0 + i * d, _S0 + (i + 1) * d, "sigmoid", d, BF16, False)
                 for i, n in enumerate(("sa", "sf", "sc")))


def _inproj_kernel(x_ref, mod_ref, g_ref, w_ref, *o_refs, sections):
    x = x_ref[0]
    d = x.shape[-1]
    m = mod_ref[0]
    shift, scale = m[:, :d], m[:, d:2 * d]
    y = x * lax.rsqrt(jnp.mean(x * x, axis=-1, keepdims=True) + NORM_EPS) * g_ref[...]
    hb = (y * (1.0 + scale) + shift).astype(BF16)
    for (_, lo, hi, act, _, dtype, paired), o_ref in zip(sections, o_refs):
        acc = jnp.dot(hb, w_ref[:, lo:hi], preferred_element_type=F32)
        if act == "qscale":
            acc = acc * (HEAD_DIM ** -0.5)
        elif act == "silu":
            acc = _silu(acc)
        elif act == "sigmoid":
            acc = _sigmoid(acc)
        elif act == "glu":
            half = (hi - lo) // 2
            acc = acc[:, :half] * _sigmoid(acc[:, half:])
        if paired:
            for p in range(N_PAIRS):
                o_ref[0, p] = acc[:, p * LANES:(p + 1) * LANES].astype(dtype)
        else:
            o_ref[0] = acc.astype(dtype)


def _inproj(x, mod_l, g_pre_l, w_in_bf, sections, mod_row, tm):
    b, s, d = x.shape
    tm = min(tm, s)
    in_w = w_in_bf.shape[1]
    out_shapes, out_specs = [], []
    for (_, _, _, _, width, dtype, paired) in sections:
        if paired:
            out_shapes.append(jax.ShapeDtypeStruct((b, N_PAIRS, s, LANES), dtype))
            out_specs.append(pl.BlockSpec((1, N_PAIRS, tm, LANES), lambda bi, i: (bi, 0, i, 0)))
        else:
            out_shapes.append(jax.ShapeDtypeStruct((b, s, width), dtype))
            out_specs.append(pl.BlockSpec((1, tm, width), lambda bi, i: (bi, i, 0)))
    return pl.pallas_call(
        functools.partial(_inproj_kernel, sections=sections),
        grid=(b, s // tm),
        in_specs=[
            pl.BlockSpec((1, tm, d), lambda bi, i: (bi, i, 0)),
            pl.BlockSpec((1, 1, 3 * d), lambda bi, i: (mod_row(bi), 0, 0)),
            pl.BlockSpec((1, d), lambda bi, i: (0, 0)),
            pl.BlockSpec((d, in_w), lambda bi, i: (0, 0)),
        ],
        out_specs=out_specs,
        out_shape=out_shapes,
        compiler_params=_params(2),
        name="inproj",
    )(x, mod_l.reshape(MOD_ROWS, 1, 3 * d), g_pre_l.reshape(1, d), w_in_bf)


def _bias_tables(rpb_l, rows):
    kr = min(WIN_ROWS, rows)
    n_blk = rows // Q_ROWS
    cols = np.arange(GRID_W)
    col_start = np.clip(cols - WIN_COLS // 2, 0, GRID_W - WIN_COLS)
    kc = np.arange(GRID_W)
    valid_c = (kc[None, :] >= col_start[:, None]) & (kc[None, :] < col_start[:, None] + WIN_COLS)
    pad = GRID_W - WIN_COLS
    padded = jnp.pad(rpb_l, ((0, 0), (0, 0), (pad, pad)))
    per_dr = jnp.stack([padded[:, :, GRID_W - 1 - c:2 * GRID_W - 1 - c] for c in range(GRID_W)], axis=2)
    per_dr = jnp.where(valid_c[None, None], per_dr, MASKED)
    masked_tile = jnp.full((N_HEADS, GRID_W, GRID_W), MASKED, F32)
    configs = []
    for i in (0, 1, n_blk - 1):
        ws = int(np.clip(Q_ROWS * i - kr // 2, 0, rows - K_ROWS))
        q_rows = []
        for ri in range(Q_ROWS):
            r = Q_ROWS * i + ri
            rs = int(np.clip(r - kr // 2, 0, rows - kr))
            tiles = [per_dr[:, ws + j - r + (WIN_ROWS - 1)] if rs <= ws + j < rs + kr else masked_tile
                     for j in range(K_ROWS)]
            q_rows.append(jnp.stack(tiles, axis=2).reshape(N_HEADS, GRID_W, K_ROWS * GRID_W))
        configs.append(jnp.stack(q_rows, axis=1).reshape(N_HEADS, Q_ROWS * GRID_W, K_ROWS * GRID_W))
    return jnp.stack(configs, axis=0)


def _head_masks():
    lane = lax.broadcasted_iota(jnp.int32, (1, LANES), 1)
    return lane < HEAD_DIM


def _softmax_pv(q_h, keys, vals, biases):
    dn = (((1,), (1,)), ((), ()))
    scores = []
    for k, bias in zip(keys, biases):
        s = lax.dot_general(q_h, k, dn, preferred_element_type=F32)
        scores.append(s if bias is None else s + bias)
    m = functools.reduce(jnp.maximum, [jnp.max(s, axis=-1, keepdims=True) for s in scores])
    probs = [jnp.exp(s - m) for s in scores]
    denom = functools.reduce(jnp.add, [jnp.sum(p, axis=-1, keepdims=True) for p in probs])
    o = functools.reduce(jnp.add, [jnp.dot(p.astype(BF16), v, preferred_element_type=F32)
                                   for p, v in zip(probs, vals)])
    return o / denom


def _attn_kernel(q_ref, k_ref, v_ref, kc_ref, vc_ref, bias_ref, o_ref, *, rows):
    i = pl.program_id(2)
    ws = jnp.clip(Q_ROWS * i - min(WIN_ROWS, rows) // 2, 0, rows - K_ROWS)
    start = pl.multiple_of(ws * GRID_W, GRID_W)
    q = q_ref[0, 0]
    kw = k_ref[0, 0, pl.ds(start, K_ROWS * GRID_W), :]
    vw = v_ref[0, 0, pl.ds(start, K_ROWS * GRID_W), :]
    kc, vc = kc_ref[0, 0], vc_ref[0, 0]
    low = _head_masks()
    zero = jnp.zeros_like(q)
    o_lo = _softmax_pv(jnp.where(low, q, zero), (kw, kc), (vw, vc), (bias_ref[0, 0], None))
    o_hi = _softmax_pv(jnp.where(low, zero, q), (kw, kc), (vw, vc), (bias_ref[0, 1], None))
    o_ref[0, 0] = jnp.where(low, o_lo, o_hi).astype(o_ref.dtype)


def _attention(q, k, v, kc, vc, bias):
    b, _, s, _ = q.shape
    l = kc.shape[2]
    rows = s // GRID_W
    n_blk = rows // Q_ROWS
    tq, tk = Q_ROWS * GRID_W, K_ROWS * GRID_W

    def cfg(i):
        return jnp.where(i == 0, 0, jnp.where(i == n_blk - 1, 2, 1))

    return pl.pallas_call(
        functools.partial(_attn_kernel, rows=rows),
        grid=(b, N_PAIRS, n_blk),
        in_specs=[
            pl.BlockSpec((1, 1, tq, LANES), lambda bi, p, i: (bi, p, i, 0)),
            pl.BlockSpec((1, 1, s, LANES), lambda bi, p, i: (bi, p, 0, 0)),
            pl.BlockSpec((1, 1, s, LANES), lambda bi, p, i: (bi, p, 0, 0)),
            pl.BlockSpec((1, 1, l, LANES), lambda bi, p, i: (bi, p, 0, 0)),
            pl.BlockSpec((1, 1, l, LANES), lambda bi, p, i: (bi, p, 0, 0)),
            pl.BlockSpec((1, 2, tq, tk), lambda bi, p, i: (cfg(i), p, 0, 0)),
        ],
        out_specs=pl.BlockSpec((1, 1, tq, LANES), lambda bi, p, i: (bi, p, i, 0)),
        out_shape=jax.ShapeDtypeStruct((b, N_PAIRS, s, LANES), BF16),
        compiler_params=_params(3),
        name="attention",
    )(q, k, v, kc, vc, bias)


def _ctx_attn_kernel(q_ref, k_ref, v_ref, o_ref):
    q, k, v = q_ref[0, 0], k_ref[0, 0], v_ref[0, 0]
    low = _head_masks()
    zero = jnp.zeros_like(q)
    o_lo = _softmax_pv(jnp.where(low, q, zero), (k,), (v,), (None,))
    o_hi = _softmax_pv(jnp.where(low, zero, q), (k,), (v,), (None,))
    o_ref[0, 0] = jnp.where(low, o_lo, o_hi).astype(o_ref.dtype)


def _ctx_attention(q, k, v):
    b, _, l, _ = q.shape
    spec = pl.BlockSpec((1, 1, l, LANES), lambda bi, p: (bi, p, 0, 0))
    return pl.pallas_call(
        _ctx_attn_kernel,
        grid=(b, N_PAIRS),
        in_specs=[spec, spec, spec],
        out_specs=spec,
        out_shape=jax.ShapeDtypeStruct((b, N_PAIRS, l, LANES), BF16),
        compiler_params=_params(2),
        name="ctx_attention",
    )(q, k, v)


def _dft_cos_sin(k, n, period):
    ang = (2.0 * np.pi / period) * ((k[:, None] * n[None, :]) % period)
    return np.cos(ang), np.sin(ang)


def _table(a):
    return jnp.asarray(a, F32).astype(BF16)


def _channel_dft():
    m = np.arange(FOURIER_GROUP_DIM)
    c, s = _dft_cos_sin(m, m, FOURIER_GROUP_DIM)
    eye = np.eye(FOURIER_GROUPS)
    scale = FOURIER_GROUP_DIM ** -0.5
    return np.concatenate([np.kron(eye, c.T), np.kron(eye, s.T)], axis=0) * scale


def _channel_mix(xr, xi, cs_ref, wf_ref):
    x = jnp.concatenate([xr, xi], axis=-1).astype(BF16)
    y = jnp.dot(x, cs_ref[...], preferred_element_type=F32)
    return jnp.dot(y.astype(BF16), wf_ref[...], preferred_element_type=F32)


def _fft_kernel(x_ref, w1_ref, m2_ref, cs_ref, wf_ref, o_ref, ar_ref, ai_ref, *, n1, group):
    for g in range(GRID_W // group):
        xs = jnp.concatenate([x_ref[0, :, g * group + j, :] for j in range(group)], axis=-1)
        a = jnp.dot(w1_ref[...], xs.astype(BF16), preferred_element_type=F32)
        for j in range(group):
            blk = a[:, j * FOURIER_W:(j + 1) * FOURIER_W]
            ar_ref[:, g * group + j, :] = blk[:n1]
            ai_ref[:, g * group + j, :] = blk[n1:]
    for g in range(n1 // group):
        xr, xi = [], []
        for j in range(group):
            k1 = g * group + j
            a = jnp.concatenate([ar_ref[k1], ai_ref[k1]], axis=0).astype(BF16)
            x = jnp.dot(m2_ref[k1], a, preferred_element_type=F32)
            xr.append(x[:GRID_W])
            xi.append(x[GRID_W:])
        z = _channel_mix(jnp.concatenate(xr, axis=0), jnp.concatenate(xi, axis=0), cs_ref, wf_ref)
        for j in range(group):
            o_ref[0, :, g * group + j, :] = z[j * GRID_W:(j + 1) * GRID_W]


def _fourier(uf, wf_bf):
    b, s, fw = uf.shape
    n1 = s // GRID_W
    group = 4
    k1 = np.arange(n1)
    c1, s1 = _dft_cos_sin(k1, k1, n1)
    w1 = np.concatenate([c1, -s1], axis=0) * n1 ** -0.5
    n2 = np.arange(GRID_W)
    kk = k1[:, None] + n1 * np.arange(GRID_W)[None, :]
    ang = (2.0 * np.pi / s) * ((kk[:, :, None] * n2[None, None, :]) % s)
    c2, s2 = np.cos(ang), np.sin(ang)
    m2 = np.concatenate([np.concatenate([c2, s2], axis=2),
                         np.concatenate([-s2, c2], axis=2)], axis=1) * GRID_W ** -0.5
    out = pl.pallas_call(
        functools.partial(_fft_kernel, n1=n1, group=group),
        grid=(b,),
        in_specs=[
            pl.BlockSpec((1, n1, GRID_W, fw), lambda bi: (bi, 0, 0, 0)),
            pl.BlockSpec((2 * n1, n1), lambda bi: (0, 0)),
            pl.BlockSpec((n1, 2 * GRID_W, 2 * GRID_W), lambda bi: (0, 0, 0)),
            pl.BlockSpec((2 * fw, fw), lambda bi: (0, 0)),
            pl.BlockSpec((fw, fw), lambda bi: (0, 0)),
        ],
        out_specs=pl.BlockSpec((1, GRID_W, n1, fw), lambda bi: (bi, 0, 0, 0)),
        out_shape=jax.ShapeDtypeStruct((b, GRID_W, n1, fw), F32),
        scratch_shapes=[pltpu.VMEM((n1, GRID_W, fw), F32), pltpu.VMEM((n1, GRID_W, fw), F32)],
        compiler_params=_params(1),
        name="fourier",
    )(uf.reshape(b, n1, GRID_W, fw), _table(w1), _table(m2), _table(_channel_dft()), wf_bf)
    return out.reshape(b, s, fw)


def _ctx_fft_kernel(x_ref, wd_ref, cs_ref, wf_ref, o_ref):
    n = x_ref.shape[1]
    x = jnp.dot(wd_ref[...], x_ref[0].astype(BF16), preferred_element_type=F32)
    o_ref[0] = _channel_mix(x[:n], x[n:], cs_ref, wf_ref)


def _ctx_fourier(uf, wf_bf):
    b, n, fw = uf.shape
    k = np.arange(n)
    c, s = _dft_cos_sin(k, k, n)
    wd = np.concatenate([c, -s], axis=0) * n ** -0.5
    return pl.pallas_call(
        _ctx_fft_kernel,
        grid=(b,),
        in_specs=[
            pl.BlockSpec((1, n, fw), lambda bi: (bi, 0, 0)),
            pl.BlockSpec((2 * n, n), lambda bi: (0, 0)),
            pl.BlockSpec((2 * fw, fw), lambda bi: (0, 0)),
            pl.BlockSpec((fw, fw), lambda bi: (0, 0)),
        ],
        out_specs=pl.BlockSpec((1, n, fw), lambda bi: (bi, 0, 0)),
        out_shape=jax.ShapeDtypeStruct((b, n, fw), F32),
        compiler_params=_params(1),
        name="ctx_fourier",
    )(uf, _table(wd), _table(_channel_dft()), wf_bf)


CONV_PAD = 16
CONV_TILE = 32
SUBLANES = 8
PW_TILE = 256


def _conv_kernel(u_ref, gc_ref, dw_ref, db_ref, lg_ref, lb_ref, pw_ref, o_ref, up_ref, y_ref):
    n = u_ref.shape[1]
    zeros = jnp.zeros((CONV_PAD, CONV_W), F32)
    up_ref[0:CONV_PAD, :] = zeros
    up_ref[CONV_PAD + n:CONV_PAD + n + CONV_PAD, :] = zeros
    up_ref[CONV_PAD:CONV_PAD + n, :] = u_ref[0].astype(F32)
    first = CONV_PAD - CONV_K // 2
    halo = CONV_TILE + 2 * CONV_PAD

    def conv_step(it, carry):
        base = pl.multiple_of(it * CONV_TILE, CONV_TILE)
        win = up_ref[pl.ds(base, halo), :]
        acc = jnp.zeros((CONV_TILE, CONV_W), F32)
        for ph in range(SUBLANES):
            taps = [t for t in range(CONV_K) if (first + t) % SUBLANES == ph]
            if not taps:
                continue
            span = (first + taps[-1]) - ph + CONV_TILE
            shifted = win[ph:ph + span]
            for t in taps:
                off = first + t - ph
                acc = acc + shifted[off:off + CONV_TILE] * dw_ref[t:t + 1, :]
        acc = acc + db_ref[...]
        xc = acc - jnp.mean(acc, axis=-1, keepdims=True)
        var = jnp.mean(xc * xc, axis=-1, keepdims=True)
        y = xc * lax.rsqrt(var + NORM_EPS) * lg_ref[...] + lb_ref[...]
        y_ref[pl.ds(base, CONV_TILE), :] = _silu(y).astype(BF16)
        return carry

    lax.fori_loop(0, n // CONV_TILE, conv_step, 0)

    pw_tile = min(PW_TILE, n)

    def pw_step(it, carry):
        base = pl.multiple_of(it * pw_tile, pw_tile)
        z = jnp.dot(y_ref[pl.ds(base, pw_tile), :], pw_ref[...], preferred_element_type=F32)
        o_ref[0, pl.ds(base, pw_tile), :] = (z * gc_ref[0, pl.ds(base, pw_tile), :].astype(F32)).astype(BF16)
        return carry

    lax.fori_loop(0, n // pw_tile, pw_step, 0)


def _conv_module(u, gc, conv_dw_l, conv_db_l, ln_g_l, ln_b_l, w_pw_bf):
    b, n, cw = u.shape
    seq = pl.BlockSpec((1, n, cw), lambda bi: (bi, 0, 0))
    row = pl.BlockSpec((1, cw), lambda bi: (0, 0))
    return pl.pallas_call(
        _conv_kernel,
        grid=(b,),
        in_specs=[seq, seq, pl.BlockSpec((CONV_K, cw), lambda bi: (0, 0)), row, row, row,
                  pl.BlockSpec((cw, cw), lambda bi: (0, 0))],
        out_specs=seq,
        out_shape=jax.ShapeDtypeStruct((b, n, cw), BF16),
        scratch_shapes=[pltpu.VMEM((n + 2 * CONV_PAD, cw), F32), pltpu.VMEM((n, cw), BF16)],
        compiler_params=_params(1),
        name="conv_module",
    )(u, gc, conv_dw_l, conv_db_l.reshape(1, cw), ln_g_l.reshape(1, cw), ln_b_l.reshape(1, cw), w_pw_bf)


def _tail_kernel(x_ref, ao_ref, ga_ref, fw_ref, gf_ref, cw_ref, sa_ref, sf_ref, sc_ref, mod_ref, gp_ref,
                 pa_ref, pf_ref, pc_ref, wo_ref, o_ref):
    d = x_ref.shape[-1]
    ao = jnp.concatenate([ao_ref[0, p] for p in range(N_PAIRS)], axis=-1)
    ya = jnp.dot(ao * ga_ref[0], pa_ref[...], preferred_element_type=F32)
    yf = jnp.dot((fw_ref[0] * gf_ref[0].astype(F32)).astype(BF16), pf_ref[...], preferred_element_type=F32)
    yc = jnp.dot(cw_ref[0], pc_ref[...], preferred_element_type=F32)
    merged = (sa_ref[0].astype(F32) * ya + sf_ref[0].astype(F32) * yf + sc_ref[0].astype(F32) * yc)
    y = jnp.dot(merged.astype(BF16), wo_ref[...], preferred_element_type=F32)
    yn = y * lax.rsqrt(jnp.mean(y * y, axis=-1, keepdims=True) + NORM_EPS) * gp_ref[...]
    gate = mod_ref[0][:, 2 * d:]
    o_ref[0] = x_ref[0] + gate * yn


def _tail(x, ao, ga, fw, gf, cw, sa, sf, sc, mod_l, g_post_l, pa_bf, pf_bf, pc_bf, wo_bf, mod_row, tm):
    b, s, d = x.shape
    tm = min(tm, s)

    def tok(width):
        return pl.BlockSpec((1, tm, width), lambda bi, i: (bi, i, 0))

    def full(shape):
        return pl.BlockSpec(shape, lambda bi, i: (0,) * len(shape))

    return pl.pallas_call(
        _tail_kernel,
        grid=(b, s // tm),
        in_specs=[
            tok(d),
            pl.BlockSpec((1, N_PAIRS, tm, LANES), lambda bi, i: (bi, 0, i, 0)),
            tok(ATTN_W), tok(FOURIER_W), tok(FOURIER_W), tok(CONV_W), tok(d), tok(d), tok(d),
            pl.BlockSpec((1, 1, 3 * d), lambda bi, i: (mod_row(bi), 0, 0)),
            full((1, d)),
            full(pa_bf.shape), full(pf_bf.shape), full(pc_bf.shape), full(wo_bf.shape),
        ],
        out_specs=tok(d),
        out_shape=jax.ShapeDtypeStruct((b, s, d), F32),
        compiler_params=_params(2),
        name="tail",
    )(x, ao, ga, fw, gf, cw, sa, sf, sc, mod_l.reshape(MOD_ROWS, 1, 3 * d), g_post_l.reshape(1, d),
      pa_bf, pf_bf, pc_bf, wo_bf)


INPROJ_TILE = 256
TAIL_TILE = 512


def kernel(x, c, ctx, c_ctx, w_mod, b_mod, g_pre, g_post, w_in, rpb, w_four, conv_dw, conv_db,
           conv_ln_g, conv_ln_b, w_pw, p_attn, p_four, p_conv, w_out):
    batch, seq, d = x.shape
    depth = w_mod.shape[0]
    rows = seq // GRID_W
    assert batch < MOD_ROWS and rows % Q_ROWS == 0 and rows >= K_ROWS
    ctx_row = batch
    c_all = jnp.zeros((MOD_ROWS, d), F32).at[:batch].set(c).at[ctx_row].set(c_ctx)
    mod = _modulation(c_all, w_mod, b_mod)
    merge_sections = _merge_sections(d)

    def latent_row(bi):
        return bi

    def context_row(bi):
        return ctx_row

    for l in range(depth):
        w_in_bf = w_in[l].astype(BF16)
        wf_bf, pw_bf = w_four[l].astype(BF16), w_pw[l].astype(BF16)
        pa_bf, pf_bf, pc_bf, wo_bf = (p_attn[l].astype(BF16), p_four[l].astype(BF16),
                                      p_conv[l].astype(BF16), w_out[l].astype(BF16))
        conv_w = (conv_dw[l], conv_db[l], conv_ln_g[l], conv_ln_b[l], pw_bf)
        tail_w = (mod[l], g_post[l], pa_bf, pf_bf, pc_bf, wo_bf)
        update_ctx = l < depth - 1

        ctx_sections = ALL_SECTIONS + merge_sections if update_ctx else KV_SECTIONS
        ctx_out = dict(zip([sec[0] for sec in ctx_sections],
                           _inproj(ctx, mod[l], g_pre[l], w_in_bf, ctx_sections, context_row, INPROJ_TILE)))

        lat = dict(zip([sec[0] for sec in ALL_SECTIONS + merge_sections],
                       _inproj(x, mod[l], g_pre[l], w_in_bf, ALL_SECTIONS + merge_sections, latent_row,
                               INPROJ_TILE)))
        ao = _attention(lat["q"], lat["k"], lat["v"], ctx_out["k"], ctx_out["v"], _bias_tables(rpb[l], rows))
        fw = _fourier(lat["uf"], wf_bf)
        cw = _conv_module(lat["u"], lat["gc"], *conv_w)
        x = _tail(x, ao, lat["ga"], fw, lat["gf"], cw, lat["sa"], lat["sf"], lat["sc"], *tail_w,
                  latent_row, TAIL_TILE)

        if update_ctx:
            co = ctx_out
            ao_c = _ctx_attention(co["q"], co["k"], co["v"])
            fw_c = _ctx_fourier(co["uf"], wf_bf)
            cw_c = _conv_module(co["u"], co["gc"], *conv_w)
            ctx = _tail(ctx, ao_c, co["ga"], fw_c, co["gf"], cw_c, co["sa"], co["sf"], co["sc"], *tail_w,
                        context_row, TAIL_TILE)
    return x
```

```python
import functools

import jax
import jax.numpy as jnp
import numpy as np
from jax import lax
from jax.experimental import pallas as pl
from jax.experimental.pallas import tpu as pltpu

F32 = jnp.float32
BF16 = jnp.bfloat16

GRID_W = 64
N_HEADS = 8
HEAD_DIM = 64
ATTN_W = N_HEADS * HEAD_DIM
LANES = 128
N_PAIRS = ATTN_W // LANES
WIN_ROWS = 8
WIN_COLS = 16
FOURIER_GROUPS = 4
FOURIER_GROUP_DIM = 64
FOURIER_W = FOURIER_GROUPS * FOURIER_GROUP_DIM
CONV_W = 256
CONV_K = 31
NORM_EPS = 1e-6
MASKED = -1e30

Q_ROWS = 4
K_ROWS = 12
MOD_ROWS = 8

VMEM_LIMIT = 56 * 1024 * 1024


def _sigmoid(x):
    return 1.0 / (1.0 + jnp.exp(-x))


def _silu(x):
    return x * _sigmoid(x)


def _params(n_axes):
    return pltpu.CompilerParams(
        dimension_semantics=("arbitrary",) * n_axes, vmem_limit_bytes=VMEM_LIMIT)


def _mod_kernel(c_ref, w_ref, b_ref, o_ref):
    sc = _silu(c_ref[...]).astype(BF16)
    o_ref[0] = jnp.dot(sc, w_ref[0].astype(BF16), preferred_element_type=F32) + b_ref[0]


def _modulation(c_all, w_mod, b_mod):
    depth, d, d3 = w_mod.shape
    tn = d
    return pl.pallas_call(
        _mod_kernel,
        grid=(depth, d3 // tn),
        in_specs=[
            pl.BlockSpec((MOD_ROWS, d), lambda l, j: (0, 0)),
            pl.BlockSpec((1, d, tn), lambda l, j: (l, 0, j)),
            pl.BlockSpec((1, 1, tn), lambda l, j: (l, 0, j)),
        ],
        out_specs=pl.BlockSpec((1, MOD_ROWS, tn), lambda l, j: (l, 0, j)),
        out_shape=jax.ShapeDtypeStruct((depth, MOD_ROWS, d3), F32),
        compiler_params=_params(2),
        name="modulation",
    )(c_all, w_mod, b_mod.reshape(depth, 1, d3))


_SEC_Q = ("q", 0, ATTN_W, "qscale", ATTN_W, BF16, True)
_SEC_K = ("k", ATTN_W, 2 * ATTN_W, None, ATTN_W, BF16, True)
_SEC_V = ("v", 2 * ATTN_W, 3 * ATTN_W, None, ATTN_W, BF16, True)
_O = 4 * ATTN_W
_SEC_REST = (
    ("ga", 3 * ATTN_W, _O, "silu", ATTN_W, BF16, False),
    ("uf", _O, _O + FOURIER_W, None, FOURIER_W, F32, False),
    ("gf", _O + FOURIER_W, _O + 2 * FOURIER_W, "silu", FOURIER_W, BF16, False),
    ("u", _O + 2 * FOURIER_W, _O + 2 * FOURIER_W + 2 * CONV_W, "glu", CONV_W, BF16, False),
    ("gc", _O + 2 * FOURIER_W + 2 * CONV_W, _O + 2 * FOURIER_W + 3 * CONV_W, "silu", CONV_W, BF16, False),
)
_S0 = _O + 2 * FOURIER_W + 3 * CONV_W
ALL_SECTIONS = (_SEC_Q, _SEC_K, _SEC_V) + _SEC_REST
KV_SECTIONS = (_SEC_K, _SEC_V)


def _merge_sections(d):
    return tuple((n, _S0 + i * d, _S0 + (i + 1) * d, "sigmoid", d, BF16, False)
                 for i, n in enumerate(("sa", "sf", "sc")))


def _inproj_kernel(x_ref, mod_ref, g_ref, w_ref, *o_refs, sections):
    x = x_ref[0]
    d = x.shape[-1]
    m = mod_ref[0]
    shift, scale = m[:, :d], m[:, d:2 * d]
    y = x * lax.rsqrt(jnp.mean(x * x, axis=-1, keepdims=True) + NORM_EPS) * g_ref[...]
    hb = (y * (1.0 + scale) + shift).astype(BF16)
    for (_, lo, hi, act, _, dtype, paired), o_ref in zip(sections, o_refs):
        acc = jnp.dot(hb, w_ref[:, lo:hi], preferred_element_type=F32)
        if act == "qscale":
            acc = acc * (HEAD_DIM ** -0.5)
        elif act == "silu":
            acc = _silu(acc)
        elif act == "sigmoid":
            acc = _sigmoid(acc)
        elif act == "glu":
            half = (hi - lo) // 2
            acc = acc[:, :half] * _sigmoid(acc[:, half:])
        if paired:
            for p in range(N_PAIRS):
                o_ref[0, p] = acc[:, p * LANES:(p + 1) * LANES].astype(dtype)
        else:
            o_ref[0] = acc.astype(dtype)


def _inproj(x, mod_l, g_pre_l, w_in_bf, sections, mod_row, tm):
    b, s, d = x.shape
    tm = min(tm, s)
    in_w = w_in_bf.shape[1]
    out_shapes, out_specs = [], []
    for (_, _, _, _, width, dtype, paired) in sections:
        if paired:
            out_shapes.append(jax.ShapeDtypeStruct((b, N_PAIRS, s, LANES), dtype))
            out_specs.append(pl.BlockSpec((1, N_PAIRS, tm, LANES), lambda bi, i: (bi, 0, i, 0)))
        else:
            out_shapes.append(jax.ShapeDtypeStruct((b, s, width), dtype))
            out_specs.append(pl.BlockSpec((1, tm, width), lambda bi, i: (bi, i, 0)))
    return pl.pallas_call(
        functools.partial(_inproj_kernel, sections=sections),
        grid=(b, s // tm),
        in_specs=[
            pl.BlockSpec((1, tm, d), lambda bi, i: (bi, i, 0)),
            pl.BlockSpec((1, 1, 3 * d), lambda bi, i: (mod_row(bi), 0, 0)),
            pl.BlockSpec((1, d), lambda bi, i: (0, 0)),
            pl.BlockSpec((d, in_w), lambda bi, i: (0, 0)),
        ],
        out_specs=out_specs,
        out_shape=out_shapes,
        compiler_params=_params(2),
        name="inproj",
    )(x, mod_l.reshape(MOD_ROWS, 1, 3 * d), g_pre_l.reshape(1, d), w_in_bf)


PAIR_STARTS = 2 * WIN_ROWS


def _bias_pair_tables(rpb):
    cols = np.arange(GRID_W)
    col_start = np.clip(cols - WIN_COLS // 2, 0, GRID_W - WIN_COLS)
    valid_c = (cols[None, :] >= col_start[:, None]) & (cols[None, :] < col_start[:, None] + WIN_COLS)
    pad = GRID_W - WIN_COLS
    padded = jnp.pad(rpb, ((0, 0), (0, 0), (0, 0), (pad, pad)))
    per_dr = jnp.stack([padded[..., GRID_W - 1 - c:2 * GRID_W - 1 - c] for c in range(GRID_W)], axis=3)
    per_dr = jnp.where(valid_c, per_dr, MASKED)
    masked = jnp.full_like(per_dr[:, :, :1], MASKED)
    ext = jnp.concatenate([masked, per_dr, masked], axis=2)
    return jnp.concatenate([ext[:, :, :-1], ext[:, :, 1:]], axis=-1)


def _window_bias(tab_ref, head, i, ws, rows):
    kr = min(WIN_ROWS, rows)
    low = _head_masks()
    row_blocks = []
    for ri in range(Q_ROWS):
        r = Q_ROWS * i + ri
        rs = jnp.clip(r - kr // 2, 0, rows - kr)
        lane_blocks = []
        for jj in range(K_ROWS // 2):
            krow = ws + 2 * jj
            ok_lo = (krow >= rs) & (krow < rs + kr)
            ok_hi = (krow + 1 >= rs) & (krow + 1 < rs + kr)
            start = jnp.clip(krow - r + WIN_ROWS, 0, PAIR_STARTS - 1)
            tile = tab_ref[head, start]
            ok = jnp.where(low, ok_lo.astype(jnp.int32), ok_hi.astype(jnp.int32))
            lane_blocks.append(jnp.where(ok > 0, tile, MASKED))
        row_blocks.append(jnp.concatenate(lane_blocks, axis=-1))
    return jnp.concatenate(row_blocks, axis=0)


def _head_masks():
    lane = lax.broadcasted_iota(jnp.int32, (1, LANES), 1)
    return lane < HEAD_DIM


def _softmax_pv(q_h, keys, vals, biases):
    dn = (((1,), (1,)), ((), ()))
    scores = []
    for k, bias in zip(keys, biases):
        s = lax.dot_general(q_h, k, dn, preferred_element_type=F32)
        scores.append(s if bias is None else s + bias)
    m = functools.reduce(jnp.maximum, [jnp.max(s, axis=-1, keepdims=True) for s in scores])
    probs = [jnp.exp(s - m) for s in scores]
    denom = functools.reduce(jnp.add, [jnp.sum(p, axis=-1, keepdims=True) for p in probs])
    o = functools.reduce(jnp.add, [jnp.dot(p.astype(BF16), v, preferred_element_type=F32)
                                   for p, v in zip(probs, vals)])
    return o / denom


def _attn_kernel(q_ref, k_ref, v_ref, kc_ref, vc_ref, tab_ref, o_ref, *, rows):
    i = pl.program_id(2)
    ws = jnp.clip(Q_ROWS * i - min(WIN_ROWS, rows) // 2, 0, rows - K_ROWS)
    start = pl.multiple_of(ws * GRID_W, GRID_W)
    q = q_ref[0, 0]
    kw = k_ref[0, 0, pl.ds(start, K_ROWS * GRID_W), :]
    vw = v_ref[0, 0, pl.ds(start, K_ROWS * GRID_W), :]
    kc, vc = kc_ref[0, 0], vc_ref[0, 0]
    low = _head_masks()
    zero = jnp.zeros_like(q)
    o_lo = _softmax_pv(jnp.where(low, q, zero), (kw, kc), (vw, vc),
                       (_window_bias(tab_ref, 0, i, ws, rows), None))
    o_hi = _softmax_pv(jnp.where(low, zero, q), (kw, kc), (vw, vc),
                       (_window_bias(tab_ref, 1, i, ws, rows), None))
    o_ref[0, 0] = jnp.where(low, o_lo, o_hi).astype(o_ref.dtype)


def _attention(q, k, v, kc, vc, tab):
    b, _, s, _ = q.shape
    l = kc.shape[2]
    rows = s // GRID_W
    n_blk = rows // Q_ROWS
    tq = Q_ROWS * GRID_W

    return pl.pallas_call(
        functools.partial(_attn_kernel, rows=rows),
        grid=(b, N_PAIRS, n_blk),
        in_specs=[
            pl.BlockSpec((1, 1, tq, LANES), lambda bi, p, i: (bi, p, i, 0)),
            pl.BlockSpec((1, 1, s, LANES), lambda bi, p, i: (bi, p, 0, 0)),
            pl.BlockSpec((1, 1, s, LANES), lambda bi, p, i: (bi, p, 0, 0)),
            pl.BlockSpec((1, 1, l, LANES), lambda bi, p, i: (bi, p, 0, 0)),
            pl.BlockSpec((1, 1, l, LANES), lambda bi, p, i: (bi, p, 0, 0)),
            pl.BlockSpec((2, PAIR_STARTS, GRID_W, LANES), lambda bi, p, i: (p, 0, 0, 0)),
        ],
        out_specs=pl.BlockSpec((1, 1, tq, LANES), lambda bi, p, i: (bi, p, i, 0)),
        out_shape=jax.ShapeDtypeStruct((b, N_PAIRS, s, LANES), BF16),
        compiler_params=_params(3),
        name="attention",
    )(q, k, v, kc, vc, tab)


def _ctx_attn_kernel(q_ref, k_ref, v_ref, o_ref):
    q, k, v = q_ref[0, 0], k_ref[0, 0], v_ref[0, 0]
    low = _head_masks()
    zero = jnp.zeros_like(q)
    o_lo = _softmax_pv(jnp.where(low, q, zero), (k,), (v,), (None,))
    o_hi = _softmax_pv(jnp.where(low, zero, q), (k,), (v,), (None,))
    o_ref[0, 0] = jnp.where(low, o_lo, o_hi).astype(o_ref.dtype)


def _ctx_attention(q, k, v):
    b, _, l, _ = q.shape
    spec = pl.BlockSpec((1, 1, l, LANES), lambda bi, p: (bi, p, 0, 0))
    return pl.pallas_call(
        _ctx_attn_kernel,
        grid=(b, N_PAIRS),
        in_specs=[spec, spec, spec],
        out_specs=spec,
        out_shape=jax.ShapeDtypeStruct((b, N_PAIRS, l, LANES), BF16),
        compiler_params=_params(2),
        name="ctx_attention",
    )(q, k, v)


def _dft_cos_sin(k, n, period):
    ang = (2.0 * np.pi / period) * ((k[:, None] * n[None, :]) % period)
    return np.cos(ang), np.sin(ang)


def _table(a):
    return jnp.asarray(a, F32).astype(BF16)


def _channel_dft():
    m = np.arange(FOURIER_GROUP_DIM)
    c, s = _dft_cos_sin(m, m, FOURIER_GROUP_DIM)
    eye = np.eye(FOURIER_GROUPS)
    scale = FOURIER_GROUP_DIM ** -0.5
    return np.concatenate([np.kron(eye, c.T), np.kron(eye, s.T)], axis=0) * scale


def _channel_mix(xr, xi, cs_ref, wf_ref):
    x = jnp.concatenate([xr, xi], axis=-1).astype(BF16)
    y = jnp.dot(x, cs_ref[...], preferred_element_type=F32)
    return jnp.dot(y.astype(BF16), wf_ref[...], preferred_element_type=F32)


def _fft_kernel(x_ref, w1_ref, m2_ref, cs_ref, wf_ref, o_ref, ar_ref, ai_ref, *, n1, group):
    for g in range(GRID_W // group):
        xs = jnp.concatenate([x_ref[0, :, g * group + j, :] for j in range(group)], axis=-1)
        a = jnp.dot(w1_ref[...], xs.astype(BF16), preferred_element_type=F32)
        for j in range(group):
            blk = a[:, j * FOURIER_W:(j + 1) * FOURIER_W]
            ar_ref[:, g * group + j, :] = blk[:n1]
            ai_ref[:, g * group + j, :] = blk[n1:]
    for g in range(n1 // group):
        xr, xi = [], []
        for j in range(group):
            k1 = g * group + j
            a = jnp.concatenate([ar_ref[k1], ai_ref[k1]], axis=0).astype(BF16)
            x = jnp.dot(m2_ref[k1], a, preferred_element_type=F32)
            xr.append(x[:GRID_W])
            xi.append(x[GRID_W:])
        z = _channel_mix(jnp.concatenate(xr, axis=0), jnp.concatenate(xi, axis=0), cs_ref, wf_ref)
        for j in range(group):
            o_ref[0, :, g * group + j, :] = z[j * GRID_W:(j + 1) * GRID_W]


def _fourier(uf, wf_bf):
    b, s, fw = uf.shape
    n1 = s // GRID_W
    group = 4
    k1 = np.arange(n1)
    c1, s1 = _dft_cos_sin(k1, k1, n1)
    w1 = np.concatenate([c1, -s1], axis=0) * n1 ** -0.5
    n2 = np.arange(GRID_W)
    kk = k1[:, None] + n1 * np.arange(GRID_W)[None, :]
    ang = (2.0 * np.pi / s) * ((kk[:, :, None] * n2[None, None, :]) % s)
    c2, s2 = np.cos(ang), np.sin(ang)
    m2 = np.concatenate([np.concatenate([c2, s2], axis=2),
                         np.concatenate([-s2, c2], axis=2)], axis=1) * GRID_W ** -0.5
    out = pl.pallas_call(
        functools.partial(_fft_kernel, n1=n1, group=group),
        grid=(b,),
        in_specs=[
            pl.BlockSpec((1, n1, GRID_W, fw), lambda bi: (bi, 0, 0, 0)),
            pl.BlockSpec((2 * n1, n1), lambda bi: (0, 0)),
            pl.BlockSpec((n1, 2 * GRID_W, 2 * GRID_W), lambda bi: (0, 0, 0)),
            pl.BlockSpec((2 * fw, fw), lambda bi: (0, 0)),
            pl.BlockSpec((fw, fw), lambda bi: (0, 0)),
        ],
        out_specs=pl.BlockSpec((1, GRID_W, n1, fw), lambda bi: (bi, 0, 0, 0)),
        out_shape=jax.ShapeDtypeStruct((b, GRID_W, n1, fw), F32),
        scratch_shapes=[pltpu.VMEM((n1, GRID_W, fw), F32), pltpu.VMEM((n1, GRID_W, fw), F32)],
        compiler_params=_params(1),
        name="fourier",
    )(uf.reshape(b, n1, GRID_W, fw), _table(w1), _table(m2), _table(_channel_dft()), wf_bf)
    return out.reshape(b, s, fw)


def _ctx_fft_kernel(x_ref, wd_ref, cs_ref, wf_ref, o_ref):
    n = x_ref.shape[1]
    x = jnp.dot(wd_ref[...], x_ref[0].astype(BF16), preferred_element_type=F32)
    o_ref[0] = _channel_mix(x[:n], x[n:], cs_ref, wf_ref)


def _ctx_fourier(uf, wf_bf):
    b, n, fw = uf.shape
    k = np.arange(n)
    c, s = _dft_cos_sin(k, k, n)
    wd = np.concatenate([c, -s], axis=0) * n ** -0.5
    return pl.pallas_call(
        _ctx_fft_kernel,
        grid=(b,),
        in_specs=[
            pl.BlockSpec((1, n, fw), lambda bi: (bi, 0, 0)),
            pl.BlockSpec((2 * n, n), lambda bi: (0, 0)),
            pl.BlockSpec((2 * fw, fw), lambda bi: (0, 0)),
            pl.BlockSpec((fw, fw), lambda bi: (0, 0)),
        ],
        out_specs=pl.BlockSpec((1, n, fw), lambda bi: (bi, 0, 0)),
        out_shape=jax.ShapeDtypeStruct((b, n, fw), F32),
        compiler_params=_params(1),
        name="ctx_fourier",
    )(uf, _table(wd), _table(_channel_dft()), wf_bf)


CONV_PAD = 16
CONV_TILE = 32
SUBLANES = 8
PW_TILE = 256


def _conv_kernel(u_ref, gc_ref, dw_ref, db_ref, lg_ref, lb_ref, pw_ref, o_ref, up_ref, c_ref):
    n = u_ref.shape[1]
    zeros = jnp.zeros((CONV_PAD, CONV_W), F32)
    up_ref[0:CONV_PAD, :] = zeros
    up_ref[CONV_PAD + n:CONV_PAD + n + CONV_PAD, :] = zeros
    up_ref[CONV_PAD:CONV_PAD + n, :] = u_ref[0].astype(F32)
    first = CONV_PAD - CONV_K // 2
    halo = CONV_TILE + 2 * CONV_PAD

    def conv_step(it, carry):
        base = pl.multiple_of(it * CONV_TILE, CONV_TILE)
        win = up_ref[pl.ds(base, halo), :]
        acc = jnp.zeros((CONV_TILE, CONV_W), F32)
        for ph in range(SUBLANES):
            taps = [t for t in range(CONV_K) if (first + t) % SUBLANES == ph]
            if not taps:
                continue
            shifted = win if ph == 0 else pltpu.roll(win, halo - ph, axis=0)
            for t in taps:
                off = first + t - ph
                acc = acc + shifted[off:off + CONV_TILE] * dw_ref[t:t + 1, :]
        c_ref[pl.ds(base, CONV_TILE), :] = acc + db_ref[...]
        return carry

    lax.fori_loop(0, n // CONV_TILE, conv_step, 0, unroll=2)

    pw_tile = min(PW_TILE, n)

    def pw_step(it, carry):
        base = pl.multiple_of(it * pw_tile, pw_tile)
        acc = c_ref[pl.ds(base, pw_tile), :]
        xc = acc - jnp.mean(acc, axis=-1, keepdims=True)
        var = jnp.mean(xc * xc, axis=-1, keepdims=True)
        y = xc * lax.rsqrt(var + NORM_EPS) * lg_ref[...] + lb_ref[...]
        z = jnp.dot(_silu(y).astype(BF16), pw_ref[...], preferred_element_type=F32)
        o_ref[0, pl.ds(base, pw_tile), :] = (z * gc_ref[0, pl.ds(base, pw_tile), :].astype(F32)).astype(BF16)
        return carry

    lax.fori_loop(0, n // pw_tile, pw_step, 0)


def _conv_module(u, gc, conv_dw_l, conv_db_l, ln_g_l, ln_b_l, w_pw_bf):
    b, n, cw = u.shape
    seq = pl.BlockSpec((1, n, cw), lambda bi: (bi, 0, 0))
    row = pl.BlockSpec((1, cw), lambda bi: (0, 0))
    return pl.pallas_call(
        _conv_kernel,
        grid=(b,),
        in_specs=[seq, seq, pl.BlockSpec((CONV_K, cw), lambda bi: (0, 0)), row, row, row,
                  pl.BlockSpec((cw, cw), lambda bi: (0, 0))],
        out_specs=seq,
        out_shape=jax.ShapeDtypeStruct((b, n, cw), BF16),
        scratch_shapes=[pltpu.VMEM((n + 2 * CONV_PAD, cw), F32), pltpu.VMEM((n, cw), F32)],
        compiler_params=_params(1),
        name="conv_module",
    )(u, gc, conv_dw_l, conv_db_l.reshape(1, cw), ln_g_l.reshape(1, cw), ln_b_l.reshape(1, cw), w_pw_bf)


def _tail_kernel(x_ref, ao_ref, ga_ref, fw_ref, gf_ref, cw_ref, sa_ref, sf_ref, sc_ref, mod_ref, gp_ref,
                 pa_ref, pf_ref, pc_ref, wo_ref, o_ref):
    d = x_ref.shape[-1]
    ao = jnp.concatenate([ao_ref[0, p] for p in range(N_PAIRS)], axis=-1)
    ya = jnp.dot(ao * ga_ref[0], pa_ref[...], preferred_element_type=F32)
    yf = jnp.dot((fw_ref[0] * gf_ref[0].astype(F32)).astype(BF16), pf_ref[...], preferred_element_type=F32)
    yc = jnp.dot(cw_ref[0], pc_ref[...], preferred_element_type=F32)
    merged = (sa_ref[0].astype(F32) * ya + sf_ref[0].astype(F32) * yf + sc_ref[0].astype(F32) * yc)
    y = jnp.dot(merged.astype(BF16), wo_ref[...], preferred_element_type=F32)
    yn = y * lax.rsqrt(jnp.mean(y * y, axis=-1, keepdims=True) + NORM_EPS) * gp_ref[...]
    gate = mod_ref[0][:, 2 * d:]
    o_ref[0] = x_ref[0] + gate * yn


def _tail(x, ao, ga, fw, gf, cw, sa, sf, sc, mod_l, g_post_l, pa_bf, pf_bf, pc_bf, wo_bf, mod_row, tm):
    b, s, d = x.shape
    tm = min(tm, s)

    def tok(width):
        return pl.BlockSpec((1, tm, width), lambda bi, i: (bi, i, 0))

    def full(shape):
        return pl.BlockSpec(shape, lambda bi, i: (0,) * len(shape))

    return pl.pallas_call(
        _tail_kernel,
        grid=(b, s // tm),
        in_specs=[
            tok(d),
            pl.BlockSpec((1, N_PAIRS, tm, LANES), lambda bi, i: (bi, 0, i, 0)),
            tok(ATTN_W), tok(FOURIER_W), tok(FOURIER_W), tok(CONV_W), tok(d), tok(d), tok(d),
            pl.BlockSpec((1, 1, 3 * d), lambda bi, i: (mod_row(bi), 0, 0)),
            full((1, d)),
            full(pa_bf.shape), full(pf_bf.shape), full(pc_bf.shape), full(wo_bf.shape),
        ],
        out_specs=tok(d),
        out_shape=jax.ShapeDtypeStruct((b, s, d), F32),
        compiler_params=_params(2),
        name="tail",
    )(x, ao, ga, fw, gf, cw, sa, sf, sc, mod_l.reshape(MOD_ROWS, 1, 3 * d), g_post_l.reshape(1, d),
      pa_bf, pf_bf, pc_bf, wo_bf)


INPROJ_TILE = 256
TAIL_TILE = 512


def kernel(x, c, ctx, c_ctx, w_mod, b_mod, g_pre, g_post, w_in, rpb, w_four, conv_dw, conv_db,
           conv_ln_g, conv_ln_b, w_pw, p_attn, p_four, p_conv, w_out):
    batch, seq, d = x.shape
    depth = w_mod.shape[0]
    rows = seq // GRID_W
    assert batch < MOD_ROWS and rows % Q_ROWS == 0 and rows >= K_ROWS
    ctx_row = batch
    c_all = jnp.zeros((MOD_ROWS, d), F32).at[:batch].set(c).at[ctx_row].set(c_ctx)
    mod = _modulation(c_all, w_mod, b_mod)
    merge_sections = _merge_sections(d)
    bias_tabs = _bias_pair_tables(rpb)

    def latent_row(bi):
        return bi

    def context_row(bi):
        return ctx_row

    for l in range(depth):
        w_in_bf = w_in[l].astype(BF16)
        wf_bf, pw_bf = w_four[l].astype(BF16), w_pw[l].astype(BF16)
        pa_bf, pf_bf, pc_bf, wo_bf = (p_attn[l].astype(BF16), p_four[l].astype(BF16),
                                      p_conv[l].astype(BF16), w_out[l].astype(BF16))
        conv_w = (conv_dw[l], conv_db[l], conv_ln_g[l], conv_ln_b[l], pw_bf)
        tail_w = (mod[l], g_post[l], pa_bf, pf_bf, pc_bf, wo_bf)
        update_ctx = l < depth - 1

        ctx_sections = ALL_SECTIONS + merge_sections if update_ctx else KV_SECTIONS
        ctx_out = dict(zip([sec[0] for sec in ctx_sections],
                           _inproj(ctx, mod[l], g_pre[l], w_in_bf, ctx_sections, context_row, INPROJ_TILE)))

        lat = dict(zip([sec[0] for sec in ALL_SECTIONS + merge_sections],
                       _inproj(x, mod[l], g_pre[l], w_in_bf, ALL_SECTIONS + merge_sections, latent_row,
                               INPROJ_TILE)))
        ao = _attention(lat["q"], lat["k"], lat["v"], ctx_out["k"], ctx_out["v"], bias_tabs[l])
        fw = _fourier(lat["uf"], wf_bf)
        cw = _conv_module(lat["u"], lat["gc"], *conv_w)
        x = _tail(x, ao, lat["ga"], fw, lat["gf"], cw, lat["sa"], lat["sf"], lat["sc"], *tail_w,
                  latent_row, TAIL_TILE)

        if update_ctx:
            co = ctx_out
            ao_c = _ctx_attention(co["q"], co["k"], co["v"])
            fw_c = _ctx_fourier(co["uf"], wf_bf)
            cw_c = _conv_module(co["u"], co["gc"], *conv_w)
            ctx = _tail(ctx, ao_c, co["ga"], fw_c, co["gf"], cw_c, co["sa"], co["sf"], co["sc"], *tail_w,
                        context_row, TAIL_TILE)
    return x
```

```python
import functools

import jax
import jax.numpy as jnp
import numpy as np
from jax import lax
from jax.experimental import pallas as pl
from jax.experimental.pallas import tpu as pltpu

F32 = jnp.float32
BF16 = jnp.bfloat16

GRID_W = 64
N_HEADS = 8
HEAD_DIM = 64
ATTN_W = N_HEADS * HEAD_DIM
LANES = 128
N_PAIRS = ATTN_W // LANES
WIN_ROWS = 8
WIN_COLS = 16
FOURIER_GROUPS = 4
FOURIER_GROUP_DIM = 64
FOURIER_W = FOURIER_GROUPS * FOURIER_GROUP_DIM
CONV_W = 256
CONV_K = 31
NORM_EPS = 1e-6
MASKED = -1e30
LOG2E = 1.4426950408889634

Q_ROWS = 4
K_ROWS = 12
MOD_ROWS = 8

VMEM_LIMIT = 56 * 1024 * 1024


def _sigmoid(x):
    return 1.0 / (1.0 + jnp.exp(-x))


def _silu(x):
    return x * _sigmoid(x)


def _params(n_axes, flags=None):
    return pltpu.CompilerParams(
        dimension_semantics=("arbitrary",) * n_axes, vmem_limit_bytes=VMEM_LIMIT, flags=flags)


def _mod_kernel(c_ref, w_ref, b_ref, o_ref):
    sc = _silu(c_ref[...]).astype(BF16)
    o_ref[0] = jnp.dot(sc, w_ref[0].astype(BF16), preferred_element_type=F32) + b_ref[0]


def _modulation(c_all, w_mod, b_mod):
    depth, d, d3 = w_mod.shape
    tn = d
    return pl.pallas_call(
        _mod_kernel,
        grid=(depth, d3 // tn),
        in_specs=[
            pl.BlockSpec((MOD_ROWS, d), lambda l, j: (0, 0)),
            pl.BlockSpec((1, d, tn), lambda l, j: (l, 0, j)),
            pl.BlockSpec((1, 1, tn), lambda l, j: (l, 0, j)),
        ],
        out_specs=pl.BlockSpec((1, MOD_ROWS, tn), lambda l, j: (l, 0, j)),
        out_shape=jax.ShapeDtypeStruct((depth, MOD_ROWS, d3), F32),
        compiler_params=_params(2),
        name="modulation",
    )(c_all, w_mod, b_mod.reshape(depth, 1, d3))


_SEC_Q = ("q", 0, ATTN_W, "qscale", ATTN_W, BF16, True)
_SEC_K = ("k", ATTN_W, 2 * ATTN_W, None, ATTN_W, BF16, True)
_SEC_V = ("v", 2 * ATTN_W, 3 * ATTN_W, None, ATTN_W, BF16, True)
_O = 4 * ATTN_W
_SEC_REST = (
    ("ga", 3 * ATTN_W, _O, "silu", ATTN_W, BF16, False),
    ("uf", _O, _O + FOURIER_W, None, FOURIER_W, F32, False),
    ("gf", _O + FOURIER_W, _O + 2 * FOURIER_W, "silu", FOURIER_W, BF16, False),
    ("u", _O + 2 * FOURIER_W, _O + 2 * FOURIER_W + 2 * CONV_W, "glu", CONV_W, BF16, False),
    ("gc", _O + 2 * FOURIER_W + 2 * CONV_W, _O + 2 * FOURIER_W + 3 * CONV_W, "silu", CONV_W, BF16, False),
)
_S0 = _O + 2 * FOURIER_W + 3 * CONV_W
ALL_SECTIONS = (_SEC_Q, _SEC_K, _SEC_V) + _SEC_REST
KV_SECTIONS = (_SEC_K, _SEC_V)


def _merge_sections(d):
    return tuple((n, _S0 + i * d, _S0 + (i + 1) * d, "sigmoid", d, BF16, False)
                 for i, n in enumerate(("sa", "sf", "sc")))


def _inproj_kernel(x_ref, mod_ref, g_ref, w_ref, *o_refs, sections):
    x = x_ref[0]
    d = x.shape[-1]
    m = mod_ref[0]
    shift, scale = m[:, :d], m[:, d:2 * d]
    y = x * lax.rsqrt(jnp.mean(x * x, axis=-1, keepdims=True) + NORM_EPS) * g_ref[...]
    hb = (y * (1.0 + scale) + shift).astype(BF16)
    for (_, lo, hi, act, _, dtype, paired), o_ref in zip(sections, o_refs):
        acc = jnp.dot(hb, w_ref[:, lo:hi], preferred_element_type=F32)
        if act == "qscale":
            acc = acc * (LOG2E * HEAD_DIM ** -0.5)
        elif act == "silu":
            acc = _silu(acc)
        elif act == "sigmoid":
            acc = _sigmoid(acc)
        elif act == "glu":
            half = (hi - lo) // 2
            acc = acc[:, :half] * _sigmoid(acc[:, half:])
        if paired:
            for p in range(N_PAIRS):
                o_ref[0, p] = acc[:, p * LANES:(p + 1) * LANES].astype(dtype)
        else:
            o_ref[0] = acc.astype(dtype)


def _inproj(x, mod_l, g_pre_l, w_in_bf, sections, mod_row, tm):
    b, s, d = x.shape
    tm = min(tm, s)
    in_w = w_in_bf.shape[1]
    out_shapes, out_specs = [], []
    for (_, _, _, _, width, dtype, paired) in sections:
        if paired:
            out_shapes.append(jax.ShapeDtypeStruct((b, N_PAIRS, s, LANES), dtype))
            out_specs.append(pl.BlockSpec((1, N_PAIRS, tm, LANES), lambda bi, i: (bi, 0, i, 0)))
        else:
            out_shapes.append(jax.ShapeDtypeStruct((b, s, width), dtype))
            out_specs.append(pl.BlockSpec((1, tm, width), lambda bi, i: (bi, i, 0)))
    return pl.pallas_call(
        functools.partial(_inproj_kernel, sections=sections),
        grid=(b, s // tm),
        in_specs=[
            pl.BlockSpec((1, tm, d), lambda bi, i: (bi, i, 0)),
            pl.BlockSpec((1, 1, 3 * d), lambda bi, i: (mod_row(bi), 0, 0)),
            pl.BlockSpec((1, d), lambda bi, i: (0, 0)),
            pl.BlockSpec((d, in_w), lambda bi, i: (0, 0)),
        ],
        out_specs=out_specs,
        out_shape=out_shapes,
        compiler_params=_params(2),
        name="inproj",
    )(x, mod_l.reshape(MOD_ROWS, 1, 3 * d), g_pre_l.reshape(1, d), w_in_bf)


PAIR_STARTS = 2 * WIN_ROWS


def _bias_pair_tables(rpb):
    cols = np.arange(GRID_W)
    col_start = np.clip(cols - WIN_COLS // 2, 0, GRID_W - WIN_COLS)
    valid_c = (cols[None, :] >= col_start[:, None]) & (cols[None, :] < col_start[:, None] + WIN_COLS)
    pad = GRID_W - WIN_COLS
    padded = jnp.pad(rpb, ((0, 0), (0, 0), (0, 0), (pad, pad)))
    per_dr = jnp.stack([padded[..., GRID_W - 1 - c:2 * GRID_W - 1 - c] for c in range(GRID_W)], axis=3)
    per_dr = jnp.where(valid_c, per_dr * LOG2E, MASKED)
    masked = jnp.full_like(per_dr[:, :, :1], MASKED)
    ext = jnp.concatenate([masked, per_dr, masked], axis=2)
    return jnp.concatenate([ext[:, :, :-1], ext[:, :, 1:]], axis=-1)


def _window_plan(i, ws, rows):
    kr = min(WIN_ROWS, rows)
    low = _head_masks()
    plan = []
    for ri in range(Q_ROWS):
        r = Q_ROWS * i + ri
        rs = jnp.clip(r - kr // 2, 0, rows - kr)
        for jj in range(K_ROWS // 2):
            krow = ws + 2 * jj
            ok_lo = (krow >= rs) & (krow < rs + kr)
            ok_hi = (krow + 1 >= rs) & (krow + 1 < rs + kr)
            ok = jnp.where(low, ok_lo.astype(jnp.int32), ok_hi.astype(jnp.int32)) > 0
            start = jnp.clip(krow - r + WIN_ROWS, 0, PAIR_STARTS - 1)
            plan.append((ok, start))
    return plan


def _window_bias(tab_ref, head, plan):
    n_jj = K_ROWS // 2
    row_blocks = []
    for ri in range(Q_ROWS):
        lane_blocks = [jnp.where(ok, tab_ref[head, start], MASKED) for ok, start in plan[ri * n_jj:(ri + 1) * n_jj]]
        row_blocks.append(jnp.concatenate(lane_blocks, axis=-1))
    return jnp.concatenate(row_blocks, axis=0)


def _head_masks():
    lane = lax.broadcasted_iota(jnp.int32, (1, LANES), 1)
    return lane < HEAD_DIM


def _pick(head, own, other):
    low = _head_masks()
    return jnp.where(low, own, other) if head == 0 else jnp.where(low, other, own)


def _softmax_pv(head, q, keys, vals, biases):
    return _pv(head, *_scores(head, q, keys, biases), vals)


def _scores(head, q, keys, biases):
    dn = (((1,), (1,)), ((), ()))
    q_h = _pick(head, q, jnp.zeros_like(q))
    scores = []
    for k, bias in zip(keys, biases):
        s = lax.dot_general(q_h, k, dn, preferred_element_type=F32)
        scores.append(s if bias is None else s + bias)
    m = functools.reduce(jnp.maximum, [jnp.max(s, axis=-1, keepdims=True) for s in scores])
    return scores, m


def _pv(head, scores, m, vals):
    o = functools.reduce(jnp.add, [
        jnp.dot(jnp.exp2(s - m).astype(BF16), _pick(head, v, jnp.ones_like(v)), preferred_element_type=F32)
        for s, v in zip(scores, vals)])
    return o / pltpu.roll(o, HEAD_DIM, axis=1)


def _attn_kernel(q_ref, k_ref, v_ref, kc_ref, vc_ref, tab_ref, o_ref, *, rows, pairs):
    i = pl.program_id(2)
    ws = jnp.clip(Q_ROWS * i - min(WIN_ROWS, rows) // 2, 0, rows - K_ROWS)
    start = pl.multiple_of(ws * GRID_W, GRID_W)
    plan = _window_plan(i, ws, rows)
    heads = [(p, head) for p in range(pairs) for head in (0, 1)]
    staged = []
    for p, head in heads:
        q = q_ref[0, p]
        kw = k_ref[0, p, pl.ds(start, K_ROWS * GRID_W), :]
        staged.append(_scores(head, q, (kw, kc_ref[0, p]), (_window_bias(tab_ref, 2 * p + head, plan), None)))
    outs = []
    for (p, head), (scores, m) in zip(heads, staged):
        vw = v_ref[0, p, pl.ds(start, K_ROWS * GRID_W), :]
        outs.append(_pv(head, scores, m, (vw, vc_ref[0, p])))
    for p in range(pairs):
        o_ref[0, p] = _pick(0, outs[2 * p], outs[2 * p + 1]).astype(o_ref.dtype)


ATTN_PAIRS_PER_STEP = 4


def _attention(q, k, v, kc, vc, tab):
    b, _, s, _ = q.shape
    l = kc.shape[2]
    rows = s // GRID_W
    n_blk = rows // Q_ROWS
    tq = Q_ROWS * GRID_W
    pp = ATTN_PAIRS_PER_STEP

    return pl.pallas_call(
        functools.partial(_attn_kernel, rows=rows, pairs=pp),
        grid=(b, N_PAIRS // pp, n_blk),
        in_specs=[
            pl.BlockSpec((1, pp, tq, LANES), lambda bi, p, i: (bi, p, i, 0)),
            pl.BlockSpec((1, pp, s, LANES), lambda bi, p, i: (bi, p, 0, 0)),
            pl.BlockSpec((1, pp, s, LANES), lambda bi, p, i: (bi, p, 0, 0)),
            pl.BlockSpec((1, pp, l, LANES), lambda bi, p, i: (bi, p, 0, 0)),
            pl.BlockSpec((1, pp, l, LANES), lambda bi, p, i: (bi, p, 0, 0)),
            pl.BlockSpec((2 * pp, PAIR_STARTS, GRID_W, LANES), lambda bi, p, i: (p, 0, 0, 0)),
        ],
        out_specs=pl.BlockSpec((1, pp, tq, LANES), lambda bi, p, i: (bi, p, i, 0)),
        out_shape=jax.ShapeDtypeStruct((b, N_PAIRS, s, LANES), BF16),
        compiler_params=_params(3),
        name="attention",
    )(q, k, v, kc, vc, tab)


def _ctx_attn_kernel(q_ref, k_ref, v_ref, o_ref):
    q, k, v = q_ref[0, 0], k_ref[0, 0], v_ref[0, 0]
    o_lo, o_hi = [_softmax_pv(head, q, (k,), (v,), (None,)) for head in (0, 1)]
    o_ref[0, 0] = _pick(0, o_lo, o_hi).astype(o_ref.dtype)


def _ctx_attention(q, k, v):
    b, _, l, _ = q.shape
    spec = pl.BlockSpec((1, 1, l, LANES), lambda bi, p: (bi, p, 0, 0))
    return pl.pallas_call(
        _ctx_attn_kernel,
        grid=(b, N_PAIRS),
        in_specs=[spec, spec, spec],
        out_specs=spec,
        out_shape=jax.ShapeDtypeStruct((b, N_PAIRS, l, LANES), BF16),
        compiler_params=_params(2),
        name="ctx_attention",
    )(q, k, v)


def _dft_cos_sin(k, n, period):
    ang = (2.0 * np.pi / period) * ((k[:, None] * n[None, :]) % period)
    return np.cos(ang), np.sin(ang)


def _table(a):
    return jnp.asarray(a, F32).astype(BF16)


def _channel_dft():
    m = np.arange(FOURIER_GROUP_DIM)
    c, s = _dft_cos_sin(m, m, FOURIER_GROUP_DIM)
    eye = np.eye(FOURIER_GROUPS)
    scale = FOURIER_GROUP_DIM ** -0.5
    return np.concatenate([np.kron(eye, c.T), np.kron(eye, s.T)], axis=0) * scale


def _channel_mix(xr, xi, cs_ref, wf_ref):
    x = jnp.concatenate([xr, xi], axis=-1).astype(BF16)
    y = jnp.dot(x, cs_ref[...], preferred_element_type=F32)
    return jnp.dot(y.astype(BF16), wf_ref[...], preferred_element_type=F32)


def _fft_kernel(x_ref, w1_ref, m2_ref, cs_ref, wf_ref, o_ref, ar_ref, ai_ref, *, n1, group):
    for g in range(GRID_W // group):
        xs = jnp.concatenate([x_ref[0, :, g * group + j, :] for j in range(group)], axis=-1)
        a = jnp.dot(w1_ref[...], xs.astype(BF16), preferred_element_type=F32)
        for j in range(group):
            blk = a[:, j * FOURIER_W:(j + 1) * FOURIER_W]
            ar_ref[:, g * group + j, :] = blk[:n1]
            ai_ref[:, g * group + j, :] = blk[n1:]
    for g in range(n1 // group):
        xr, xi = [], []
        for j in range(group):
            k1 = g * group + j
            a = jnp.concatenate([ar_ref[k1], ai_ref[k1]], axis=0).astype(BF16)
            x = jnp.dot(m2_ref[k1], a, preferred_element_type=F32)
            xr.append(x[:GRID_W])
            xi.append(x[GRID_W:])
        z = _channel_mix(jnp.concatenate(xr, axis=0), jnp.concatenate(xi, axis=0), cs_ref, wf_ref)
        for j in range(group):
            o_ref[0, :, g * group + j, :] = z[j * GRID_W:(j + 1) * GRID_W]


def _fourier(uf, wf_bf):
    b, s, fw = uf.shape
    n1 = s // GRID_W
    group = 4
    k1 = np.arange(n1)
    c1, s1 = _dft_cos_sin(k1, k1, n1)
    w1 = np.concatenate([c1, -s1], axis=0) * n1 ** -0.5
    n2 = np.arange(GRID_W)
    kk = k1[:, None] + n1 * np.arange(GRID_W)[None, :]
    ang = (2.0 * np.pi / s) * ((kk[:, :, None] * n2[None, None, :]) % s)
    c2, s2 = np.cos(ang), np.sin(ang)
    m2 = np.concatenate([np.concatenate([c2, s2], axis=2),
                         np.concatenate([-s2, c2], axis=2)], axis=1) * GRID_W ** -0.5
    out = pl.pallas_call(
        functools.partial(_fft_kernel, n1=n1, group=group),
        grid=(b,),
        in_specs=[
            pl.BlockSpec((1, n1, GRID_W, fw), lambda bi: (bi, 0, 0, 0)),
            pl.BlockSpec((2 * n1, n1), lambda bi: (0, 0)),
            pl.BlockSpec((n1, 2 * GRID_W, 2 * GRID_W), lambda bi: (0, 0, 0)),
            pl.BlockSpec((2 * fw, fw), lambda bi: (0, 0)),
            pl.BlockSpec((fw, fw), lambda bi: (0, 0)),
        ],
        out_specs=pl.BlockSpec((1, GRID_W, n1, fw), lambda bi: (bi, 0, 0, 0)),
        out_shape=jax.ShapeDtypeStruct((b, GRID_W, n1, fw), F32),
        scratch_shapes=[pltpu.VMEM((n1, GRID_W, fw), F32), pltpu.VMEM((n1, GRID_W, fw), F32)],
        compiler_params=_params(1),
        name="fourier",
    )(uf.reshape(b, n1, GRID_W, fw), _table(w1), _table(m2), _table(_channel_dft()), wf_bf)
    return out.reshape(b, s, fw)


def _ctx_fft_kernel(x_ref, wd_ref, cs_ref, wf_ref, o_ref):
    n = x_ref.shape[1]
    x = jnp.dot(wd_ref[...], x_ref[0].astype(BF16), preferred_element_type=F32)
    o_ref[0] = _channel_mix(x[:n], x[n:], cs_ref, wf_ref)


def _ctx_fourier(uf, wf_bf):
    b, n, fw = uf.shape
    k = np.arange(n)
    c, s = _dft_cos_sin(k, k, n)
    wd = np.concatenate([c, -s], axis=0) * n ** -0.5
    return pl.pallas_call(
        _ctx_fft_kernel,
        grid=(b,),
        in_specs=[
            pl.BlockSpec((1, n, fw), lambda bi: (bi, 0, 0)),
            pl.BlockSpec((2 * n, n), lambda bi: (0, 0)),
            pl.BlockSpec((2 * fw, fw), lambda bi: (0, 0)),
            pl.BlockSpec((fw, fw), lambda bi: (0, 0)),
        ],
        out_specs=pl.BlockSpec((1, n, fw), lambda bi: (bi, 0, 0)),
        out_shape=jax.ShapeDtypeStruct((b, n, fw), F32),
        compiler_params=_params(1),
        name="ctx_fourier",
    )(uf, _table(wd), _table(_channel_dft()), wf_bf)


CONV_PAD = 16
CONV_TILE = 32
SUBLANES = 8
PW_TILE = 256


def _conv_kernel(u_ref, gc_ref, dw_ref, db_ref, lg_ref, lb_ref, pw_ref, o_ref, up_ref, c_ref):
    n = u_ref.shape[1]
    zeros = jnp.zeros((CONV_PAD, CONV_W), F32)
    up_ref[0:CONV_PAD, :] = zeros
    up_ref[CONV_PAD + n:CONV_PAD + n + CONV_PAD, :] = zeros
    up_ref[CONV_PAD:CONV_PAD + n, :] = u_ref[0].astype(F32)
    first = CONV_PAD - CONV_K // 2
    halo = CONV_TILE + 2 * CONV_PAD

    def conv_step(it, carry):
        base = pl.multiple_of(it * CONV_TILE, CONV_TILE)
        win = up_ref[pl.ds(base, halo), :]
        acc = jnp.zeros((CONV_TILE, CONV_W), F32)
        for ph in range(SUBLANES):
            taps = [t for t in range(CONV_K) if (first + t) % SUBLANES == ph]
            if not taps:
                continue
            shifted = win if ph == 0 else pltpu.roll(win, halo - ph, axis=0)
            for t in taps:
                off = first + t - ph
                acc = acc + shifted[off:off + CONV_TILE] * dw_ref[t:t + 1, :]
        c_ref[pl.ds(base, CONV_TILE), :] = acc + db_ref[...]
        return carry

    lax.fori_loop(0, n // CONV_TILE, conv_step, 0, unroll=2)

    pw_tile = min(PW_TILE, n)

    def pw_step(it, carry):
        base = pl.multiple_of(it * pw_tile, pw_tile)
        acc = c_ref[pl.ds(base, pw_tile), :]
        xc = acc - jnp.mean(acc, axis=-1, keepdims=True)
        var = jnp.mean(xc * xc, axis=-1, keepdims=True)
        y = xc * lax.rsqrt(var + NORM_EPS) * lg_ref[...] + lb_ref[...]
        z = jnp.dot(_silu(y).astype(BF16), pw_ref[...], preferred_element_type=F32)
        o_ref[0, pl.ds(base, pw_tile), :] = (z * gc_ref[0, pl.ds(base, pw_tile), :].astype(F32)).astype(BF16)
        return carry

    lax.fori_loop(0, n // pw_tile, pw_step, 0)


def _conv_module(u, gc, conv_dw_l, conv_db_l, ln_g_l, ln_b_l, w_pw_bf):
    b, n, cw = u.shape
    seq = pl.BlockSpec((1, n, cw), lambda bi: (bi, 0, 0))
    row = pl.BlockSpec((1, cw), lambda bi: (0, 0))
    return pl.pallas_call(
        _conv_kernel,
        grid=(b,),
        in_specs=[seq, seq, pl.BlockSpec((CONV_K, cw), lambda bi: (0, 0)), row, row, row,
                  pl.BlockSpec((cw, cw), lambda bi: (0, 0))],
        out_specs=seq,
        out_shape=jax.ShapeDtypeStruct((b, n, cw), BF16),
        scratch_shapes=[pltpu.VMEM((n + 2 * CONV_PAD, cw), F32), pltpu.VMEM((n, cw), F32)],
        compiler_params=_params(1),
        name="conv_module",
    )(u, gc, conv_dw_l, conv_db_l.reshape(1, cw), ln_g_l.reshape(1, cw), ln_b_l.reshape(1, cw), w_pw_bf)


def _tail_kernel(x_ref, ao_ref, ga_ref, fw_ref, gf_ref, cw_ref, sa_ref, sf_ref, sc_ref, mod_ref, gp_ref,
                 pa_ref, pf_ref, pc_ref, wo_ref, o_ref):
    d = x_ref.shape[-1]
    ao = jnp.concatenate([ao_ref[0, p] for p in range(N_PAIRS)], axis=-1)
    ya = jnp.dot(ao * ga_ref[0], pa_ref[...], preferred_element_type=F32)
    yf = jnp.dot((fw_ref[0] * gf_ref[0].astype(F32)).astype(BF16), pf_ref[...], preferred_element_type=F32)
    yc = jnp.dot(cw_ref[0], pc_ref[...], preferred_element_type=F32)
    merged = (sa_ref[0].astype(F32) * ya + sf_ref[0].astype(F32) * yf + sc_ref[0].astype(F32) * yc)
    y = jnp.dot(merged.astype(BF16), wo_ref[...], preferred_element_type=F32)
    yn = y * lax.rsqrt(jnp.mean(y * y, axis=-1, keepdims=True) + NORM_EPS) * gp_ref[...]
    gate = mod_ref[0][:, 2 * d:]
    o_ref[0] = x_ref[0] + gate * yn


def _tail(x, ao, ga, fw, gf, cw, sa, sf, sc, mod_l, g_post_l, pa_bf, pf_bf, pc_bf, wo_bf, mod_row, tm):
    b, s, d = x.shape
    tm = min(tm, s)

    def tok(width):
        return pl.BlockSpec((1, tm, width), lambda bi, i: (bi, i, 0))

    def full(shape):
        return pl.BlockSpec(shape, lambda bi, i: (0,) * len(shape))

    return pl.pallas_call(
        _tail_kernel,
        grid=(b, s // tm),
        in_specs=[
            tok(d),
            pl.BlockSpec((1, N_PAIRS, tm, LANES), lambda bi, i: (bi, 0, i, 0)),
            tok(ATTN_W), tok(FOURIER_W), tok(FOURIER_W), tok(CONV_W), tok(d), tok(d), tok(d),
            pl.BlockSpec((1, 1, 3 * d), lambda bi, i: (mod_row(bi), 0, 0)),
            full((1, d)),
            full(pa_bf.shape), full(pf_bf.shape), full(pc_bf.shape), full(wo_bf.shape),
        ],
        out_specs=tok(d),
        out_shape=jax.ShapeDtypeStruct((b, s, d), F32),
        compiler_params=_params(2),
        name="tail",
    )(x, ao, ga, fw, gf, cw, sa, sf, sc, mod_l.reshape(MOD_ROWS, 1, 3 * d), g_post_l.reshape(1, d),
      pa_bf, pf_bf, pc_bf, wo_bf)


INPROJ_TILE = 256
TAIL_TILE = 512


def kernel(x, c, ctx, c_ctx, w_mod, b_mod, g_pre, g_post, w_in, rpb, w_four, conv_dw, conv_db,
           conv_ln_g, conv_ln_b, w_pw, p_attn, p_four, p_conv, w_out):
    batch, seq, d = x.shape
    depth = w_mod.shape[0]
    rows = seq // GRID_W
    assert batch < MOD_ROWS and rows % Q_ROWS == 0 and rows >= K_ROWS
    ctx_row = batch
    c_all = jnp.zeros((MOD_ROWS, d), F32).at[:batch].set(c).at[ctx_row].set(c_ctx)
    mod = _modulation(c_all, w_mod, b_mod)
    merge_sections = _merge_sections(d)
    bias_tabs = _bias_pair_tables(rpb)

    def latent_row(bi):
        return bi

    def context_row(bi):
        return ctx_row

    for l in range(depth):
        w_in_bf = w_in[l].astype(BF16)
        wf_bf, pw_bf = w_four[l].astype(BF16), w_pw[l].astype(BF16)
        pa_bf, pf_bf, pc_bf, wo_bf = (p_attn[l].astype(BF16), p_four[l].astype(BF16),
                                      p_conv[l].astype(BF16), w_out[l].astype(BF16))
        conv_w = (conv_dw[l], conv_db[l], conv_ln_g[l], conv_ln_b[l], pw_bf)
        tail_w = (mod[l], g_post[l], pa_bf, pf_bf, pc_bf, wo_bf)
        update_ctx = l < depth - 1

        ctx_sections = ALL_SECTIONS + merge_sections if update_ctx else KV_SECTIONS
        ctx_out = dict(zip([sec[0] for sec in ctx_sections],
                           _inproj(ctx, mod[l], g_pre[l], w_in_bf, ctx_sections, context_row, INPROJ_TILE)))

        lat = dict(zip([sec[0] for sec in ALL_SECTIONS + merge_sections],
                       _inproj(x, mod[l], g_pre[l], w_in_bf, ALL_SECTIONS + merge_sections, latent_row,
                               INPROJ_TILE)))
        ao = _attention(lat["q"], lat["k"], lat["v"], ctx_out["k"], ctx_out["v"], bias_tabs[l])
        fw = _fourier(lat["uf"], wf_bf)
        cw = _conv_module(lat["u"], lat["gc"], *conv_w)
        x = _tail(x, ao, lat["ga"], fw, lat["gf"], cw, lat["sa"], lat["sf"], lat["sc"], *tail_w,
                  latent_row, TAIL_TILE)

        if update_ctx:
            co = ctx_out
            ao_c = _ctx_attention(co["q"], co["k"], co["v"])
            fw_c = _ctx_fourier(co["uf"], wf_bf)
            cw_c = _conv_module(co["u"], co["gc"], *conv_w)
            ctx = _tail(ctx, ao_c, co["ga"], fw_c, co["gf"], cw_c, co["sa"], co["sf"], co["sc"], *tail_w,
                        context_row, TAIL_TILE)
    return x
```

```python
import functools

import jax
import jax.numpy as jnp
import numpy as np
from jax import lax
from jax.experimental import pallas as pl
from jax.experimental.pallas import tpu as pltpu

F32 = jnp.float32
BF16 = jnp.bfloat16

GRID_W = 64
N_HEADS = 8
HEAD_DIM = 64
ATTN_W = N_HEADS * HEAD_DIM
LANES = 128
N_PAIRS = ATTN_W // LANES
WIN_ROWS = 8
WIN_COLS = 16
FOURIER_GROUPS = 4
FOURIER_GROUP_DIM = 64
FOURIER_W = FOURIER_GROUPS * FOURIER_GROUP_DIM
CONV_W = 256
CONV_K = 31
NORM_EPS = 1e-6
MASKED = -1e30
LOG2E = 1.4426950408889634

Q_ROWS = 4
K_ROWS = 12
MOD_ROWS = 8

VMEM_LIMIT = 56 * 1024 * 1024


def _sigmoid(x):
    return 1.0 / (1.0 + jnp.exp(-x))


def _silu(x):
    return x * _sigmoid(x)


def _params(n_axes, flags=None):
    return pltpu.CompilerParams(
        dimension_semantics=("arbitrary",) * n_axes, vmem_limit_bytes=VMEM_LIMIT, flags=flags)


def _mod_kernel(c_ref, w_ref, b_ref, o_ref):
    sc = _silu(c_ref[...]).astype(BF16)
    o_ref[0] = jnp.dot(sc, w_ref[0].astype(BF16), preferred_element_type=F32) + b_ref[0]


def _modulation(c_all, w_mod, b_mod):
    depth, d, d3 = w_mod.shape
    tn = d
    return pl.pallas_call(
        _mod_kernel,
        grid=(depth, d3 // tn),
        in_specs=[
            pl.BlockSpec((MOD_ROWS, d), lambda l, j: (0, 0)),
            pl.BlockSpec((1, d, tn), lambda l, j: (l, 0, j)),
            pl.BlockSpec((1, 1, tn), lambda l, j: (l, 0, j)),
        ],
        out_specs=pl.BlockSpec((1, MOD_ROWS, tn), lambda l, j: (l, 0, j)),
        out_shape=jax.ShapeDtypeStruct((depth, MOD_ROWS, d3), F32),
        compiler_params=_params(2),
        name="modulation",
    )(c_all, w_mod, b_mod.reshape(depth, 1, d3))


_SEC_Q = ("q", 0, ATTN_W, "qscale", ATTN_W, BF16, True)
_SEC_K = ("k", ATTN_W, 2 * ATTN_W, None, ATTN_W, BF16, True)
_SEC_V = ("v", 2 * ATTN_W, 3 * ATTN_W, None, ATTN_W, BF16, True)
_O = 4 * ATTN_W
_SEC_REST = (
    ("ga", 3 * ATTN_W, _O, "silu", ATTN_W, BF16, False),
    ("uf", _O, _O + FOURIER_W, None, FOURIER_W, F32, False),
    ("gf", _O + FOURIER_W, _O + 2 * FOURIER_W, "silu", FOURIER_W, BF16, False),
    ("u", _O + 2 * FOURIER_W, _O + 2 * FOURIER_W + 2 * CONV_W, "glu", CONV_W, BF16, False),
    ("gc", _O + 2 * FOURIER_W + 2 * CONV_W, _O + 2 * FOURIER_W + 3 * CONV_W, "silu", CONV_W, BF16, False),
)
_S0 = _O + 2 * FOURIER_W + 3 * CONV_W
ALL_SECTIONS = (_SEC_Q, _SEC_K, _SEC_V) + _SEC_REST
KV_SECTIONS = (_SEC_K, _SEC_V)


def _merge_sections(d):
    return tuple((n, _S0 + i * d, _S0 + (i + 1) * d, "sigmoid", d, BF16, False)
                 for i, n in enumerate(("sa", "sf", "sc")))


def _inproj_kernel(x_ref, mod_ref, g_ref, w_ref, *o_refs, sections):
    x = x_ref[0]
    d = x.shape[-1]
    m = mod_ref[0]
    shift, scale = m[:, :d], m[:, d:2 * d]
    y = x * lax.rsqrt(jnp.mean(x * x, axis=-1, keepdims=True) + NORM_EPS) * g_ref[...]
    hb = (y * (1.0 + scale) + shift).astype(BF16)
    epilogue_rank = {"sigmoid": 0, "glu": 1, "silu": 2, "qscale": 3, None: 4}
    order = sorted(range(len(sections)), key=lambda j: epilogue_rank[sections[j][3]])
    for (_, lo, hi, act, _, dtype, paired), o_ref in [(sections[j], o_refs[j]) for j in order]:
        acc = jnp.dot(hb, w_ref[:, lo:hi], preferred_element_type=F32)
        if act == "qscale":
            acc = acc * (LOG2E * HEAD_DIM ** -0.5)
        elif act == "silu":
            acc = _silu(acc)
        elif act == "sigmoid":
            acc = _sigmoid(acc)
        elif act == "glu":
            half = (hi - lo) // 2
            acc = acc[:, :half] * _sigmoid(acc[:, half:])
        if paired:
            for p in range(N_PAIRS):
                o_ref[0, p] = acc[:, p * LANES:(p + 1) * LANES].astype(dtype)
        else:
            o_ref[0] = acc.astype(dtype)


def _inproj(x, mod_l, g_pre_l, w_in_bf, sections, mod_row, tm):
    b, s, d = x.shape
    tm = min(tm, s)
    in_w = w_in_bf.shape[1]
    out_shapes, out_specs = [], []
    for (_, _, _, _, width, dtype, paired) in sections:
        if paired:
            out_shapes.append(jax.ShapeDtypeStruct((b, N_PAIRS, s, LANES), dtype))
            out_specs.append(pl.BlockSpec((1, N_PAIRS, tm, LANES), lambda bi, i: (bi, 0, i, 0)))
        else:
            out_shapes.append(jax.ShapeDtypeStruct((b, s, width), dtype))
            out_specs.append(pl.BlockSpec((1, tm, width), lambda bi, i: (bi, i, 0)))
    return pl.pallas_call(
        functools.partial(_inproj_kernel, sections=sections),
        grid=(b, s // tm),
        in_specs=[
            pl.BlockSpec((1, tm, d), lambda bi, i: (bi, i, 0)),
            pl.BlockSpec((1, 1, 3 * d), lambda bi, i: (mod_row(bi), 0, 0)),
            pl.BlockSpec((1, d), lambda bi, i: (0, 0)),
            pl.BlockSpec((d, in_w), lambda bi, i: (0, 0)),
        ],
        out_specs=out_specs,
        out_shape=out_shapes,
        compiler_params=_params(2),
        name="inproj",
    )(x, mod_l.reshape(MOD_ROWS, 1, 3 * d), g_pre_l.reshape(1, d), w_in_bf)


PAIR_STARTS = 2 * WIN_ROWS


def _bias_pair_tables(rpb):
    cols = np.arange(GRID_W)
    col_start = np.clip(cols - WIN_COLS // 2, 0, GRID_W - WIN_COLS)
    valid_c = (cols[None, :] >= col_start[:, None]) & (cols[None, :] < col_start[:, None] + WIN_COLS)
    pad = GRID_W - WIN_COLS
    padded = jnp.pad(rpb, ((0, 0), (0, 0), (0, 0), (pad, pad)))
    per_dr = jnp.stack([padded[..., GRID_W - 1 - c:2 * GRID_W - 1 - c] for c in range(GRID_W)], axis=3)
    per_dr = jnp.where(valid_c, per_dr * LOG2E, MASKED)
    masked = jnp.full_like(per_dr[:, :, :1], MASKED)
    ext = jnp.concatenate([masked, per_dr, masked], axis=2)
    return jnp.concatenate([ext[:, :, :-1], ext[:, :, 1:]], axis=-1)


def _window_plan(i, ws, rows):
    kr = min(WIN_ROWS, rows)
    low = _head_masks()
    plan = []
    for ri in range(Q_ROWS):
        r = Q_ROWS * i + ri
        rs = jnp.clip(r - kr // 2, 0, rows - kr)
        for jj in range(K_ROWS // 2):
            krow = ws + 2 * jj
            ok_lo = (krow >= rs) & (krow < rs + kr)
            ok_hi = (krow + 1 >= rs) & (krow + 1 < rs + kr)
            ok = jnp.where(low, ok_lo.astype(jnp.int32), ok_hi.astype(jnp.int32)) > 0
            start = jnp.clip(krow - r + WIN_ROWS, 0, PAIR_STARTS - 1)
            plan.append((ok, start))
    return plan


def _window_bias(tab_ref, head, plan):
    n_jj = K_ROWS // 2
    row_blocks = []
    for ri in range(Q_ROWS):
        lane_blocks = [jnp.where(ok, tab_ref[head, start], MASKED) for ok, start in plan[ri * n_jj:(ri + 1) * n_jj]]
        row_blocks.append(jnp.concatenate(lane_blocks, axis=-1))
    return jnp.concatenate(row_blocks, axis=0)


def _head_masks():
    lane = lax.broadcasted_iota(jnp.int32, (1, LANES), 1)
    return lane < HEAD_DIM


def _pick(head, own, other):
    low = _head_masks()
    return jnp.where(low, own, other) if head == 0 else jnp.where(low, other, own)


def _softmax_pv(head, q, keys, vals, biases):
    return _pv(head, *_scores(head, q, keys, biases), vals)


def _scores(head, q, keys, biases):
    dn = (((1,), (1,)), ((), ()))
    q_h = _pick(head, q, jnp.zeros_like(q))
    scores = []
    for k, bias in zip(keys, biases):
        s = lax.dot_general(q_h, k, dn, preferred_element_type=F32)
        scores.append(s if bias is None else s + bias)
    m = functools.reduce(jnp.maximum, [jnp.max(s, axis=-1, keepdims=True) for s in scores])
    return scores, m


def _pv(head, scores, m, vals):
    o = functools.reduce(jnp.add, [
        jnp.dot(jnp.exp2(s - m).astype(BF16), _pick(head, v, jnp.ones_like(v)), preferred_element_type=F32)
        for s, v in zip(scores, vals)])
    return o / pltpu.roll(o, HEAD_DIM, axis=1)


def _attn_kernel(q_ref, k_ref, v_ref, kc_ref, vc_ref, tab_ref, o_ref, *, rows, pairs):
    i = pl.program_id(2)
    ws = jnp.clip(Q_ROWS * i - min(WIN_ROWS, rows) // 2, 0, rows - K_ROWS)
    start = pl.multiple_of(ws * GRID_W, GRID_W)
    plan = _window_plan(i, ws, rows)
    heads = [(p, head) for p in range(pairs) for head in (0, 1)]
    staged = []
    for p, head in heads:
        q = q_ref[0, p]
        kw = k_ref[0, p, pl.ds(start, K_ROWS * GRID_W), :]
        staged.append(_scores(head, q, (kw, kc_ref[0, p]), (_window_bias(tab_ref, 2 * p + head, plan), None)))
    outs = []
    for (p, head), (scores, m) in zip(heads, staged):
        vw = v_ref[0, p, pl.ds(start, K_ROWS * GRID_W), :]
        outs.append(_pv(head, scores, m, (vw, vc_ref[0, p])))
    for p in range(pairs):
        o_ref[0, p] = _pick(0, outs[2 * p], outs[2 * p + 1]).astype(o_ref.dtype)


ATTN_PAIRS_PER_STEP = 4


def _attention(q, k, v, kc, vc, tab):
    b, _, s, _ = q.shape
    l = kc.shape[2]
    rows = s // GRID_W
    n_blk = rows // Q_ROWS
    tq = Q_ROWS * GRID_W
    pp = ATTN_PAIRS_PER_STEP

    return pl.pallas_call(
        functools.partial(_attn_kernel, rows=rows, pairs=pp),
        grid=(b, N_PAIRS // pp, n_blk),
        in_specs=[
            pl.BlockSpec((1, pp, tq, LANES), lambda bi, p, i: (bi, p, i, 0)),
            pl.BlockSpec((1, pp, s, LANES), lambda bi, p, i: (bi, p, 0, 0)),
            pl.BlockSpec((1, pp, s, LANES), lambda bi, p, i: (bi, p, 0, 0)),
            pl.BlockSpec((1, pp, l, LANES), lambda bi, p, i: (bi, p, 0, 0)),
            pl.BlockSpec((1, pp, l, LANES), lambda bi, p, i: (bi, p, 0, 0)),
            pl.BlockSpec((2 * pp, PAIR_STARTS, GRID_W, LANES), lambda bi, p, i: (p, 0, 0, 0)),
        ],
        out_specs=pl.BlockSpec((1, pp, tq, LANES), lambda bi, p, i: (bi, p, i, 0)),
        out_shape=jax.ShapeDtypeStruct((b, N_PAIRS, s, LANES), BF16),
        compiler_params=_params(3),
        name="attention",
    )(q, k, v, kc, vc, tab)


def _ctx_attn_kernel(q_ref, k_ref, v_ref, o_ref):
    q, k, v = q_ref[0, 0], k_ref[0, 0], v_ref[0, 0]
    o_lo, o_hi = [_softmax_pv(head, q, (k,), (v,), (None,)) for head in (0, 1)]
    o_ref[0, 0] = _pick(0, o_lo, o_hi).astype(o_ref.dtype)


def _ctx_attention(q, k, v):
    b, _, l, _ = q.shape
    spec = pl.BlockSpec((1, 1, l, LANES), lambda bi, p: (bi, p, 0, 0))
    return pl.pallas_call(
        _ctx_attn_kernel,
        grid=(b, N_PAIRS),
        in_specs=[spec, spec, spec],
        out_specs=spec,
        out_shape=jax.ShapeDtypeStruct((b, N_PAIRS, l, LANES), BF16),
        compiler_params=_params(2),
        name="ctx_attention",
    )(q, k, v)


def _dft_cos_sin(k, n, period):
    ang = (2.0 * np.pi / period) * ((k[:, None] * n[None, :]) % period)
    return np.cos(ang), np.sin(ang)


def _table(a):
    return jnp.asarray(a, F32).astype(BF16)


def _channel_dft():
    m = np.arange(FOURIER_GROUP_DIM)
    c, s = _dft_cos_sin(m, m, FOURIER_GROUP_DIM)
    eye = np.eye(FOURIER_GROUPS)
    scale = FOURIER_GROUP_DIM ** -0.5
    return np.concatenate([np.kron(eye, c.T), np.kron(eye, s.T)], axis=0) * scale


def _channel_weights(cs_ref, wf_ref):
    return jnp.dot(cs_ref[...], wf_ref[...], preferred_element_type=F32).astype(BF16)


def _channel_mix(xr, xi, cw):
    x = jnp.concatenate([xr, xi], axis=-1).astype(BF16)
    return jnp.dot(x, cw, preferred_element_type=F32)


FFT_GROUP = 8


def _fft_kernel(x_ref, w1_ref, m2_ref, cs_ref, wf_ref, pm_ref, o_ref, ar_ref, ai_ref, *, n1):
    g8 = FFT_GROUP
    for g in range(GRID_W // g8):
        xg = x_ref[0, :, g * g8:(g + 1) * g8, :].reshape(n1 * g8, FOURIER_W).astype(BF16)
        a = jnp.dot(w1_ref[...], xg, preferred_element_type=F32)
        ar_ref[:, g * g8:(g + 1) * g8, :] = a[:n1 * g8].reshape(n1, g8, FOURIER_W)
        ai_ref[:, g * g8:(g + 1) * g8, :] = a[n1 * g8:].reshape(n1, g8, FOURIER_W)
    for k1 in range(n1):
        a = jnp.concatenate([ar_ref[k1], ai_ref[k1]], axis=0).astype(BF16)
        x = jnp.dot(m2_ref[k1], a, preferred_element_type=F32)
        ar_ref[k1] = x[:GRID_W]
        ai_ref[k1] = x[GRID_W:]
    cw = _channel_weights(cs_ref, wf_ref)
    for g in range(n1 // g8):
        xr = ar_ref[g * g8:(g + 1) * g8].reshape(g8 * GRID_W, FOURIER_W)
        xi = ai_ref[g * g8:(g + 1) * g8].reshape(g8 * GRID_W, FOURIER_W)
        z = _channel_mix(xr, xi, cw).astype(BF16)
        o = jnp.dot(pm_ref[...], z, preferred_element_type=F32)
        o_ref[0, :, g * g8:(g + 1) * g8, :] = o.reshape(GRID_W, g8, FOURIER_W)


def _fourier(uf, wf_bf):
    b, s, fw = uf.shape
    n1 = s // GRID_W
    g8 = FFT_GROUP
    k1 = np.arange(n1)
    c1, s1 = _dft_cos_sin(k1, k1, n1)
    w1 = np.kron(np.concatenate([c1, -s1], axis=0) * n1 ** -0.5, np.eye(g8))
    n2 = np.arange(GRID_W)
    kk = k1[:, None] + n1 * np.arange(GRID_W)[None, :]
    ang = (2.0 * np.pi / s) * ((kk[:, :, None] * n2[None, None, :]) % s)
    c2, s2 = np.cos(ang), np.sin(ang)
    m2 = np.concatenate([np.concatenate([c2, s2], axis=2),
                         np.concatenate([-s2, c2], axis=2)], axis=1) * GRID_W ** -0.5
    perm = np.zeros((GRID_W * g8, g8 * GRID_W))
    k2g, k1g = np.meshgrid(np.arange(GRID_W), np.arange(g8), indexing="ij")
    perm[(k2g * g8 + k1g).ravel(), (k1g * GRID_W + k2g).ravel()] = 1.0
    out = pl.pallas_call(
        functools.partial(_fft_kernel, n1=n1),
        grid=(b,),
        in_specs=[
            pl.BlockSpec((1, n1, GRID_W, fw), lambda bi: (bi, 0, 0, 0)),
            pl.BlockSpec(w1.shape, lambda bi: (0, 0)),
            pl.BlockSpec((n1, 2 * GRID_W, 2 * GRID_W), lambda bi: (0, 0, 0)),
            pl.BlockSpec((2 * fw, fw), lambda bi: (0, 0)),
            pl.BlockSpec((fw, fw), lambda bi: (0, 0)),
            pl.BlockSpec(perm.shape, lambda bi: (0, 0)),
        ],
        out_specs=pl.BlockSpec((1, GRID_W, n1, fw), lambda bi: (bi, 0, 0, 0)),
        out_shape=jax.ShapeDtypeStruct((b, GRID_W, n1, fw), F32),
        scratch_shapes=[pltpu.VMEM((n1, GRID_W, fw), F32), pltpu.VMEM((n1, GRID_W, fw), F32)],
        compiler_params=_params(1),
        name="fourier",
    )(uf.reshape(b, n1, GRID_W, fw), _table(w1), _table(m2), _table(_channel_dft()), wf_bf, _table(perm))
    return out.reshape(b, s, fw)


def _ctx_fft_kernel(x_ref, wd_ref, cs_ref, wf_ref, o_ref):
    n = x_ref.shape[1]
    x = jnp.dot(wd_ref[...], x_ref[0].astype(BF16), preferred_element_type=F32)
    o_ref[0] = _channel_mix(x[:n], x[n:], _channel_weights(cs_ref, wf_ref))


def _ctx_fourier(uf, wf_bf):
    b, n, fw = uf.shape
    k = np.arange(n)
    c, s = _dft_cos_sin(k, k, n)
    wd = np.concatenate([c, -s], axis=0) * n ** -0.5
    return pl.pallas_call(
        _ctx_fft_kernel,
        grid=(b,),
        in_specs=[
            pl.BlockSpec((1, n, fw), lambda bi: (bi, 0, 0)),
            pl.BlockSpec((2 * n, n), lambda bi: (0, 0)),
            pl.BlockSpec((2 * fw, fw), lambda bi: (0, 0)),
            pl.BlockSpec((fw, fw), lambda bi: (0, 0)),
        ],
        out_specs=pl.BlockSpec((1, n, fw), lambda bi: (bi, 0, 0)),
        out_shape=jax.ShapeDtypeStruct((b, n, fw), F32),
        compiler_params=_params(1),
        name="ctx_fourier",
    )(uf, _table(wd), _table(_channel_dft()), wf_bf)


CONV_PAD = 16
CONV_TILE = 32
SUBLANES = 8
PW_TILE = 256


def _conv_kernel(u_ref, gc_ref, dw_ref, db_ref, lg_ref, lb_ref, pw_ref, o_ref, up_ref, c_ref):
    n = u_ref.shape[1]
    zeros = jnp.zeros((CONV_PAD, CONV_W), F32)
    up_ref[0:CONV_PAD, :] = zeros
    up_ref[CONV_PAD + n:CONV_PAD + n + CONV_PAD, :] = zeros
    up_ref[CONV_PAD:CONV_PAD + n, :] = u_ref[0].astype(F32)
    first = CONV_PAD - CONV_K // 2
    halo = CONV_TILE + 2 * CONV_PAD

    def conv_step(it, carry):
        base = pl.multiple_of(it * CONV_TILE, CONV_TILE)
        win = up_ref[pl.ds(base, halo), :]
        acc = jnp.zeros((CONV_TILE, CONV_W), F32)
        for ph in range(SUBLANES):
            taps = [t for t in range(CONV_K) if (first + t) % SUBLANES == ph]
            if not taps:
                continue
            shifted = win if ph == 0 else pltpu.roll(win, halo - ph, axis=0)
            for t in taps:
                off = first + t - ph
                acc = acc + shifted[off:off + CONV_TILE] * dw_ref[t:t + 1, :]
        c_ref[pl.ds(base, CONV_TILE), :] = acc + db_ref[...]
        return carry

    lax.fori_loop(0, n // CONV_TILE, conv_step, 0, unroll=2)

    pw_tile = min(PW_TILE, n)

    def pw_step(it, carry):
        base = pl.multiple_of(it * pw_tile, pw_tile)
        acc = c_ref[pl.ds(base, pw_tile), :]
        xc = acc - jnp.mean(acc, axis=-1, keepdims=True)
        var = jnp.mean(xc * xc, axis=-1, keepdims=True)
        y = xc * lax.rsqrt(var + NORM_EPS) * lg_ref[...] + lb_ref[...]
        z = jnp.dot(_silu(y).astype(BF16), pw_ref[...], preferred_element_type=F32)
        o_ref[0, pl.ds(base, pw_tile), :] = (z * gc_ref[0, pl.ds(base, pw_tile), :].astype(F32)).astype(BF16)
        return carry

    lax.fori_loop(0, n // pw_tile, pw_step, 0)


def _conv_module(u, gc, conv_dw_l, conv_db_l, ln_g_l, ln_b_l, w_pw_bf):
    b, n, cw = u.shape
    seq = pl.BlockSpec((1, n, cw), lambda bi: (bi, 0, 0))
    row = pl.BlockSpec((1, cw), lambda bi: (0, 0))
    return pl.pallas_call(
        _conv_kernel,
        grid=(b,),
        in_specs=[seq, seq, pl.BlockSpec((CONV_K, cw), lambda bi: (0, 0)), row, row, row,
                  pl.BlockSpec((cw, cw), lambda bi: (0, 0))],
        out_specs=seq,
        out_shape=jax.ShapeDtypeStruct((b, n, cw), BF16),
        scratch_shapes=[pltpu.VMEM((n + 2 * CONV_PAD, cw), F32), pltpu.VMEM((n, cw), F32)],
        compiler_params=_params(1),
        name="conv_module",
    )(u, gc, conv_dw_l, conv_db_l.reshape(1, cw), ln_g_l.reshape(1, cw), ln_b_l.reshape(1, cw), w_pw_bf)


def _tail_kernel(x_ref, ao_ref, ga_ref, fw_ref, gf_ref, cw_ref, sa_ref, sf_ref, sc_ref, mod_ref, gp_ref,
                 pa_ref, pf_ref, pc_ref, wo_ref, o_ref):
    d = x_ref.shape[-1]
    ao = jnp.concatenate([ao_ref[0, p] for p in range(N_PAIRS)], axis=-1)
    ya = jnp.dot(ao * ga_ref[0], pa_ref[...], preferred_element_type=F32)
    yf = jnp.dot((fw_ref[0] * gf_ref[0].astype(F32)).astype(BF16), pf_ref[...], preferred_element_type=F32)
    yc = jnp.dot(cw_ref[0], pc_ref[...], preferred_element_type=F32)
    merged = (sa_ref[0].astype(F32) * ya + sf_ref[0].astype(F32) * yf + sc_ref[0].astype(F32) * yc)
    y = jnp.dot(merged.astype(BF16), wo_ref[...], preferred_element_type=F32)
    yn = y * lax.rsqrt(jnp.mean(y * y, axis=-1, keepdims=True) + NORM_EPS) * gp_ref[...]
    gate = mod_ref[0][:, 2 * d:]
    o_ref[0] = x_ref[0] + gate * yn


def _tail(x, ao, ga, fw, gf, cw, sa, sf, sc, mod_l, g_post_l, pa_bf, pf_bf, pc_bf, wo_bf, mod_row, tm):
    b, s, d = x.shape
    tm = min(tm, s)

    def tok(width):
        return pl.BlockSpec((1, tm, width), lambda bi, i: (bi, i, 0))

    def full(shape):
        return pl.BlockSpec(shape, lambda bi, i: (0,) * len(shape))

    return pl.pallas_call(
        _tail_kernel,
        grid=(b, s // tm),
        in_specs=[
            tok(d),
            pl.BlockSpec((1, N_PAIRS, tm, LANES), lambda bi, i: (bi, 0, i, 0)),
            tok(ATTN_W), tok(FOURIER_W), tok(FOURIER_W), tok(CONV_W), tok(d), tok(d), tok(d),
            pl.BlockSpec((1, 1, 3 * d), lambda bi, i: (mod_row(bi), 0, 0)),
            full((1, d)),
            full(pa_bf.shape), full(pf_bf.shape), full(pc_bf.shape), full(wo_bf.shape),
        ],
        out_specs=tok(d),
        out_shape=jax.ShapeDtypeStruct((b, s, d), F32),
        compiler_params=_params(2),
        name="tail",
    )(x, ao, ga, fw, gf, cw, sa, sf, sc, mod_l.reshape(MOD_ROWS, 1, 3 * d), g_post_l.reshape(1, d),
      pa_bf, pf_bf, pc_bf, wo_bf)


INPROJ_TILE = 256
TAIL_TILE = 512


def kernel(x, c, ctx, c_ctx, w_mod, b_mod, g_pre, g_post, w_in, rpb, w_four, conv_dw, conv_db,
           conv_ln_g, conv_ln_b, w_pw, p_attn, p_four, p_conv, w_out):
    batch, seq, d = x.shape
    depth = w_mod.shape[0]
    rows = seq // GRID_W
    assert batch < MOD_ROWS and rows % Q_ROWS == 0 and rows >= K_ROWS
    ctx_row = batch
    c_all = jnp.zeros((MOD_ROWS, d), F32).at[:batch].set(c).at[ctx_row].set(c_ctx)
    mod = _modulation(c_all, w_mod, b_mod)
    merge_sections = _merge_sections(d)
    bias_tabs = _bias_pair_tables(rpb)

    def latent_row(bi):
        return bi

    def context_row(bi):
        return ctx_row

    for l in range(depth):
        w_in_bf = w_in[l].astype(BF16)
        wf_bf, pw_bf = w_four[l].astype(BF16), w_pw[l].astype(BF16)
        pa_bf, pf_bf, pc_bf, wo_bf = (p_attn[l].astype(BF16), p_four[l].astype(BF16),
                                      p_conv[l].astype(BF16), w_out[l].astype(BF16))
        conv_w = (conv_dw[l], conv_db[l], conv_ln_g[l], conv_ln_b[l], pw_bf)
        tail_w = (mod[l], g_post[l], pa_bf, pf_bf, pc_bf, wo_bf)
        update_ctx = l < depth - 1

        ctx_sections = ALL_SECTIONS + merge_sections if update_ctx else KV_SECTIONS
        ctx_out = dict(zip([sec[0] for sec in ctx_sections],
                           _inproj(ctx, mod[l], g_pre[l], w_in_bf, ctx_sections, context_row, INPROJ_TILE)))

        lat = dict(zip([sec[0] for sec in ALL_SECTIONS + merge_sections],
                       _inproj(x, mod[l], g_pre[l], w_in_bf, ALL_SECTIONS + merge_sections, latent_row,
                               INPROJ_TILE)))
        ao = _attention(lat["q"], lat["k"], lat["v"], ctx_out["k"], ctx_out["v"], bias_tabs[l])
        fw = _fourier(lat["uf"], wf_bf)
        cw = _conv_module(lat["u"], lat["gc"], *conv_w)
        x = _tail(x, ao, lat["ga"], fw, lat["gf"], cw, lat["sa"], lat["sf"], lat["sc"], *tail_w,
                  latent_row, TAIL_TILE)

        if update_ctx:
            co = ctx_out
            ao_c = _ctx_attention(co["q"], co["k"], co["v"])
            fw_c = _ctx_fourier(co["uf"], wf_bf)
            cw_c = _conv_module(co["u"], co["gc"], *conv_w)
            ctx = _tail(ctx, ao_c, co["ga"], fw_c, co["gf"], cw_c, co["sa"], co["sf"], co["sc"], *tail_w,
                        context_row, TAIL_TILE)
    return x
```

```python
import functools

import jax
import jax.numpy as jnp
import numpy as np
from jax import lax
from jax.experimental import pallas as pl
from jax.experimental.pallas import tpu as pltpu

F32 = jnp.float32
BF16 = jnp.bfloat16

GRID_W = 64
N_HEADS = 8
HEAD_DIM = 64
ATTN_W = N_HEADS * HEAD_DIM
LANES = 128
N_PAIRS = ATTN_W // LANES
WIN_ROWS = 8
WIN_COLS = 16
FOURIER_GROUPS = 4
FOURIER_GROUP_DIM = 64
FOURIER_W = FOURIER_GROUPS * FOURIER_GROUP_DIM
CONV_W = 256
CONV_K = 31
NORM_EPS = 1e-6
MASKED = -1e30
LOG2E = 1.4426950408889634

Q_ROWS = 4
K_ROWS = 12
MOD_ROWS = 8

VMEM_LIMIT = 56 * 1024 * 1024


def _sigmoid(x):
    return 1.0 / (1.0 + jnp.exp(-x))


def _silu(x):
    return x * _sigmoid(x)


def _params(n_axes, flags=None):
    return pltpu.CompilerParams(
        dimension_semantics=("arbitrary",) * n_axes, vmem_limit_bytes=VMEM_LIMIT, flags=flags)


def _mod_kernel(c_ref, w_ref, b_ref, o_ref):
    sc = _silu(c_ref[...]).astype(BF16)
    o_ref[0] = jnp.dot(sc, w_ref[0].astype(BF16), preferred_element_type=F32) + b_ref[0]


def _modulation(c_all, w_mod, b_mod):
    depth, d, d3 = w_mod.shape
    tn = d
    return pl.pallas_call(
        _mod_kernel,
        grid=(depth, d3 // tn),
        in_specs=[
            pl.BlockSpec((MOD_ROWS, d), lambda l, j: (0, 0)),
            pl.BlockSpec((1, d, tn), lambda l, j: (l, 0, j)),
            pl.BlockSpec((1, 1, tn), lambda l, j: (l, 0, j)),
        ],
        out_specs=pl.BlockSpec((1, MOD_ROWS, tn), lambda l, j: (l, 0, j)),
        out_shape=jax.ShapeDtypeStruct((depth, MOD_ROWS, d3), F32),
        compiler_params=_params(2),
        name="modulation",
    )(c_all, w_mod, b_mod.reshape(depth, 1, d3))


_SEC_Q = ("q", 0, ATTN_W, "qscale", ATTN_W, BF16, True)
_SEC_K = ("k", ATTN_W, 2 * ATTN_W, None, ATTN_W, BF16, True)
_SEC_V = ("v", 2 * ATTN_W, 3 * ATTN_W, None, ATTN_W, BF16, True)
_O = 4 * ATTN_W
_SEC_REST = (
    ("ga", 3 * ATTN_W, _O, "silu", ATTN_W, BF16, False),
    ("uf", _O, _O + FOURIER_W, None, FOURIER_W, F32, False),
    ("gf", _O + FOURIER_W, _O + 2 * FOURIER_W, "silu", FOURIER_W, BF16, False),
    ("u", _O + 2 * FOURIER_W, _O + 2 * FOURIER_W + 2 * CONV_W, "glu", CONV_W, BF16, False),
    ("gc", _O + 2 * FOURIER_W + 2 * CONV_W, _O + 2 * FOURIER_W + 3 * CONV_W, "silu", CONV_W, BF16, False),
)
_S0 = _O + 2 * FOURIER_W + 3 * CONV_W
ALL_SECTIONS = (_SEC_Q, _SEC_K, _SEC_V) + _SEC_REST
KV_SECTIONS = (_SEC_K, _SEC_V)


def _merge_sections(d):
    return tuple((n, _S0 + i * d, _S0 + (i + 1) * d, "sigmoid", d, BF16, False)
                 for i, n in enumerate(("sa", "sf", "sc")))


def _inproj_kernel(x_ref, mod_ref, g_ref, w_ref, *o_refs, sections):
    x = x_ref[0]
    d = x.shape[-1]
    m = mod_ref[0]
    shift, scale = m[:, :d], m[:, d:2 * d]
    y = x * lax.rsqrt(jnp.mean(x * x, axis=-1, keepdims=True) + NORM_EPS) * g_ref[...]
    hb = (y * (1.0 + scale) + shift).astype(BF16)
    epilogue_rank = {"sigmoid": 0, "glu": 1, "silu": 2, "qscale": 3, None: 4}
    order = sorted(range(len(sections)), key=lambda j: epilogue_rank[sections[j][3]])
    for (_, lo, hi, act, _, dtype, paired), o_ref in [(sections[j], o_refs[j]) for j in order]:
        acc = jnp.dot(hb, w_ref[:, lo:hi], preferred_element_type=F32)
        if act == "qscale":
            acc = acc * (LOG2E * HEAD_DIM ** -0.5)
        elif act == "silu":
            acc = _silu(acc)
        elif act == "sigmoid":
            acc = _sigmoid(acc)
        elif act == "glu":
            half = (hi - lo) // 2
            acc = acc[:, :half] * _sigmoid(acc[:, half:])
        if paired:
            for p in range(N_PAIRS):
                o_ref[0, p] = acc[:, p * LANES:(p + 1) * LANES].astype(dtype)
        else:
            o_ref[0] = acc.astype(dtype)


def _inproj(x, mod_l, g_pre_l, w_in_bf, layer, sections, mod_row, tm):
    b, s, d = x.shape
    tm = min(tm, s)
    in_w = w_in_bf.shape[-1]
    out_shapes, out_specs = [], []
    for (_, _, _, _, width, dtype, paired) in sections:
        if paired:
            out_shapes.append(jax.ShapeDtypeStruct((b, N_PAIRS, s, LANES), dtype))
            out_specs.append(pl.BlockSpec((1, N_PAIRS, tm, LANES), lambda bi, i: (bi, 0, i, 0)))
        else:
            out_shapes.append(jax.ShapeDtypeStruct((b, s, width), dtype))
            out_specs.append(pl.BlockSpec((1, tm, width), lambda bi, i: (bi, i, 0)))
    return pl.pallas_call(
        functools.partial(_inproj_kernel, sections=sections),
        grid=(b, s // tm),
        in_specs=[
            pl.BlockSpec((1, tm, d), lambda bi, i: (bi, i, 0)),
            pl.BlockSpec((1, 1, 3 * d), lambda bi, i: (mod_row(bi), 0, 0)),
            pl.BlockSpec((1, d), lambda bi, i: (0, 0)),
            pl.BlockSpec((None, d, in_w), lambda bi, i: (layer, 0, 0)),
        ],
        out_specs=out_specs,
        out_shape=out_shapes,
        compiler_params=_params(2),
        name="inproj",
    )(x, mod_l.reshape(MOD_ROWS, 1, 3 * d), g_pre_l.reshape(1, d), w_in_bf)


PAIR_STARTS = 2 * WIN_ROWS


def _bias_pair_tables(rpb):
    cols = np.arange(GRID_W)
    col_start = np.clip(cols - WIN_COLS // 2, 0, GRID_W - WIN_COLS)
    valid_c = (cols[None, :] >= col_start[:, None]) & (cols[None, :] < col_start[:, None] + WIN_COLS)
    dc = cols[None, :] - cols[:, None] + (WIN_COLS - 1)
    one_hot = (dc[None] == np.arange(2 * WIN_COLS - 1)[:, None, None]) & valid_c[None]
    per_dr = jnp.einsum("lhdp,pck->lhdck", rpb, jnp.asarray(one_hot, F32), precision=lax.Precision.HIGHEST)
    per_dr = jnp.where(valid_c, per_dr * LOG2E, MASKED)
    masked = jnp.full_like(per_dr[:, :, :1], MASKED)
    ext = jnp.concatenate([masked, per_dr, masked], axis=2)
    return jnp.concatenate([ext[:, :, :-1], ext[:, :, 1:]], axis=-1)


def _window_plan(i, ws, rows):
    kr = min(WIN_ROWS, rows)
    low = _head_masks()
    plan = []
    for ri in range(Q_ROWS):
        r = Q_ROWS * i + ri
        rs = jnp.clip(r - kr // 2, 0, rows - kr)
        for jj in range(K_ROWS // 2):
            krow = ws + 2 * jj
            ok_lo = (krow >= rs) & (krow < rs + kr)
            ok_hi = (krow + 1 >= rs) & (krow + 1 < rs + kr)
            ok = jnp.where(low, ok_lo.astype(jnp.int32), ok_hi.astype(jnp.int32)) > 0
            start = jnp.clip(krow - r + WIN_ROWS, 0, PAIR_STARTS - 1)
            plan.append((ok, start))
    return plan


def _window_bias(tab_ref, head, plan):
    n_jj = K_ROWS // 2
    row_blocks = []
    for ri in range(Q_ROWS):
        lane_blocks = [jnp.where(ok, tab_ref[head, start], MASKED) for ok, start in plan[ri * n_jj:(ri + 1) * n_jj]]
        row_blocks.append(jnp.concatenate(lane_blocks, axis=-1))
    return jnp.concatenate(row_blocks, axis=0)


def _head_masks():
    lane = lax.broadcasted_iota(jnp.int32, (1, LANES), 1)
    return lane < HEAD_DIM


def _pick(head, own, other):
    low = _head_masks()
    return jnp.where(low, own, other) if head == 0 else jnp.where(low, other, own)


def _pair_scores(q, keys, biases):
    dn = (((1,), (1,)), ((), ()))
    m_rows = q.shape[0]
    q2 = jnp.concatenate([_pick(head, q, jnp.zeros_like(q)) for head in (0, 1)], axis=0)
    both = [lax.dot_general(q2, k, dn, preferred_element_type=F32) for k in keys]
    out = []
    for head in (0, 1):
        scores = [s[head * m_rows:(head + 1) * m_rows] for s in both]
        scores = [s if bias is None else s + bias for s, bias in zip(scores, biases[head])]
        m = functools.reduce(jnp.maximum, [jnp.max(s, axis=-1, keepdims=True) for s in scores])
        out.append((scores, m))
    return out


def _pv(head, scores, m, vals):
    o = functools.reduce(jnp.add, [
        jnp.dot(jnp.exp2(s - m).astype(BF16), _pick(head, v, jnp.ones_like(v)), preferred_element_type=F32)
        for s, v in zip(scores, vals)])
    return o / pltpu.roll(o, HEAD_DIM, axis=1)


def _attn_kernel(q_ref, k_ref, v_ref, kc_ref, vc_ref, tab_ref, o_ref, *, rows, pairs):
    i = pl.program_id(2)
    ws = jnp.clip(Q_ROWS * i - min(WIN_ROWS, rows) // 2, 0, rows - K_ROWS)
    start = pl.multiple_of(ws * GRID_W, GRID_W)
    plan = _window_plan(i, ws, rows)
    staged = []
    for p in range(pairs):
        kw = k_ref[0, p, pl.ds(start, K_ROWS * GRID_W), :]
        biases = [(_window_bias(tab_ref, 2 * p + head, plan), None) for head in (0, 1)]
        staged.extend(_pair_scores(q_ref[0, p], (kw, kc_ref[0, p]), biases))
    outs = []
    for j, (scores, m) in enumerate(staged):
        p, head = divmod(j, 2)
        vw = v_ref[0, p, pl.ds(start, K_ROWS * GRID_W), :]
        outs.append(_pv(head, scores, m, (vw, vc_ref[0, p])))
    for p in range(pairs):
        o_ref[0, p] = _pick(0, outs[2 * p], outs[2 * p + 1]).astype(o_ref.dtype)


ATTN_PAIRS_PER_STEP = 4


def _attention(q, k, v, kc, vc, tab):
    b, _, s, _ = q.shape
    l = kc.shape[2]
    rows = s // GRID_W
    n_blk = rows // Q_ROWS
    tq = Q_ROWS * GRID_W
    pp = ATTN_PAIRS_PER_STEP

    return pl.pallas_call(
        functools.partial(_attn_kernel, rows=rows, pairs=pp),
        grid=(b, N_PAIRS // pp, n_blk),
        in_specs=[
            pl.BlockSpec((1, pp, tq, LANES), lambda bi, p, i: (bi, p, i, 0)),
            pl.BlockSpec((1, pp, s, LANES), lambda bi, p, i: (bi, p, 0, 0)),
            pl.BlockSpec((1, pp, s, LANES), lambda bi, p, i: (bi, p, 0, 0)),
            pl.BlockSpec((1, pp, l, LANES), lambda bi, p, i: (bi, p, 0, 0)),
            pl.BlockSpec((1, pp, l, LANES), lambda bi, p, i: (bi, p, 0, 0)),
            pl.BlockSpec((2 * pp, PAIR_STARTS, GRID_W, LANES), lambda bi, p, i: (p, 0, 0, 0)),
        ],
        out_specs=pl.BlockSpec((1, pp, tq, LANES), lambda bi, p, i: (bi, p, i, 0)),
        out_shape=jax.ShapeDtypeStruct((b, N_PAIRS, s, LANES), BF16),
        compiler_params=_params(3),
        name="attention",
    )(q, k, v, kc, vc, tab)


def _ctx_attn_kernel(q_ref, k_ref, v_ref, o_ref):
    staged = []
    for p in range(N_PAIRS):
        staged.extend(_pair_scores(q_ref[0, p], (k_ref[0, p],), [(None,), (None,)]))
    outs = [_pv(j % 2, scores, m, (v_ref[0, j // 2],)) for j, (scores, m) in enumerate(staged)]
    for p in range(N_PAIRS):
        o_ref[0, p] = _pick(0, outs[2 * p], outs[2 * p + 1]).astype(o_ref.dtype)


def _ctx_attention(q, k, v):
    b, _, l, _ = q.shape
    spec = pl.BlockSpec((1, N_PAIRS, l, LANES), lambda bi: (bi, 0, 0, 0))
    return pl.pallas_call(
        _ctx_attn_kernel,
        grid=(b,),
        in_specs=[spec, spec, spec],
        out_specs=spec,
        out_shape=jax.ShapeDtypeStruct((b, N_PAIRS, l, LANES), BF16),
        compiler_params=_params(1),
        name="ctx_attention",
    )(q, k, v)


def _dft_cos_sin(k, n, period):
    ang = (2.0 * np.pi / period) * ((k[:, None] * n[None, :]) % period)
    return np.cos(ang), np.sin(ang)


def _table(a):
    return jnp.asarray(a, F32).astype(BF16)


def _channel_dft():
    m = np.arange(FOURIER_GROUP_DIM)
    c, s = _dft_cos_sin(m, m, FOURIER_GROUP_DIM)
    eye = np.eye(FOURIER_GROUPS)
    scale = FOURIER_GROUP_DIM ** -0.5
    return np.concatenate([np.kron(eye, c.T), np.kron(eye, s.T)], axis=0) * scale


def _channel_weights(cs_ref, wf_ref):
    return jnp.dot(cs_ref[...], wf_ref[...], preferred_element_type=F32).astype(BF16)


def _channel_mix(xr, xi, cw):
    x = jnp.concatenate([xr, xi], axis=-1).astype(BF16)
    return jnp.dot(x, cw, preferred_element_type=F32)


FFT_GROUP = 8


def _fft_kernel(x_ref, w1_ref, m2_ref, cs_ref, wf_ref, pm_ref, o_ref, ar_ref, ai_ref, *, n1):
    g8 = FFT_GROUP
    for g in range(GRID_W // g8):
        xg = x_ref[0, :, g * g8:(g + 1) * g8, :].reshape(n1 * g8, FOURIER_W).astype(BF16)
        a = jnp.dot(w1_ref[...], xg, preferred_element_type=F32)
        ar_ref[:, g * g8:(g + 1) * g8, :] = a[:n1 * g8].reshape(n1, g8, FOURIER_W)
        ai_ref[:, g * g8:(g + 1) * g8, :] = a[n1 * g8:].reshape(n1, g8, FOURIER_W)
    for k1 in range(n1):
        a = jnp.concatenate([ar_ref[k1], ai_ref[k1]], axis=0).astype(BF16)
        x = jnp.dot(m2_ref[k1], a, preferred_element_type=F32)
        ar_ref[k1] = x[:GRID_W]
        ai_ref[k1] = x[GRID_W:]
    cw = _channel_weights(cs_ref, wf_ref)
    for g in range(n1 // g8):
        xr = ar_ref[g * g8:(g + 1) * g8].reshape(g8 * GRID_W, FOURIER_W)
        xi = ai_ref[g * g8:(g + 1) * g8].reshape(g8 * GRID_W, FOURIER_W)
        z = _channel_mix(xr, xi, cw).astype(BF16)
        o = jnp.dot(pm_ref[...], z, preferred_element_type=F32)
        o_ref[0, :, g * g8:(g + 1) * g8, :] = o.reshape(GRID_W, g8, FOURIER_W)


def _fourier(uf, wf_bf):
    b, s, fw = uf.shape
    n1 = s // GRID_W
    g8 = FFT_GROUP
    k1 = np.arange(n1)
    c1, s1 = _dft_cos_sin(k1, k1, n1)
    w1 = np.kron(np.concatenate([c1, -s1], axis=0) * n1 ** -0.5, np.eye(g8))
    n2 = np.arange(GRID_W)
    kk = k1[:, None] + n1 * np.arange(GRID_W)[None, :]
    ang = (2.0 * np.pi / s) * ((kk[:, :, None] * n2[None, None, :]) % s)
    c2, s2 = np.cos(ang), np.sin(ang)
    m2 = np.concatenate([np.concatenate([c2, s2], axis=2),
                         np.concatenate([-s2, c2], axis=2)], axis=1) * GRID_W ** -0.5
    perm = np.zeros((GRID_W * g8, g8 * GRID_W))
    k2g, k1g = np.meshgrid(np.arange(GRID_W), np.arange(g8), indexing="ij")
    perm[(k2g * g8 + k1g).ravel(), (k1g * GRID_W + k2g).ravel()] = 1.0
    out = pl.pallas_call(
        functools.partial(_fft_kernel, n1=n1),
        grid=(b,),
        in_specs=[
            pl.BlockSpec((1, n1, GRID_W, fw), lambda bi: (bi, 0, 0, 0)),
            pl.BlockSpec(w1.shape, lambda bi: (0, 0)),
            pl.BlockSpec((n1, 2 * GRID_W, 2 * GRID_W), lambda bi: (0, 0, 0)),
            pl.BlockSpec((2 * fw, fw), lambda bi: (0, 0)),
            pl.BlockSpec((fw, fw), lambda bi: (0, 0)),
            pl.BlockSpec(perm.shape, lambda bi: (0, 0)),
        ],
        out_specs=pl.BlockSpec((1, GRID_W, n1, fw), lambda bi: (bi, 0, 0, 0)),
        out_shape=jax.ShapeDtypeStruct((b, GRID_W, n1, fw), F32),
        scratch_shapes=[pltpu.VMEM((n1, GRID_W, fw), F32), pltpu.VMEM((n1, GRID_W, fw), F32)],
        compiler_params=_params(1),
        name="fourier",
    )(uf.reshape(b, n1, GRID_W, fw), _table(w1), _table(m2), _table(_channel_dft()), wf_bf, _table(perm))
    return out.reshape(b, s, fw)


def _ctx_fft_kernel(x_ref, wd_ref, cs_ref, wf_ref, o_ref):
    n = x_ref.shape[1]
    x = jnp.dot(wd_ref[...], x_ref[0].astype(BF16), preferred_element_type=F32)
    o_ref[0] = _channel_mix(x[:n], x[n:], _channel_weights(cs_ref, wf_ref))


def _ctx_fourier(uf, wf_bf):
    b, n, fw = uf.shape
    k = np.arange(n)
    c, s = _dft_cos_sin(k, k, n)
    wd = np.concatenate([c, -s], axis=0) * n ** -0.5
    return pl.pallas_call(
        _ctx_fft_kernel,
        grid=(b,),
        in_specs=[
            pl.BlockSpec((1, n, fw), lambda bi: (bi, 0, 0)),
            pl.BlockSpec((2 * n, n), lambda bi: (0, 0)),
            pl.BlockSpec((2 * fw, fw), lambda bi: (0, 0)),
            pl.BlockSpec((fw, fw), lambda bi: (0, 0)),
        ],
        out_specs=pl.BlockSpec((1, n, fw), lambda bi: (bi, 0, 0)),
        out_shape=jax.ShapeDtypeStruct((b, n, fw), F32),
        compiler_params=_params(1),
        name="ctx_fourier",
    )(uf, _table(wd), _table(_channel_dft()), wf_bf)


CONV_PAD = 16
CONV_TILE = 32
SUBLANES = 8
PW_TILE = 256


def _conv_kernel(u_ref, gc_ref, dw_ref, db_ref, lg_ref, lb_ref, pw_ref, o_ref, up_ref, c_ref):
    n = u_ref.shape[1]
    zeros = jnp.zeros((CONV_PAD, CONV_W), F32)
    up_ref[0:CONV_PAD, :] = zeros
    up_ref[CONV_PAD + n:CONV_PAD + n + CONV_PAD, :] = zeros
    up_ref[CONV_PAD:CONV_PAD + n, :] = u_ref[0].astype(F32)
    first = CONV_PAD - CONV_K // 2
    halo = CONV_TILE + 2 * CONV_PAD

    def conv_step(it, carry):
        base = pl.multiple_of(it * CONV_TILE, CONV_TILE)
        win = up_ref[pl.ds(base, halo), :]
        acc = jnp.zeros((CONV_TILE, CONV_W), F32)
        for ph in range(SUBLANES):
            taps = [t for t in range(CONV_K) if (first + t) % SUBLANES == ph]
            if not taps:
                continue
            shifted = win if ph == 0 else pltpu.roll(win, halo - ph, axis=0)
            for t in taps:
                off = first + t - ph
                acc = acc + shifted[off:off + CONV_TILE] * dw_ref[t:t + 1, :]
        c_ref[pl.ds(base, CONV_TILE), :] = acc + db_ref[...]
        return carry

    lax.fori_loop(0, n // CONV_TILE, conv_step, 0, unroll=2)

    pw_tile = min(PW_TILE, n)

    def pw_step(it, carry):
        base = pl.multiple_of(it * pw_tile, pw_tile)
        acc = c_ref[pl.ds(base, pw_tile), :]
        xc = acc - jnp.mean(acc, axis=-1, keepdims=True)
        var = jnp.mean(xc * xc, axis=-1, keepdims=True)
        y = xc * lax.rsqrt(var + NORM_EPS) * lg_ref[...] + lb_ref[...]
        z = jnp.dot(_silu(y).astype(BF16), pw_ref[...], preferred_element_type=F32)
        o_ref[0, pl.ds(base, pw_tile), :] = (z * gc_ref[0, pl.ds(base, pw_tile), :].astype(F32)).astype(BF16)
        return carry

    lax.fori_loop(0, n // pw_tile, pw_step, 0)


def _conv_module(u, gc, conv_dw_l, conv_db_l, ln_g_l, ln_b_l, w_pw_bf):
    b, n, cw = u.shape
    seq = pl.BlockSpec((1, n, cw), lambda bi: (bi, 0, 0))
    row = pl.BlockSpec((1, cw), lambda bi: (0, 0))
    return pl.pallas_call(
        _conv_kernel,
        grid=(b,),
        in_specs=[seq, seq, pl.BlockSpec((CONV_K, cw), lambda bi: (0, 0)), row, row, row,
                  pl.BlockSpec((cw, cw), lambda bi: (0, 0))],
        out_specs=seq,
        out_shape=jax.ShapeDtypeStruct((b, n, cw), BF16),
        scratch_shapes=[pltpu.VMEM((n + 2 * CONV_PAD, cw), F32), pltpu.VMEM((n, cw), F32)],
        compiler_params=_params(1),
        name="conv_module",
    )(u, gc, conv_dw_l, conv_db_l.reshape(1, cw), ln_g_l.reshape(1, cw), ln_b_l.reshape(1, cw), w_pw_bf)


def _tail_kernel(x_ref, ao_ref, ga_ref, fw_ref, gf_ref, cw_ref, sa_ref, sf_ref, sc_ref, mod_ref, gp_ref,
                 pa_ref, pf_ref, pc_ref, wo_ref, o_ref):
    d = x_ref.shape[-1]
    ao = jnp.concatenate([ao_ref[0, p] for p in range(N_PAIRS)], axis=-1)
    ya = jnp.dot(ao * ga_ref[0], pa_ref[...], preferred_element_type=F32)
    yf = jnp.dot((fw_ref[0] * gf_ref[0].astype(F32)).astype(BF16), pf_ref[...], preferred_element_type=F32)
    yc = jnp.dot(cw_ref[0], pc_ref[...], preferred_element_type=F32)
    merged = (sa_ref[0].astype(F32) * ya + sf_ref[0].astype(F32) * yf + sc_ref[0].astype(F32) * yc)
    y = jnp.dot(merged.astype(BF16), wo_ref[...], preferred_element_type=F32)
    yn = y * lax.rsqrt(jnp.mean(y * y, axis=-1, keepdims=True) + NORM_EPS) * gp_ref[...]
    gate = mod_ref[0][:, 2 * d:]
    o_ref[0] = x_ref[0] + gate * yn


def _tail(x, ao, ga, fw, gf, cw, sa, sf, sc, mod_l, g_post_l, pa_bf, pf_bf, pc_bf, wo_bf, mod_row, tm):
    b, s, d = x.shape
    tm = min(tm, s)

    def tok(width):
        return pl.BlockSpec((1, tm, width), lambda bi, i: (bi, i, 0))

    def full(shape):
        return pl.BlockSpec(shape, lambda bi, i: (0,) * len(shape))

    return pl.pallas_call(
        _tail_kernel,
        grid=(b, s // tm),
        in_specs=[
            tok(d),
            pl.BlockSpec((1, N_PAIRS, tm, LANES), lambda bi, i: (bi, 0, i, 0)),
            tok(ATTN_W), tok(FOURIER_W), tok(FOURIER_W), tok(CONV_W), tok(d), tok(d), tok(d),
            pl.BlockSpec((1, 1, 3 * d), lambda bi, i: (mod_row(bi), 0, 0)),
            full((1, d)),
            full(pa_bf.shape), full(pf_bf.shape), full(pc_bf.shape), full(wo_bf.shape),
        ],
        out_specs=tok(d),
        out_shape=jax.ShapeDtypeStruct((b, s, d), F32),
        compiler_params=_params(2),
        name="tail",
    )(x, ao, ga, fw, gf, cw, sa, sf, sc, mod_l.reshape(MOD_ROWS, 1, 3 * d), g_post_l.reshape(1, d),
      pa_bf, pf_bf, pc_bf, wo_bf)


INPROJ_TILE = 256
TAIL_TILE = 512


def kernel(x, c, ctx, c_ctx, w_mod, b_mod, g_pre, g_post, w_in, rpb, w_four, conv_dw, conv_db,
           conv_ln_g, conv_ln_b, w_pw, p_attn, p_four, p_conv, w_out):
    batch, seq, d = x.shape
    depth = w_mod.shape[0]
    rows = seq // GRID_W
    assert batch < MOD_ROWS and rows % Q_ROWS == 0 and rows >= K_ROWS
    ctx_row = batch
    c_all = jnp.zeros((MOD_ROWS, d), F32).at[:batch].set(c).at[ctx_row].set(c_ctx)
    mod = _modulation(c_all, w_mod, b_mod)
    merge_sections = _merge_sections(d)
    bias_tabs = _bias_pair_tables(rpb)

    def latent_row(bi):
        return bi

    def context_row(bi):
        return ctx_row

    w_in_bf = w_in.astype(BF16)
    for l in range(depth):
        wf_bf, pw_bf = w_four[l].astype(BF16), w_pw[l].astype(BF16)
        pa_bf, pf_bf, pc_bf, wo_bf = (p_attn[l].astype(BF16), p_four[l].astype(BF16),
                                      p_conv[l].astype(BF16), w_out[l].astype(BF16))
        conv_w = (conv_dw[l], conv_db[l], conv_ln_g[l], conv_ln_b[l], pw_bf)
        tail_w = (mod[l], g_post[l], pa_bf, pf_bf, pc_bf, wo_bf)
        update_ctx = l < depth - 1

        if update_ctx:
            ctx_sections, ctx_w, ctx_layer = ALL_SECTIONS + merge_sections, w_in_bf, l
        else:
            lo, hi = KV_SECTIONS[0][1], KV_SECTIONS[-1][2]
            ctx_sections = tuple((sec[0], sec[1] - lo, sec[2] - lo) + sec[3:] for sec in KV_SECTIONS)
            ctx_w, ctx_layer = w_in_bf[l:l + 1, :, lo:hi], 0
        ctx_out = dict(zip([sec[0] for sec in ctx_sections],
                           _inproj(ctx, mod[l], g_pre[l], ctx_w, ctx_layer, ctx_sections, context_row,
                                   INPROJ_TILE)))

        lat = dict(zip([sec[0] for sec in ALL_SECTIONS + merge_sections],
                       _inproj(x, mod[l], g_pre[l], w_in_bf, l, ALL_SECTIONS + merge_sections, latent_row,
                               INPROJ_TILE)))
        ao = _attention(lat["q"], lat["k"], lat["v"], ctx_out["k"], ctx_out["v"], bias_tabs[l])
        fw = _fourier(lat["uf"], wf_bf)
        cw = _conv_module(lat["u"], lat["gc"], *conv_w)
        x = _tail(x, ao, lat["ga"], fw, lat["gf"], cw, lat["sa"], lat["sf"], lat["sc"], *tail_w,
                  latent_row, TAIL_TILE)

        if update_ctx:
            co = ctx_out
            ao_c = _ctx_attention(co["q"], co["k"], co["v"])
            fw_c = _ctx_fourier(co["uf"], wf_bf)
            cw_c = _conv_module(co["u"], co["gc"], *conv_w)
            ctx = _tail(ctx, ao_c, co["ga"], fw_c, co["gf"], cw_c, co["sa"], co["sf"], co["sc"], *tail_w,
                        context_row, TAIL_TILE)
    return x
```

```python
import functools

import jax
import jax.numpy as jnp
import numpy as np
from jax import lax
from jax.experimental import pallas as pl
from jax.experimental.pallas import tpu as pltpu

F32 = jnp.float32
BF16 = jnp.bfloat16

GRID_W = 64
N_HEADS = 8
HEAD_DIM = 64
ATTN_W = N_HEADS * HEAD_DIM
LANES = 128
N_PAIRS = ATTN_W // LANES
WIN_ROWS = 8
WIN_COLS = 16
FOURIER_GROUPS = 4
FOURIER_GROUP_DIM = 64
FOURIER_W = FOURIER_GROUPS * FOURIER_GROUP_DIM
CONV_W = 256
CONV_K = 31
NORM_EPS = 1e-6
MASKED = -1e30
LOG2E = 1.4426950408889634

Q_ROWS = 4
K_ROWS = 12
MOD_ROWS = 8

VMEM_LIMIT = 56 * 1024 * 1024


def _sigmoid(x):
    return 1.0 / (1.0 + jnp.exp(-x))


def _silu(x):
    return x * _sigmoid(x)


def _params(n_axes):
    return pltpu.CompilerParams(
        dimension_semantics=("arbitrary",) * n_axes, vmem_limit_bytes=VMEM_LIMIT)


def _mod_kernel(c_ref, w_ref, b_ref, o_ref):
    sc = _silu(c_ref[...]).astype(BF16)
    o_ref[0] = jnp.dot(sc, w_ref[0].astype(BF16), preferred_element_type=F32) + b_ref[0]


def _modulation(c_all, w_mod, b_mod):
    depth, d, d3 = w_mod.shape
    tn = d
    return pl.pallas_call(
        _mod_kernel,
        grid=(depth, d3 // tn),
        in_specs=[
            pl.BlockSpec((MOD_ROWS, d), lambda l, j: (0, 0)),
            pl.BlockSpec((1, d, tn), lambda l, j: (l, 0, j)),
            pl.BlockSpec((1, 1, tn), lambda l, j: (l, 0, j)),
        ],
        out_specs=pl.BlockSpec((1, MOD_ROWS, tn), lambda l, j: (l, 0, j)),
        out_shape=jax.ShapeDtypeStruct((depth, MOD_ROWS, d3), F32),
        compiler_params=_params(2),
        name="modulation",
    )(c_all, w_mod, b_mod.reshape(depth, 1, d3))


_SEC_Q = ("q", 0, ATTN_W, "qscale", ATTN_W, BF16, True)
_SEC_K = ("k", ATTN_W, 2 * ATTN_W, None, ATTN_W, BF16, True)
_SEC_V = ("v", 2 * ATTN_W, 3 * ATTN_W, None, ATTN_W, BF16, True)
_O = 4 * ATTN_W
_SEC_REST = (
    ("ga", 3 * ATTN_W, _O, "silu", ATTN_W, BF16, False),
    ("uf", _O, _O + FOURIER_W, None, FOURIER_W, F32, False),
    ("gf", _O + FOURIER_W, _O + 2 * FOURIER_W, "silu", FOURIER_W, BF16, False),
    ("u", _O + 2 * FOURIER_W, _O + 2 * FOURIER_W + 2 * CONV_W, "glu", CONV_W, BF16, False),
    ("gc", _O + 2 * FOURIER_W + 2 * CONV_W, _O + 2 * FOURIER_W + 3 * CONV_W, "silu", CONV_W, BF16, False),
)
_S0 = _O + 2 * FOURIER_W + 3 * CONV_W
ALL_SECTIONS = (_SEC_Q, _SEC_K, _SEC_V) + _SEC_REST
KV_SECTIONS = (_SEC_K, _SEC_V)


def _merge_sections(d):
    return tuple((n, _S0 + i * d, _S0 + (i + 1) * d, "sigmoid", d, BF16, False)
                 for i, n in enumerate(("sa", "sf", "sc")))


def _inproj_kernel(x_ref, mod_ref, g_ref, w_ref, *o_refs, sections):
    x = x_ref[0]
    d = x.shape[-1]
    m = mod_ref[0]
    shift, scale = m[:, :d], m[:, d:2 * d]
    y = x * lax.rsqrt(jnp.mean(x * x, axis=-1, keepdims=True) + NORM_EPS) * g_ref[...]
    hb = (y * (1.0 + scale) + shift).astype(BF16)
    epilogue_rank = {"sigmoid": 0, "glu": 1, "silu": 2, "qscale": 3, None: 4}
    order = sorted(range(len(sections)), key=lambda j: epilogue_rank[sections[j][3]])
    for (_, lo, hi, act, _, dtype, paired), o_ref in [(sections[j], o_refs[j]) for j in order]:
        acc = jnp.dot(hb, w_ref[:, lo:hi], preferred_element_type=F32)
        if act == "qscale":
            acc = acc * (LOG2E * HEAD_DIM ** -0.5)
        elif act == "silu":
            acc = _silu(acc)
        elif act == "sigmoid":
            acc = _sigmoid(acc)
        elif act == "glu":
            half = (hi - lo) // 2
            acc = acc[:, :half] * _sigmoid(acc[:, half:])
        if paired:
            for p in range(N_PAIRS):
                o_ref[0, p] = acc[:, p * LANES:(p + 1) * LANES].astype(dtype)
        else:
            o_ref[0] = acc.astype(dtype)


def _inproj(x, mod_l, g_pre_l, w_in_bf, layer, sections, mod_row, tm):
    b, s, d = x.shape
    tm = min(tm, s)
    in_w = w_in_bf.shape[-1]
    out_shapes, out_specs = [], []
    for (_, _, _, _, width, dtype, paired) in sections:
        if paired:
            out_shapes.append(jax.ShapeDtypeStruct((b, N_PAIRS, s, LANES), dtype))
            out_specs.append(pl.BlockSpec((1, N_PAIRS, tm, LANES), lambda bi, i: (bi, 0, i, 0)))
        else:
            out_shapes.append(jax.ShapeDtypeStruct((b, s, width), dtype))
            out_specs.append(pl.BlockSpec((1, tm, width), lambda bi, i: (bi, i, 0)))
    return pl.pallas_call(
        functools.partial(_inproj_kernel, sections=sections),
        grid=(b, s // tm),
        in_specs=[
            pl.BlockSpec((1, tm, d), lambda bi, i: (bi, i, 0)),
            pl.BlockSpec((1, 1, 3 * d), lambda bi, i: (mod_row(bi), 0, 0)),
            pl.BlockSpec((1, d), lambda bi, i: (0, 0)),
            pl.BlockSpec((None, d, in_w), lambda bi, i: (layer, 0, 0)),
        ],
        out_specs=out_specs,
        out_shape=out_shapes,
        compiler_params=_params(2),
        name="inproj",
    )(x, mod_l.reshape(MOD_ROWS, 1, 3 * d), g_pre_l.reshape(1, d), w_in_bf)


PAIR_STARTS = 2 * WIN_ROWS


def _bias_pair_tables(rpb):
    cols = np.arange(GRID_W)
    col_start = np.clip(cols - WIN_COLS // 2, 0, GRID_W - WIN_COLS)
    valid_c = (cols[None, :] >= col_start[:, None]) & (cols[None, :] < col_start[:, None] + WIN_COLS)
    dc = cols[None, :] - cols[:, None] + (WIN_COLS - 1)
    one_hot = (dc[None] == np.arange(2 * WIN_COLS - 1)[:, None, None]) & valid_c[None]
    per_dr = jnp.einsum("lhdp,pck->lhdck", rpb, jnp.asarray(one_hot, F32), precision=lax.Precision.HIGHEST)
    per_dr = jnp.where(valid_c, per_dr * LOG2E, MASKED)
    masked = jnp.full_like(per_dr[:, :, :1], MASKED)
    ext = jnp.concatenate([masked, per_dr, masked], axis=2)
    return jnp.concatenate([ext[:, :, :-1], ext[:, :, 1:]], axis=-1)


def _window_plan(i, ws, rows):
    kr = min(WIN_ROWS, rows)
    low = _head_masks()
    plan = []
    for ri in range(Q_ROWS):
        r = Q_ROWS * i + ri
        rs = jnp.clip(r - kr // 2, 0, rows - kr)
        for jj in range(K_ROWS // 2):
            krow = ws + 2 * jj
            ok_lo = (krow >= rs) & (krow < rs + kr)
            ok_hi = (krow + 1 >= rs) & (krow + 1 < rs + kr)
            ok = jnp.where(low, ok_lo.astype(jnp.int32), ok_hi.astype(jnp.int32)) > 0
            start = jnp.clip(krow - r + WIN_ROWS, 0, PAIR_STARTS - 1)
            plan.append((ok, start))
    return plan


def _window_bias(tab_ref, head, plan):
    n_jj = K_ROWS // 2
    row_blocks = []
    for ri in range(Q_ROWS):
        lane_blocks = [jnp.where(ok, tab_ref[head, start], MASKED) for ok, start in plan[ri * n_jj:(ri + 1) * n_jj]]
        row_blocks.append(jnp.concatenate(lane_blocks, axis=-1))
    return jnp.concatenate(row_blocks, axis=0)


def _head_masks():
    lane = lax.broadcasted_iota(jnp.int32, (1, LANES), 1)
    return lane < HEAD_DIM


def _pick(head, own, other):
    low = _head_masks()
    return jnp.where(low, own, other) if head == 0 else jnp.where(low, other, own)


def _pair_scores(q, keys, biases):
    dn = (((1,), (1,)), ((), ()))
    m_rows = q.shape[0]
    q2 = jnp.concatenate([_pick(head, q, jnp.zeros_like(q)) for head in (0, 1)], axis=0)
    both = [lax.dot_general(q2, k, dn, preferred_element_type=F32) for k in keys]
    out = []
    for head in (0, 1):
        scores = [s[head * m_rows:(head + 1) * m_rows] for s in both]
        scores = [s if bias is None else s + bias for s, bias in zip(scores, biases[head])]
        m = functools.reduce(jnp.maximum, [jnp.max(s, axis=-1, keepdims=True) for s in scores])
        out.append((scores, m))
    return out


def _pv(head, scores, m, vals):
    o = functools.reduce(jnp.add, [
        jnp.dot(jnp.exp2(s - m).astype(BF16), _pick(head, v, jnp.ones_like(v)), preferred_element_type=F32)
        for s, v in zip(scores, vals)])
    return o / pltpu.roll(o, HEAD_DIM, axis=1)


def _attn_kernel(q_ref, k_ref, v_ref, kc_ref, vc_ref, tab_ref, o_ref, *, rows, pairs):
    i = pl.program_id(2)
    ws = jnp.clip(Q_ROWS * i - min(WIN_ROWS, rows) // 2, 0, rows - K_ROWS)
    start = pl.multiple_of(ws * GRID_W, GRID_W)
    plan = _window_plan(i, ws, rows)
    staged = []
    for p in range(pairs):
        kw = k_ref[0, p, pl.ds(start, K_ROWS * GRID_W), :]
        biases = [(_window_bias(tab_ref, 2 * p + head, plan), None) for head in (0, 1)]
        staged.extend(_pair_scores(q_ref[0, p], (kw, kc_ref[0, p]), biases))
    outs = []
    for j, (scores, m) in enumerate(staged):
        p, head = divmod(j, 2)
        vw = v_ref[0, p, pl.ds(start, K_ROWS * GRID_W), :]
        outs.append(_pv(head, scores, m, (vw, vc_ref[0, p])))
    for p in range(pairs):
        o_ref[0, p] = _pick(0, outs[2 * p], outs[2 * p + 1]).astype(o_ref.dtype)


ATTN_PAIRS_PER_STEP = 4


def _attention(q, k, v, kc, vc, tab):
    b, _, s, _ = q.shape
    l = kc.shape[2]
    rows = s // GRID_W
    n_blk = rows // Q_ROWS
    tq = Q_ROWS * GRID_W
    pp = ATTN_PAIRS_PER_STEP

    return pl.pallas_call(
        functools.partial(_attn_kernel, rows=rows, pairs=pp),
        grid=(b, N_PAIRS // pp, n_blk),
        in_specs=[
            pl.BlockSpec((1, pp, tq, LANES), lambda bi, p, i: (bi, p, i, 0)),
            pl.BlockSpec((1, pp, s, LANES), lambda bi, p, i: (bi, p, 0, 0)),
            pl.BlockSpec((1, pp, s, LANES), lambda bi, p, i: (bi, p, 0, 0)),
            pl.BlockSpec((1, pp, l, LANES), lambda bi, p, i: (bi, p, 0, 0)),
            pl.BlockSpec((1, pp, l, LANES), lambda bi, p, i: (bi, p, 0, 0)),
            pl.BlockSpec((2 * pp, PAIR_STARTS, GRID_W, LANES), lambda bi, p, i: (p, 0, 0, 0)),
        ],
        out_specs=pl.BlockSpec((1, pp, tq, LANES), lambda bi, p, i: (bi, p, i, 0)),
        out_shape=jax.ShapeDtypeStruct((b, N_PAIRS, s, LANES), BF16),
        compiler_params=_params(3),
        name="attention",
    )(q, k, v, kc, vc, tab)


def _ctx_attn_kernel(q_ref, k_ref, v_ref, o_ref):
    staged = []
    for p in range(N_PAIRS):
        staged.extend(_pair_scores(q_ref[0, p], (k_ref[0, p],), [(None,), (None,)]))
    outs = [_pv(j % 2, scores, m, (v_ref[0, j // 2],)) for j, (scores, m) in enumerate(staged)]
    for p in range(N_PAIRS):
        o_ref[0, p] = _pick(0, outs[2 * p], outs[2 * p + 1]).astype(o_ref.dtype)


def _ctx_attention(q, k, v):
    b, _, l, _ = q.shape
    spec = pl.BlockSpec((1, N_PAIRS, l, LANES), lambda bi: (bi, 0, 0, 0))
    return pl.pallas_call(
        _ctx_attn_kernel,
        grid=(b,),
        in_specs=[spec, spec, spec],
        out_specs=spec,
        out_shape=jax.ShapeDtypeStruct((b, N_PAIRS, l, LANES), BF16),
        compiler_params=_params(1),
        name="ctx_attention",
    )(q, k, v)


def _dft_cos_sin(k, n, period):
    ang = (2.0 * np.pi / period) * ((k[:, None] * n[None, :]) % period)
    return np.cos(ang), np.sin(ang)


def _table(a):
    return jnp.asarray(a, F32).astype(BF16)


def _channel_dft():
    m = np.arange(FOURIER_GROUP_DIM)
    c, s = _dft_cos_sin(m, m, FOURIER_GROUP_DIM)
    eye = np.eye(FOURIER_GROUPS)
    scale = FOURIER_GROUP_DIM ** -0.5
    return np.concatenate([np.kron(eye, c.T), np.kron(eye, s.T)], axis=0) * scale


def _channel_weights(cs_ref, wf_ref):
    return jnp.dot(cs_ref[...], wf_ref[...], preferred_element_type=F32).astype(BF16)


def _channel_mix(xr, xi, cw):
    x = jnp.concatenate([xr, xi], axis=-1).astype(BF16)
    return jnp.dot(x, cw, preferred_element_type=F32)


FFT_GROUP = 8


def _fft_kernel(x_ref, w1_ref, m2_ref, cs_ref, wf_ref, pm_ref, o_ref, ar_ref, ai_ref, *, n1):
    g8 = FFT_GROUP
    for g in range(GRID_W // g8):
        xg = x_ref[0, :, g * g8:(g + 1) * g8, :].reshape(n1 * g8, FOURIER_W).astype(BF16)
        a = jnp.dot(w1_ref[...], xg, preferred_element_type=F32)
        ar_ref[:, g * g8:(g + 1) * g8, :] = a[:n1 * g8].reshape(n1, g8, FOURIER_W)
        ai_ref[:, g * g8:(g + 1) * g8, :] = a[n1 * g8:].reshape(n1, g8, FOURIER_W)
    for k1 in range(n1):
        a = jnp.concatenate([ar_ref[k1], ai_ref[k1]], axis=0).astype(BF16)
        x = jnp.dot(m2_ref[k1], a, preferred_element_type=F32)
        ar_ref[k1] = x[:GRID_W]
        ai_ref[k1] = x[GRID_W:]
    cw = _channel_weights(cs_ref, wf_ref)
    for g in range(n1 // g8):
        xr = ar_ref[g * g8:(g + 1) * g8].reshape(g8 * GRID_W, FOURIER_W)
        xi = ai_ref[g * g8:(g + 1) * g8].reshape(g8 * GRID_W, FOURIER_W)
        z = _channel_mix(xr, xi, cw).astype(BF16)
        o = jnp.dot(pm_ref[...], z, preferred_element_type=F32)
        o_ref[0, :, g * g8:(g + 1) * g8, :] = o.reshape(GRID_W, g8, FOURIER_W)


def _fourier(uf, wf_bf):
    b, s, fw = uf.shape
    n1 = s // GRID_W
    g8 = FFT_GROUP
    k1 = np.arange(n1)
    c1, s1 = _dft_cos_sin(k1, k1, n1)
    w1 = np.kron(np.concatenate([c1, -s1], axis=0) * n1 ** -0.5, np.eye(g8))
    n2 = np.arange(GRID_W)
    kk = k1[:, None] + n1 * np.arange(GRID_W)[None, :]
    ang = (2.0 * np.pi / s) * ((kk[:, :, None] * n2[None, None, :]) % s)
    c2, s2 = np.cos(ang), np.sin(ang)
    m2 = np.concatenate([np.concatenate([c2, s2], axis=2),
                         np.concatenate([-s2, c2], axis=2)], axis=1) * GRID_W ** -0.5
    perm = np.zeros((GRID_W * g8, g8 * GRID_W))
    k2g, k1g = np.meshgrid(np.arange(GRID_W), np.arange(g8), indexing="ij")
    perm[(k2g * g8 + k1g).ravel(), (k1g * GRID_W + k2g).ravel()] = 1.0
    out = pl.pallas_call(
        functools.partial(_fft_kernel, n1=n1),
        grid=(b,),
        in_specs=[
            pl.BlockSpec((1, n1, GRID_W, fw), lambda bi: (bi, 0, 0, 0)),
            pl.BlockSpec(w1.shape, lambda bi: (0, 0)),
            pl.BlockSpec((n1, 2 * GRID_W, 2 * GRID_W), lambda bi: (0, 0, 0)),
            pl.BlockSpec((2 * fw, fw), lambda bi: (0, 0)),
            pl.BlockSpec((fw, fw), lambda bi: (0, 0)),
            pl.BlockSpec(perm.shape, lambda bi: (0, 0)),
        ],
        out_specs=pl.BlockSpec((1, GRID_W, n1, fw), lambda bi: (bi, 0, 0, 0)),
        out_shape=jax.ShapeDtypeStruct((b, GRID_W, n1, fw), F32),
        scratch_shapes=[pltpu.VMEM((n1, GRID_W, fw), F32), pltpu.VMEM((n1, GRID_W, fw), F32)],
        compiler_params=_params(1),
        name="fourier",
    )(uf.reshape(b, n1, GRID_W, fw), _table(w1), _table(m2), _table(_channel_dft()), wf_bf, _table(perm))
    return out.reshape(b, s, fw)


def _ctx_fft_kernel(x_ref, wd_ref, cs_ref, wf_ref, o_ref):
    n = x_ref.shape[1]
    x = jnp.dot(wd_ref[...], x_ref[0].astype(BF16), preferred_element_type=F32)
    o_ref[0] = _channel_mix(x[:n], x[n:], _channel_weights(cs_ref, wf_ref))


def _ctx_fourier(uf, wf_bf):
    b, n, fw = uf.shape
    k = np.arange(n)
    c, s = _dft_cos_sin(k, k, n)
    wd = np.concatenate([c, -s], axis=0) * n ** -0.5
    return pl.pallas_call(
        _ctx_fft_kernel,
        grid=(b,),
        in_specs=[
            pl.BlockSpec((1, n, fw), lambda bi: (bi, 0, 0)),
            pl.BlockSpec((2 * n, n), lambda bi: (0, 0)),
            pl.BlockSpec((2 * fw, fw), lambda bi: (0, 0)),
            pl.BlockSpec((fw, fw), lambda bi: (0, 0)),
        ],
        out_specs=pl.BlockSpec((1, n, fw), lambda bi: (bi, 0, 0)),
        out_shape=jax.ShapeDtypeStruct((b, n, fw), F32),
        compiler_params=_params(1),
        name="ctx_fourier",
    )(uf, _table(wd), _table(_channel_dft()), wf_bf)


CONV_PAD = 16
CONV_TILE = 32
SUBLANES = 8


def _conv_tile(up_ref, base, dw_ref):
    first = CONV_PAD - CONV_K // 2
    halo = CONV_TILE + 2 * CONV_PAD
    win = up_ref[pl.ds(base, halo), :]
    acc = jnp.zeros((CONV_TILE, CONV_W), F32)
    for ph in range(SUBLANES):
        taps = [t for t in range(CONV_K) if (first + t) % SUBLANES == ph]
        if not taps:
            continue
        shifted = win if ph == 0 else pltpu.roll(win, halo - ph, axis=0)
        for t in taps:
            off = first + t - ph
            acc = acc + shifted[off:off + CONV_TILE] * dw_ref[t:t + 1, :]
    return acc


def _tail_kernel(x_ref, ao_ref, ga_ref, fw_ref, gf_ref, u_ref, gc_ref, sa_ref, sf_ref, sc_ref, mod_ref, gp_ref,
                 pa_ref, pf_ref, pc_ref, wo_ref, dw_ref, db_ref, lg_ref, lb_ref, pw_ref, o_ref, up_ref):
    d = x_ref.shape[-1]
    tm = x_ref.shape[1]
    n = u_ref.shape[1]
    i = pl.program_id(1)

    @pl.when(i == 0)
    def _():
        zeros = jnp.zeros((CONV_PAD, CONV_W), F32)
        up_ref[0:CONV_PAD, :] = zeros
        up_ref[CONV_PAD + n:CONV_PAD + n + CONV_PAD, :] = zeros
        up_ref[CONV_PAD:CONV_PAD + n, :] = u_ref[0].astype(F32)

    ao = jnp.concatenate([ao_ref[0, p] for p in range(N_PAIRS)], axis=-1)
    ya = jnp.dot(ao * ga_ref[0], pa_ref[...], preferred_element_type=F32)
    yf = jnp.dot((fw_ref[0] * gf_ref[0].astype(F32)).astype(BF16), pf_ref[...], preferred_element_type=F32)
    conv = jnp.concatenate(
        [_conv_tile(up_ref, pl.multiple_of(i * tm + j * CONV_TILE, CONV_TILE), dw_ref)
         for j in range(tm // CONV_TILE)], axis=0) + db_ref[...]
    xc = conv - jnp.mean(conv, axis=-1, keepdims=True)
    var = jnp.mean(xc * xc, axis=-1, keepdims=True)
    cn = xc * lax.rsqrt(var + NORM_EPS) * lg_ref[...] + lb_ref[...]
    cw = jnp.dot(_silu(cn).astype(BF16), pw_ref[...], preferred_element_type=F32) * gc_ref[0].astype(F32)
    yc = jnp.dot(cw.astype(BF16), pc_ref[...], preferred_element_type=F32)
    merged = (sa_ref[0].astype(F32) * ya + sf_ref[0].astype(F32) * yf + sc_ref[0].astype(F32) * yc)
    y = jnp.dot(merged.astype(BF16), wo_ref[...], preferred_element_type=F32)
    yn = y * lax.rsqrt(jnp.mean(y * y, axis=-1, keepdims=True) + NORM_EPS) * gp_ref[...]
    gate = mod_ref[0][:, 2 * d:]
    o_ref[0] = x_ref[0] + gate * yn


def _tail(x, ao, ga, fw, gf, u, gc, sa, sf, sc, mod_l, g_post_l, pa_bf, pf_bf, pc_bf, wo_bf,
          conv_dw_l, conv_db_l, ln_g_l, ln_b_l, w_pw_bf, mod_row, tm):
    b, s, d = x.shape
    tm = min(tm, s)

    def tok(width):
        return pl.BlockSpec((1, tm, width), lambda bi, i: (bi, i, 0))

    def full(shape):
        return pl.BlockSpec(shape, lambda bi, i: (0,) * len(shape))

    return pl.pallas_call(
        _tail_kernel,
        grid=(b, s // tm),
        in_specs=[
            tok(d),
            pl.BlockSpec((1, N_PAIRS, tm, LANES), lambda bi, i: (bi, 0, i, 0)),
            tok(ATTN_W), tok(FOURIER_W), tok(FOURIER_W),
            pl.BlockSpec((1, s, CONV_W), lambda bi, i: (bi, 0, 0)),
            tok(CONV_W), tok(d), tok(d), tok(d),
            pl.BlockSpec((1, 1, 3 * d), lambda bi, i: (mod_row(bi), 0, 0)),
            full((1, d)),
            full(pa_bf.shape), full(pf_bf.shape), full(pc_bf.shape), full(wo_bf.shape),
            full((CONV_K, CONV_W)), full((1, CONV_W)), full((1, CONV_W)), full((1, CONV_W)), full(w_pw_bf.shape),
        ],
        out_specs=tok(d),
        out_shape=jax.ShapeDtypeStruct((b, s, d), F32),
        scratch_shapes=[pltpu.VMEM((s + 2 * CONV_PAD, CONV_W), F32)],
        compiler_params=_params(2),
        name="tail",
    )(x, ao, ga, fw, gf, u, gc, sa, sf, sc, mod_l.reshape(MOD_ROWS, 1, 3 * d), g_post_l.reshape(1, d),
      pa_bf, pf_bf, pc_bf, wo_bf, conv_dw_l, conv_db_l.reshape(1, CONV_W), ln_g_l.reshape(1, CONV_W),
      ln_b_l.reshape(1, CONV_W), w_pw_bf)


INPROJ_TILE = 256
TAIL_TILE = 512


def kernel(x, c, ctx, c_ctx, w_mod, b_mod, g_pre, g_post, w_in, rpb, w_four, conv_dw, conv_db,
           conv_ln_g, conv_ln_b, w_pw, p_attn, p_four, p_conv, w_out):
    batch, seq, d = x.shape
    depth = w_mod.shape[0]
    rows = seq // GRID_W
    assert batch < MOD_ROWS and rows % Q_ROWS == 0 and rows >= K_ROWS
    ctx_row = batch
    c_all = jnp.zeros((MOD_ROWS, d), F32).at[:batch].set(c).at[ctx_row].set(c_ctx)
    mod = _modulation(c_all, w_mod, b_mod)
    merge_sections = _merge_sections(d)
    bias_tabs = _bias_pair_tables(rpb)

    def latent_row(bi):
        return bi

    def context_row(bi):
        return ctx_row

    w_in_bf = w_in.astype(BF16)
    for l in range(depth):
        wf_bf, pw_bf = w_four[l].astype(BF16), w_pw[l].astype(BF16)
        pa_bf, pf_bf, pc_bf, wo_bf = (p_attn[l].astype(BF16), p_four[l].astype(BF16),
                                      p_conv[l].astype(BF16), w_out[l].astype(BF16))
        conv_w = (conv_dw[l], conv_db[l], conv_ln_g[l], conv_ln_b[l], pw_bf)
        tail_w = (mod[l], g_post[l], pa_bf, pf_bf, pc_bf, wo_bf)
        update_ctx = l < depth - 1

        if update_ctx:
            ctx_sections, ctx_w, ctx_layer = ALL_SECTIONS + merge_sections, w_in_bf, l
        else:
            lo, hi = KV_SECTIONS[0][1], KV_SECTIONS[-1][2]
            ctx_sections = tuple((sec[0], sec[1] - lo, sec[2] - lo) + sec[3:] for sec in KV_SECTIONS)
            ctx_w, ctx_layer = w_in_bf[l:l + 1, :, lo:hi], 0
        ctx_out = dict(zip([sec[0] for sec in ctx_sections],
                           _inproj(ctx, mod[l], g_pre[l], ctx_w, ctx_layer, ctx_sections, context_row,
                                   INPROJ_TILE)))

        lat = dict(zip([sec[0] for sec in ALL_SECTIONS + merge_sections],
                       _inproj(x, mod[l], g_pre[l], w_in_bf, l, ALL_SECTIONS + merge_sections, latent_row,
                               INPROJ_TILE)))
        ao = _attention(lat["q"], lat["k"], lat["v"], ctx_out["k"], ctx_out["v"], bias_tabs[l])
        fw = _fourier(lat["uf"], wf_bf)
        x = _tail(x, ao, lat["ga"], fw, lat["gf"], lat["u"], lat["gc"], lat["sa"], lat["sf"], lat["sc"],
                  *tail_w, *conv_w, latent_row, TAIL_TILE)

        if update_ctx:
            co = ctx_out
            ao_c = _ctx_attention(co["q"], co["k"], co["v"])
            fw_c = _ctx_fourier(co["uf"], wf_bf)
            ctx = _tail(ctx, ao_c, co["ga"], fw_c, co["gf"], co["u"], co["gc"], co["sa"], co["sf"], co["sc"],
                        *tail_w, *conv_w, context_row, TAIL_TILE)
    return x
```

```python
import functools

import jax
import jax.numpy as jnp
import numpy as np
from jax import lax
from jax.experimental import pallas as pl
from jax.experimental.pallas import tpu as pltpu

F32 = jnp.float32
BF16 = jnp.bfloat16

GRID_W = 64
N_HEADS = 8
HEAD_DIM = 64
ATTN_W = N_HEADS * HEAD_DIM
LANES = 128
N_PAIRS = ATTN_W // LANES
WIN_ROWS = 8
WIN_COLS = 16
FOURIER_GROUPS = 4
FOURIER_GROUP_DIM = 64
FOURIER_W = FOURIER_GROUPS * FOURIER_GROUP_DIM
CONV_W = 256
CONV_K = 31
NORM_EPS = 1e-6
MASKED = -1e30
LOG2E = 1.4426950408889634

Q_ROWS = 4
K_ROWS = 12
MOD_ROWS = 8

VMEM_LIMIT = 56 * 1024 * 1024


def _sigmoid(x):
    return 1.0 / (1.0 + jnp.exp(-x))


def _silu(x):
    return x * _sigmoid(x)


def _params(n_axes):
    return pltpu.CompilerParams(
        dimension_semantics=("arbitrary",) * n_axes, vmem_limit_bytes=VMEM_LIMIT)


def _mod_kernel(c_ref, w_ref, b_ref, o_ref):
    sc = _silu(c_ref[...]).astype(BF16)
    o_ref[0] = jnp.dot(sc, w_ref[0].astype(BF16), preferred_element_type=F32) + b_ref[0]


def _modulation(c_all, w_mod, b_mod):
    depth, d, d3 = w_mod.shape
    tn = d
    return pl.pallas_call(
        _mod_kernel,
        grid=(depth, d3 // tn),
        in_specs=[
            pl.BlockSpec((MOD_ROWS, d), lambda l, j: (0, 0)),
            pl.BlockSpec((1, d, tn), lambda l, j: (l, 0, j)),
            pl.BlockSpec((1, 1, tn), lambda l, j: (l, 0, j)),
        ],
        out_specs=pl.BlockSpec((1, MOD_ROWS, tn), lambda l, j: (l, 0, j)),
        out_shape=jax.ShapeDtypeStruct((depth, MOD_ROWS, d3), F32),
        compiler_params=_params(2),
        name="modulation",
    )(c_all, w_mod, b_mod.reshape(depth, 1, d3))


_SEC_Q = ("q", 0, ATTN_W, "qscale", ATTN_W, BF16, True)
_SEC_K = ("k", ATTN_W, 2 * ATTN_W, None, ATTN_W, BF16, True)
_SEC_V = ("v", 2 * ATTN_W, 3 * ATTN_W, None, ATTN_W, BF16, True)
_O = 4 * ATTN_W
_SEC_REST = (
    ("ga", 3 * ATTN_W, _O, "silu", ATTN_W, BF16, False),
    ("uf", _O, _O + FOURIER_W, None, FOURIER_W, F32, False),
    ("gf", _O + FOURIER_W, _O + 2 * FOURIER_W, "silu", FOURIER_W, BF16, False),
    ("u", _O + 2 * FOURIER_W, _O + 2 * FOURIER_W + 2 * CONV_W, "glu", CONV_W, BF16, False),
    ("gc", _O + 2 * FOURIER_W + 2 * CONV_W, _O + 2 * FOURIER_W + 3 * CONV_W, "silu", CONV_W, BF16, False),
)
_S0 = _O + 2 * FOURIER_W + 3 * CONV_W
ALL_SECTIONS = (_SEC_Q, _SEC_K, _SEC_V) + _SEC_REST
KV_SECTIONS = (_SEC_K, _SEC_V)


def _merge_sections(d):
    return tuple((n, _S0 + i * d, _S0 + (i + 1) * d, "sigmoid", d, BF16, False)
                 for i, n in enumerate(("sa", "sf", "sc")))


def _inproj_kernel(x_ref, mod_ref, g_ref, w_ref, *o_refs, sections):
    x = x_ref[0]
    d = x.shape[-1]
    m = mod_ref[0]
    shift, scale = m[:, :d], m[:, d:2 * d]
    y = x * lax.rsqrt(jnp.mean(x * x, axis=-1, keepdims=True) + NORM_EPS) * g_ref[...]
    hb = (y * (1.0 + scale) + shift).astype(BF16)
    epilogue_rank = {"sigmoid": 0, "glu": 1, "silu": 2, "qscale": 3, None: 4}
    order = sorted(range(len(sections)), key=lambda j: epilogue_rank[sections[j][3]])
    for (_, lo, hi, act, _, dtype, paired), o_ref in [(sections[j], o_refs[j]) for j in order]:
        acc = jnp.dot(hb, w_ref[:, lo:hi], preferred_element_type=F32)
        if act == "qscale":
            acc = acc * (LOG2E * HEAD_DIM ** -0.5)
        elif act == "silu":
            acc = _silu(acc)
        elif act == "sigmoid":
            acc = _sigmoid(acc)
        elif act == "glu":
            half = (hi - lo) // 2
            acc = acc[:, :half] * _sigmoid(acc[:, half:])
        if paired:
            for p in range(N_PAIRS):
                o_ref[0, p] = acc[:, p * LANES:(p + 1) * LANES].astype(dtype)
        else:
            o_ref[0] = acc.astype(dtype)


def _inproj(x, mod_l, g_pre_l, w_in_bf, layer, sections, mod_row, tm):
    b, s, d = x.shape
    tm = min(tm, s)
    in_w = w_in_bf.shape[-1]
    out_shapes, out_specs = [], []
    for (_, _, _, _, width, dtype, paired) in sections:
        if paired:
            out_shapes.append(jax.ShapeDtypeStruct((b, N_PAIRS, s, LANES), dtype))
            out_specs.append(pl.BlockSpec((1, N_PAIRS, tm, LANES), lambda bi, i: (bi, 0, i, 0)))
        else:
            out_shapes.append(jax.ShapeDtypeStruct((b, s, width), dtype))
            out_specs.append(pl.BlockSpec((1, tm, width), lambda bi, i: (bi, i, 0)))
    return pl.pallas_call(
        functools.partial(_inproj_kernel, sections=sections),
        grid=(b, s // tm),
        in_specs=[
            pl.BlockSpec((1, tm, d), lambda bi, i: (bi, i, 0)),
            pl.BlockSpec((1, 1, 3 * d), lambda bi, i: (mod_row(bi), 0, 0)),
            pl.BlockSpec((1, d), lambda bi, i: (0, 0)),
            pl.BlockSpec((None, d, in_w), lambda bi, i: (layer, 0, 0)),
        ],
        out_specs=out_specs,
        out_shape=out_shapes,
        compiler_params=_params(2),
        name="inproj",
    )(x, mod_l.reshape(MOD_ROWS, 1, 3 * d), g_pre_l.reshape(1, d), w_in_bf)


PAIR_STARTS = 2 * WIN_ROWS


def _bias_pair_tables(rpb):
    cols = np.arange(GRID_W)
    col_start = np.clip(cols - WIN_COLS // 2, 0, GRID_W - WIN_COLS)
    valid_c = (cols[None, :] >= col_start[:, None]) & (cols[None, :] < col_start[:, None] + WIN_COLS)
    dc = cols[None, :] - cols[:, None] + (WIN_COLS - 1)
    one_hot = (dc[None] == np.arange(2 * WIN_COLS - 1)[:, None, None]) & valid_c[None]
    per_dr = jnp.einsum("lhdp,pck->lhdck", rpb, jnp.asarray(one_hot, F32), precision=lax.Precision.HIGHEST)
    per_dr = jnp.where(valid_c, per_dr * LOG2E, MASKED)
    masked = jnp.full_like(per_dr[:, :, :1], MASKED)
    ext = jnp.concatenate([masked, per_dr, masked], axis=2)
    return jnp.concatenate([ext[:, :, :-1], ext[:, :, 1:]], axis=-1)


def _window_plan(i, ws, rows):
    kr = min(WIN_ROWS, rows)
    low = _head_masks()
    plan = []
    for ri in range(Q_ROWS):
        r = Q_ROWS * i + ri
        rs = jnp.clip(r - kr // 2, 0, rows - kr)
        for jj in range(K_ROWS // 2):
            krow = ws + 2 * jj
            ok_lo = (krow >= rs) & (krow < rs + kr)
            ok_hi = (krow + 1 >= rs) & (krow + 1 < rs + kr)
            ok = jnp.where(low, ok_lo.astype(jnp.int32), ok_hi.astype(jnp.int32)) > 0
            start = jnp.clip(krow - r + WIN_ROWS, 0, PAIR_STARTS - 1)
            plan.append((ok, start))
    return plan


def _window_bias(tab_ref, head, plan):
    n_jj = K_ROWS // 2
    row_blocks = []
    for ri in range(Q_ROWS):
        lane_blocks = [jnp.where(ok, tab_ref[head, start], MASKED) for ok, start in plan[ri * n_jj:(ri + 1) * n_jj]]
        row_blocks.append(jnp.concatenate(lane_blocks, axis=-1))
    return jnp.concatenate(row_blocks, axis=0)


def _head_masks():
    lane = lax.broadcasted_iota(jnp.int32, (1, LANES), 1)
    return lane < HEAD_DIM


def _pick(head, own, other):
    low = _head_masks()
    return jnp.where(low, own, other) if head == 0 else jnp.where(low, other, own)


def _pair_scores(q, keys, biases):
    dn = (((1,), (1,)), ((), ()))
    m_rows = q.shape[0]
    q2 = jnp.concatenate([_pick(head, q, jnp.zeros_like(q)) for head in (0, 1)], axis=0)
    both = [lax.dot_general(q2, k, dn, preferred_element_type=F32) for k in keys]
    out = []
    for head in (0, 1):
        scores = [s[head * m_rows:(head + 1) * m_rows] for s in both]
        scores = [s if bias is None else s + bias for s, bias in zip(scores, biases[head])]
        m = functools.reduce(jnp.maximum, [jnp.max(s, axis=-1, keepdims=True) for s in scores])
        out.append((scores, m))
    return out


def _pv(head, scores, m, vals):
    o = functools.reduce(jnp.add, [
        jnp.dot(jnp.exp2(s - m).astype(BF16), _pick(head, v, jnp.ones_like(v)), preferred_element_type=F32)
        for s, v in zip(scores, vals)])
    return o / pltpu.roll(o, HEAD_DIM, axis=1)


ATTN_BLOCKS_PER_STEP = 2
ATTN_PAIRS_PER_STEP = 4


def _attn_kernel(q_ref, k_ref, v_ref, kc_ref, vc_ref, tab_ref, o_ref, *, rows, pairs):
    tq = Q_ROWS * GRID_W
    staged, starts = [], []
    for blk in range(ATTN_BLOCKS_PER_STEP):
        i = pl.program_id(2) * ATTN_BLOCKS_PER_STEP + blk
        ws = jnp.clip(Q_ROWS * i - min(WIN_ROWS, rows) // 2, 0, rows - K_ROWS)
        start = pl.multiple_of(ws * GRID_W, GRID_W)
        starts.append(start)
        plan = _window_plan(i, ws, rows)
        for p in range(pairs):
            kw = k_ref[0, p, pl.ds(start, K_ROWS * GRID_W), :]
            biases = [(_window_bias(tab_ref, 2 * p + head, plan), None) for head in (0, 1)]
            staged.extend(_pair_scores(q_ref[0, p, blk * tq:(blk + 1) * tq, :], (kw, kc_ref[0, p]), biases))
    outs = []
    for j, (scores, m) in enumerate(staged):
        blk, jj = divmod(j, 2 * pairs)
        p, head = divmod(jj, 2)
        vw = v_ref[0, p, pl.ds(starts[blk], K_ROWS * GRID_W), :]
        outs.append(_pv(head, scores, m, (vw, vc_ref[0, p])))
    for blk in range(ATTN_BLOCKS_PER_STEP):
        for p in range(pairs):
            j = blk * 2 * pairs + 2 * p
            o_ref[0, p, blk * tq:(blk + 1) * tq, :] = _pick(0, outs[j], outs[j + 1]).astype(o_ref.dtype)


def _attention(q, k, v, kc, vc, tab):
    b, _, s, _ = q.shape
    l = kc.shape[2]
    rows = s // GRID_W
    n_blk = rows // Q_ROWS
    tq = Q_ROWS * GRID_W * ATTN_BLOCKS_PER_STEP
    pp = ATTN_PAIRS_PER_STEP

    return pl.pallas_call(
        functools.partial(_attn_kernel, rows=rows, pairs=pp),
        grid=(b, N_PAIRS // pp, n_blk // ATTN_BLOCKS_PER_STEP),
        in_specs=[
            pl.BlockSpec((1, pp, tq, LANES), lambda bi, p, i: (bi, p, i, 0)),
            pl.BlockSpec((1, pp, s, LANES), lambda bi, p, i: (bi, p, 0, 0)),
            pl.BlockSpec((1, pp, s, LANES), lambda bi, p, i: (bi, p, 0, 0)),
            pl.BlockSpec((1, pp, l, LANES), lambda bi, p, i: (bi, p, 0, 0)),
            pl.BlockSpec((1, pp, l, LANES), lambda bi, p, i: (bi, p, 0, 0)),
            pl.BlockSpec((2 * pp, PAIR_STARTS, GRID_W, LANES), lambda bi, p, i: (p, 0, 0, 0)),
        ],
        out_specs=pl.BlockSpec((1, pp, tq, LANES), lambda bi, p, i: (bi, p, i, 0)),
        out_shape=jax.ShapeDtypeStruct((b, N_PAIRS, s, LANES), BF16),
        compiler_params=_params(3),
        name="attention",
    )(q, k, v, kc, vc, tab)


def _ctx_attn_kernel(q_ref, k_ref, v_ref, o_ref):
    staged = []
    for p in range(N_PAIRS):
        staged.extend(_pair_scores(q_ref[0, p], (k_ref[0, p],), [(None,), (None,)]))
    outs = [_pv(j % 2, scores, m, (v_ref[0, j // 2],)) for j, (scores, m) in enumerate(staged)]
    for p in range(N_PAIRS):
        o_ref[0, p] = _pick(0, outs[2 * p], outs[2 * p + 1]).astype(o_ref.dtype)


def _ctx_attention(q, k, v):
    b, _, l, _ = q.shape
    spec = pl.BlockSpec((1, N_PAIRS, l, LANES), lambda bi: (bi, 0, 0, 0))
    return pl.pallas_call(
        _ctx_attn_kernel,
        grid=(b,),
        in_specs=[spec, spec, spec],
        out_specs=spec,
        out_shape=jax.ShapeDtypeStruct((b, N_PAIRS, l, LANES), BF16),
        compiler_params=_params(1),
        name="ctx_attention",
    )(q, k, v)


def _dft_cos_sin(k, n, period):
    ang = (2.0 * np.pi / period) * ((k[:, None] * n[None, :]) % period)
    return np.cos(ang), np.sin(ang)


def _table(a):
    return jnp.asarray(a, F32).astype(BF16)


def _channel_dft():
    m = np.arange(FOURIER_GROUP_DIM)
    c, s = _dft_cos_sin(m, m, FOURIER_GROUP_DIM)
    eye = np.eye(FOURIER_GROUPS)
    scale = FOURIER_GROUP_DIM ** -0.5
    return np.concatenate([np.kron(eye, c.T), np.kron(eye, s.T)], axis=0) * scale


def _channel_weights(cs_ref, wf_ref):
    return jnp.dot(cs_ref[...], wf_ref[...], preferred_element_type=F32).astype(BF16)


def _channel_mix(xr, xi, cw):
    x = jnp.concatenate([xr, xi], axis=-1).astype(BF16)
    return jnp.dot(x, cw, preferred_element_type=F32)


FFT_GROUP = 8


def _fft_kernel(x_ref, w1_ref, m2_ref, cs_ref, wf_ref, pm_ref, o_ref, ar_ref, ai_ref, *, n1):
    g8 = FFT_GROUP
    for g in range(GRID_W // g8):
        xg = x_ref[0, :, g * g8:(g + 1) * g8, :].reshape(n1 * g8, FOURIER_W).astype(BF16)
        a = jnp.dot(w1_ref[...], xg, preferred_element_type=F32)
        ar_ref[:, g * g8:(g + 1) * g8, :] = a[:n1 * g8].reshape(n1, g8, FOURIER_W)
        ai_ref[:, g * g8:(g + 1) * g8, :] = a[n1 * g8:].reshape(n1, g8, FOURIER_W)
    for k1 in range(n1):
        a = jnp.concatenate([ar_ref[k1], ai_ref[k1]], axis=0).astype(BF16)
        x = jnp.dot(m2_ref[k1], a, preferred_element_type=F32)
        ar_ref[k1] = x[:GRID_W]
        ai_ref[k1] = x[GRID_W:]
    cw = _channel_weights(cs_ref, wf_ref)
    for g in range(n1 // g8):
        xr = ar_ref[g * g8:(g + 1) * g8].reshape(g8 * GRID_W, FOURIER_W)
        xi = ai_ref[g * g8:(g + 1) * g8].reshape(g8 * GRID_W, FOURIER_W)
        z = _channel_mix(xr, xi, cw).astype(BF16)
        o = jnp.dot(pm_ref[...], z, preferred_element_type=F32)
        o_ref[0, :, g * g8:(g + 1) * g8, :] = o.reshape(GRID_W, g8, FOURIER_W)


def _fourier(uf, wf_bf):
    b, s, fw = uf.shape
    n1 = s // GRID_W
    g8 = FFT_GROUP
    k1 = np.arange(n1)
    c1, s1 = _dft_cos_sin(k1, k1, n1)
    w1 = np.kron(np.concatenate([c1, -s1], axis=0) * n1 ** -0.5, np.eye(g8))
    n2 = np.arange(GRID_W)
    kk = k1[:, None] + n1 * np.arange(GRID_W)[None, :]
    ang = (2.0 * np.pi / s) * ((kk[:, :, None] * n2[None, None, :]) % s)
    c2, s2 = np.cos(ang), np.sin(ang)
    m2 = np.concatenate([np.concatenate([c2, s2], axis=2),
                         np.concatenate([-s2, c2], axis=2)], axis=1) * GRID_W ** -0.5
    perm = np.zeros((GRID_W * g8, g8 * GRID_W))
    k2g, k1g = np.meshgrid(np.arange(GRID_W), np.arange(g8), indexing="ij")
    perm[(k2g * g8 + k1g).ravel(), (k1g * GRID_W + k2g).ravel()] = 1.0
    out = pl.pallas_call(
        functools.partial(_fft_kernel, n1=n1),
        grid=(b,),
        in_specs=[
            pl.BlockSpec((1, n1, GRID_W, fw), lambda bi: (bi, 0, 0, 0)),
            pl.BlockSpec(w1.shape, lambda bi: (0, 0)),
            pl.BlockSpec((n1, 2 * GRID_W, 2 * GRID_W), lambda bi: (0, 0, 0)),
            pl.BlockSpec((2 * fw, fw), lambda bi: (0, 0)),
            pl.BlockSpec((fw, fw), lambda bi: (0, 0)),
            pl.BlockSpec(perm.shape, lambda bi: (0, 0)),
        ],
        out_specs=pl.BlockSpec((1, GRID_W, n1, fw), lambda bi: (bi, 0, 0, 0)),
        out_shape=jax.ShapeDtypeStruct((b, GRID_W, n1, fw), F32),
        scratch_shapes=[pltpu.VMEM((n1, GRID_W, fw), F32), pltpu.VMEM((n1, GRID_W, fw), F32)],
        compiler_params=_params(1),
        name="fourier",
    )(uf.reshape(b, n1, GRID_W, fw), _table(w1), _table(m2), _table(_channel_dft()), wf_bf, _table(perm))
    return out.reshape(b, s, fw)


def _ctx_fft_kernel(x_ref, wd_ref, cs_ref, wf_ref, o_ref):
    n = x_ref.shape[1]
    x = jnp.dot(wd_ref[...], x_ref[0].astype(BF16), preferred_element_type=F32)
    o_ref[0] = _channel_mix(x[:n], x[n:], _channel_weights(cs_ref, wf_ref))


def _ctx_fourier(uf, wf_bf):
    b, n, fw = uf.shape
    k = np.arange(n)
    c, s = _dft_cos_sin(k, k, n)
    wd = np.concatenate([c, -s], axis=0) * n ** -0.5
    return pl.pallas_call(
        _ctx_fft_kernel,
        grid=(b,),
        in_specs=[
            pl.BlockSpec((1, n, fw), lambda bi: (bi, 0, 0)),
            pl.BlockSpec((2 * n, n), lambda bi: (0, 0)),
            pl.BlockSpec((2 * fw, fw), lambda bi: (0, 0)),
            pl.BlockSpec((fw, fw), lambda bi: (0, 0)),
        ],
        out_specs=pl.BlockSpec((1, n, fw), lambda bi: (bi, 0, 0)),
        out_shape=jax.ShapeDtypeStruct((b, n, fw), F32),
        compiler_params=_params(1),
        name="ctx_fourier",
    )(uf, _table(wd), _table(_channel_dft()), wf_bf)


CONV_PAD = 16
CONV_TILE = 32
SUBLANES = 8


def _conv_tile(up_ref, base, dw_ref):
    first = CONV_PAD - CONV_K // 2
    halo = CONV_TILE + 2 * CONV_PAD
    win = up_ref[pl.ds(base, halo), :]
    acc = jnp.zeros((CONV_TILE, CONV_W), F32)
    for ph in range(SUBLANES):
        taps = [t for t in range(CONV_K) if (first + t) % SUBLANES == ph]
        if not taps:
            continue
        shifted = win if ph == 0 else pltpu.roll(win, halo - ph, axis=0)
        for t in taps:
            off = first + t - ph
            acc = acc + shifted[off:off + CONV_TILE] * dw_ref[t:t + 1, :]
    return acc


def _tail_kernel(x_ref, ao_ref, ga_ref, fw_ref, gf_ref, u_ref, gc_ref, sa_ref, sf_ref, sc_ref, mod_ref, gp_ref,
                 pa_ref, pf_ref, pc_ref, wo_ref, dw_ref, db_ref, lg_ref, lb_ref, pw_ref, o_ref, up_ref):
    d = x_ref.shape[-1]
    tm = x_ref.shape[1]
    n = u_ref.shape[1]
    i = pl.program_id(1)

    @pl.when(i == 0)
    def _():
        zeros = jnp.zeros((CONV_PAD, CONV_W), F32)
        up_ref[0:CONV_PAD, :] = zeros
        up_ref[CONV_PAD + n:CONV_PAD + n + CONV_PAD, :] = zeros
        up_ref[CONV_PAD:CONV_PAD + n, :] = u_ref[0].astype(F32)

    ao = jnp.concatenate([ao_ref[0, p] for p in range(N_PAIRS)], axis=-1)
    ya = jnp.dot(ao * ga_ref[0], pa_ref[...], preferred_element_type=F32)
    yf = jnp.dot((fw_ref[0] * gf_ref[0].astype(F32)).astype(BF16), pf_ref[...], preferred_element_type=F32)
    conv = jnp.concatenate(
        [_conv_tile(up_ref, pl.multiple_of(i * tm + j * CONV_TILE, CONV_TILE), dw_ref)
         for j in range(tm // CONV_TILE)], axis=0) + db_ref[...]
    xc = conv - jnp.mean(conv, axis=-1, keepdims=True)
    var = jnp.mean(xc * xc, axis=-1, keepdims=True)
    cn = xc * lax.rsqrt(var + NORM_EPS) * lg_ref[...] + lb_ref[...]
    cw = jnp.dot(_silu(cn).astype(BF16), pw_ref[...], preferred_element_type=F32) * gc_ref[0].astype(F32)
    yc = jnp.dot(cw.astype(BF16), pc_ref[...], preferred_element_type=F32)
    merged = (sa_ref[0].astype(F32) * ya + sf_ref[0].astype(F32) * yf + sc_ref[0].astype(F32) * yc)
    y = jnp.dot(merged.astype(BF16), wo_ref[...], preferred_element_type=F32)
    yn = y * lax.rsqrt(jnp.mean(y * y, axis=-1, keepdims=True) + NORM_EPS) * gp_ref[...]
    gate = mod_ref[0][:, 2 * d:]
    o_ref[0] = x_ref[0] + gate * yn


def _tail(x, ao, ga, fw, gf, u, gc, sa, sf, sc, mod_l, g_post_l, pa_bf, pf_bf, pc_bf, wo_bf,
          conv_dw_l, conv_db_l, ln_g_l, ln_b_l, w_pw_bf, mod_row, tm):
    b, s, d = x.shape
    tm = min(tm, s)

    def tok(width):
        return pl.BlockSpec((1, tm, width), lambda bi, i: (bi, i, 0))

    def full(shape):
        return pl.BlockSpec(shape, lambda bi, i: (0,) * len(shape))

    return pl.pallas_call(
        _tail_kernel,
        grid=(b, s // tm),
        in_specs=[
            tok(d),
            pl.BlockSpec((1, N_PAIRS, tm, LANES), lambda bi, i: (bi, 0, i, 0)),
            tok(ATTN_W), tok(FOURIER_W), tok(FOURIER_W),
            pl.BlockSpec((1, s, CONV_W), lambda bi, i: (bi, 0, 0)),
            tok(CONV_W), tok(d), tok(d), tok(d),
            pl.BlockSpec((1, 1, 3 * d), lambda bi, i: (mod_row(bi), 0, 0)),
            full((1, d)),
            full(pa_bf.shape), full(pf_bf.shape), full(pc_bf.shape), full(wo_bf.shape),
            full((CONV_K, CONV_W)), full((1, CONV_W)), full((1, CONV_W)), full((1, CONV_W)), full(w_pw_bf.shape),
        ],
        out_specs=tok(d),
        out_shape=jax.ShapeDtypeStruct((b, s, d), F32),
        scratch_shapes=[pltpu.VMEM((s + 2 * CONV_PAD, CONV_W), F32)],
        compiler_params=_params(2),
        name="tail",
    )(x, ao, ga, fw, gf, u, gc, sa, sf, sc, mod_l.reshape(MOD_ROWS, 1, 3 * d), g_post_l.reshape(1, d),
      pa_bf, pf_bf, pc_bf, wo_bf, conv_dw_l, conv_db_l.reshape(1, CONV_W), ln_g_l.reshape(1, CONV_W),
      ln_b_l.reshape(1, CONV_W), w_pw_bf)


INPROJ_TILE = 256
TAIL_TILE = 512


def kernel(x, c, ctx, c_ctx, w_mod, b_mod, g_pre, g_post, w_in, rpb, w_four, conv_dw, conv_db,
           conv_ln_g, conv_ln_b, w_pw, p_attn, p_four, p_conv, w_out):
    batch, seq, d = x.shape
    depth = w_mod.shape[0]
    rows = seq // GRID_W
    assert batch < MOD_ROWS and rows % (Q_ROWS * ATTN_BLOCKS_PER_STEP) == 0 and rows >= K_ROWS
    ctx_row = batch
    c_all = jnp.zeros((MOD_ROWS, d), F32).at[:batch].set(c).at[ctx_row].set(c_ctx)
    mod = _modulation(c_all, w_mod, b_mod)
    merge_sections = _merge_sections(d)
    bias_tabs = _bias_pair_tables(rpb)

    def latent_row(bi):
        return bi

    def context_row(bi):
        return ctx_row

    w_in_bf = w_in.astype(BF16)
    for l in range(depth):
        wf_bf, pw_bf = w_four[l].astype(BF16), w_pw[l].astype(BF16)
        pa_bf, pf_bf, pc_bf, wo_bf = (p_attn[l].astype(BF16), p_four[l].astype(BF16),
                                      p_conv[l].astype(BF16), w_out[l].astype(BF16))
        conv_w = (conv_dw[l], conv_db[l], conv_ln_g[l], conv_ln_b[l], pw_bf)
        tail_w = (mod[l], g_post[l], pa_bf, pf_bf, pc_bf, wo_bf)
        update_ctx = l < depth - 1

        if update_ctx:
            ctx_sections, ctx_w, ctx_layer = ALL_SECTIONS + merge_sections, w_in_bf, l
        else:
            lo, hi = KV_SECTIONS[0][1], KV_SECTIONS[-1][2]
            ctx_sections = tuple((sec[0], sec[1] - lo, sec[2] - lo) + sec[3:] for sec in KV_SECTIONS)
            ctx_w, ctx_layer = w_in_bf[l:l + 1, :, lo:hi], 0
        ctx_out = dict(zip([sec[0] for sec in ctx_sections],
                           _inproj(ctx, mod[l], g_pre[l], ctx_w, ctx_layer, ctx_sections, context_row,
                                   INPROJ_TILE)))

        lat = dict(zip([sec[0] for sec in ALL_SECTIONS + merge_sections],
                       _inproj(x, mod[l], g_pre[l], w_in_bf, l, ALL_SECTIONS + merge_sections, latent_row,
                               INPROJ_TILE)))
        ao = _attention(lat["q"], lat["k"], lat["v"], ctx_out["k"], ctx_out["v"], bias_tabs[l])
        fw = _fourier(lat["uf"], wf_bf)
        x = _tail(x, ao, lat["ga"], fw, lat["gf"], lat["u"], lat["gc"], lat["sa"], lat["sf"], lat["sc"],
                  *tail_w, *conv_w, latent_row, TAIL_TILE)

        if update_ctx:
            co = ctx_out
            ao_c = _ctx_attention(co["q"], co["k"], co["v"])
            fw_c = _ctx_fourier(co["uf"], wf_bf)
            ctx = _tail(ctx, ao_c, co["ga"], fw_c, co["gf"], co["u"], co["gc"], co["sa"], co["sf"], co["sc"],
                        *tail_w, *conv_w, context_row, TAIL_TILE)
    return x
```

```python
import functools

import jax
import jax.numpy as jnp
import numpy as np
from jax import lax
from jax.experimental import pallas as pl
from jax.experimental.pallas import tpu as pltpu

F32 = jnp.float32
BF16 = jnp.bfloat16

GRID_W = 64
N_HEADS = 8
HEAD_DIM = 64
ATTN_W = N_HEADS * HEAD_DIM
LANES = 128
N_PAIRS = ATTN_W // LANES
WIN_ROWS = 8
WIN_COLS = 16
FOURIER_GROUPS = 4
FOURIER_GROUP_DIM = 64
FOURIER_W = FOURIER_GROUPS * FOURIER_GROUP_DIM
CONV_W = 256
CONV_K = 31
NORM_EPS = 1e-6
MASKED = -1e30
LOG2E = 1.4426950408889634

Q_ROWS = 4
K_ROWS = 12
MOD_ROWS = 8

VMEM_LIMIT = 56 * 1024 * 1024


def _sigmoid(x):
    return 1.0 / (1.0 + jnp.exp(-x))


def _silu(x):
    return x * _sigmoid(x)


def _params(n_axes):
    return pltpu.CompilerParams(
        dimension_semantics=("arbitrary",) * n_axes, vmem_limit_bytes=VMEM_LIMIT)


def _mod_kernel(c_ref, w_ref, b_ref, o_ref):
    sc = _silu(c_ref[...]).astype(BF16)
    o_ref[0] = jnp.dot(sc, w_ref[0].astype(BF16), preferred_element_type=F32) + b_ref[0]


def _modulation(c_all, w_mod, b_mod):
    depth, d, d3 = w_mod.shape
    tn = d
    return pl.pallas_call(
        _mod_kernel,
        grid=(depth, d3 // tn),
        in_specs=[
            pl.BlockSpec((MOD_ROWS, d), lambda l, j: (0, 0)),
            pl.BlockSpec((1, d, tn), lambda l, j: (l, 0, j)),
            pl.BlockSpec((1, 1, tn), lambda l, j: (l, 0, j)),
        ],
        out_specs=pl.BlockSpec((1, MOD_ROWS, tn), lambda l, j: (l, 0, j)),
        out_shape=jax.ShapeDtypeStruct((depth, MOD_ROWS, d3), F32),
        compiler_params=_params(2),
        name="modulation",
    )(c_all, w_mod, b_mod.reshape(depth, 1, d3))


_SEC_Q = ("q", 0, ATTN_W, "qscale", ATTN_W, BF16, True)
_SEC_K = ("k", ATTN_W, 2 * ATTN_W, None, ATTN_W, BF16, True)
_SEC_V = ("v", 2 * ATTN_W, 3 * ATTN_W, None, ATTN_W, BF16, True)
_O = 4 * ATTN_W
_SEC_REST = (
    ("ga", 3 * ATTN_W, _O, "silu", ATTN_W, BF16, False),
    ("uf", _O, _O + FOURIER_W, None, FOURIER_W, F32, False),
    ("gf", _O + FOURIER_W, _O + 2 * FOURIER_W, "silu", FOURIER_W, BF16, False),
    ("u", _O + 2 * FOURIER_W, _O + 2 * FOURIER_W + 2 * CONV_W, "glu", CONV_W, BF16, False),
    ("gc", _O + 2 * FOURIER_W + 2 * CONV_W, _O + 2 * FOURIER_W + 3 * CONV_W, "silu", CONV_W, BF16, False),
)
_S0 = _O + 2 * FOURIER_W + 3 * CONV_W
ALL_SECTIONS = (_SEC_Q, _SEC_K, _SEC_V) + _SEC_REST
KV_SECTIONS = (_SEC_K, _SEC_V)


def _merge_sections(d):
    return tuple((n, _S0 + i * d, _S0 + (i + 1) * d, "sigmoid", d, BF16, False)
                 for i, n in enumerate(("sa", "sf", "sc")))


INPROJ_MIN_ROWS = 128
INPROJ_MAX_GROUPS = 4


def _inproj_kernel(x_ref, mod_ref, g_ref, w_ref, *o_refs, sections):
    tm, d = x_ref.shape[1], x_ref.shape[2]
    m = mod_ref[0]
    shift, scale = m[:, :d], m[:, d:2 * d]
    epilogue_rank = {"sigmoid": 0, "glu": 1, "silu": 2, "qscale": 3, None: 4}
    order = sorted(range(len(sections)), key=lambda j: epilogue_rank[sections[j][3]])
    groups = max(1, min(INPROJ_MAX_GROUPS, tm // INPROJ_MIN_ROWS))
    sub = tm // groups
    for h in range(groups):
        r = slice(h * sub, (h + 1) * sub)
        x = x_ref[0, r, :]
        y = x * lax.rsqrt(jnp.mean(x * x, axis=-1, keepdims=True) + NORM_EPS) * g_ref[...]
        hb = (y * (1.0 + scale) + shift).astype(BF16)
        for (_, lo, hi, act, _, dtype, paired), o_ref in [(sections[j], o_refs[j]) for j in order]:
            acc = jnp.dot(hb, w_ref[:, lo:hi], preferred_element_type=F32)
            if act == "qscale":
                acc = acc * (LOG2E * HEAD_DIM ** -0.5)
            elif act == "silu":
                acc = _silu(acc)
            elif act == "sigmoid":
                acc = _sigmoid(acc)
            elif act == "glu":
                half = (hi - lo) // 2
                acc = acc[:, :half] * _sigmoid(acc[:, half:])
            if paired:
                for p in range(N_PAIRS):
                    o_ref[0, p, r, :] = acc[:, p * LANES:(p + 1) * LANES].astype(dtype)
            else:
                o_ref[0, r, :] = acc.astype(dtype)


def _inproj(x, mod_l, g_pre_l, w_in_bf, layer, sections, mod_row, tm):
    b, s, d = x.shape
    tm = min(tm, s)
    in_w = w_in_bf.shape[-1]
    out_shapes, out_specs = [], []
    for (_, _, _, _, width, dtype, paired) in sections:
        if paired:
            out_shapes.append(jax.ShapeDtypeStruct((b, N_PAIRS, s, LANES), dtype))
            out_specs.append(pl.BlockSpec((1, N_PAIRS, tm, LANES), lambda bi, i: (bi, 0, i, 0)))
        else:
            out_shapes.append(jax.ShapeDtypeStruct((b, s, width), dtype))
            out_specs.append(pl.BlockSpec((1, tm, width), lambda bi, i: (bi, i, 0)))
    return pl.pallas_call(
        functools.partial(_inproj_kernel, sections=sections),
        grid=(b, s // tm),
        in_specs=[
            pl.BlockSpec((1, tm, d), lambda bi, i: (bi, i, 0)),
            pl.BlockSpec((1, 1, 3 * d), lambda bi, i: (mod_row(bi), 0, 0)),
            pl.BlockSpec((1, d), lambda bi, i: (0, 0)),
            pl.BlockSpec((None, d, in_w), lambda bi, i: (layer, 0, 0)),
        ],
        out_specs=out_specs,
        out_shape=out_shapes,
        compiler_params=_params(2),
        name="inproj",
    )(x, mod_l.reshape(MOD_ROWS, 1, 3 * d), g_pre_l.reshape(1, d), w_in_bf)


PAIR_STARTS = 2 * WIN_ROWS


def _bias_pair_tables(rpb):
    cols = np.arange(GRID_W)
    col_start = np.clip(cols - WIN_COLS // 2, 0, GRID_W - WIN_COLS)
    valid_c = (cols[None, :] >= col_start[:, None]) & (cols[None, :] < col_start[:, None] + WIN_COLS)
    dc = cols[None, :] - cols[:, None] + (WIN_COLS - 1)
    one_hot = (dc[None] == np.arange(2 * WIN_COLS - 1)[:, None, None]) & valid_c[None]
    per_dr = jnp.einsum("lhdp,pck->lhdck", rpb, jnp.asarray(one_hot, F32), precision=lax.Precision.HIGHEST)
    per_dr = jnp.where(valid_c, per_dr * LOG2E, MASKED)
    masked = jnp.full_like(per_dr[:, :, :1], MASKED)
    ext = jnp.concatenate([masked, per_dr, masked], axis=2)
    return jnp.concatenate([ext[:, :, :-1], ext[:, :, 1:]], axis=-1)


def _window_plan(i, ws, rows):
    kr = min(WIN_ROWS, rows)
    low = _head_masks()
    plan = []
    for ri in range(Q_ROWS):
        r = Q_ROWS * i + ri
        rs = jnp.clip(r - kr // 2, 0, rows - kr)
        for jj in range(K_ROWS // 2):
            krow = ws + 2 * jj
            ok_lo = (krow >= rs) & (krow < rs + kr)
            ok_hi = (krow + 1 >= rs) & (krow + 1 < rs + kr)
            ok = jnp.where(low, ok_lo.astype(jnp.int32), ok_hi.astype(jnp.int32)) > 0
            start = jnp.clip(krow - r + WIN_ROWS, 0, PAIR_STARTS - 1)
            plan.append((ok, start))
    return plan


def _window_bias(tab_ref, head, plan):
    n_jj = K_ROWS // 2
    row_blocks = []
    for ri in range(Q_ROWS):
        lane_blocks = [jnp.where(ok, tab_ref[head, start], MASKED) for ok, start in plan[ri * n_jj:(ri + 1) * n_jj]]
        row_blocks.append(jnp.concatenate(lane_blocks, axis=-1))
    return jnp.concatenate(row_blocks, axis=0)


def _head_masks():
    lane = lax.broadcasted_iota(jnp.int32, (1, LANES), 1)
    return lane < HEAD_DIM


def _pick(head, own, other):
    low = _head_masks()
    return jnp.where(low, own, other) if head == 0 else jnp.where(low, other, own)


def _pair_scores(q, keys, biases):
    dn = (((1,), (1,)), ((), ()))
    m_rows = q.shape[0]
    q2 = jnp.concatenate([_pick(head, q, jnp.zeros_like(q)) for head in (0, 1)], axis=0)
    both = [lax.dot_general(q2, k, dn, preferred_element_type=F32) for k in keys]
    out = []
    for head in (0, 1):
        scores = [s[head * m_rows:(head + 1) * m_rows] for s in both]
        scores = [s if bias is None else s + bias for s, bias in zip(scores, biases[head])]
        m = functools.reduce(jnp.maximum, [jnp.max(s, axis=-1, keepdims=True) for s in scores])
        out.append((scores, m))
    return out


def _pv(head, scores, m, vals):
    o = functools.reduce(jnp.add, [
        jnp.dot(jnp.exp2(s - m).astype(BF16), _pick(head, v, jnp.ones_like(v)), preferred_element_type=F32)
        for s, v in zip(scores, vals)])
    return o / pltpu.roll(o, HEAD_DIM, axis=1)


ATTN_BLOCKS_PER_STEP = 2
ATTN_PAIRS_PER_STEP = 4


def _attn_kernel(q_ref, k_ref, v_ref, kc_ref, vc_ref, tab_ref, o_ref, *, rows, pairs):
    tq = Q_ROWS * GRID_W
    staged, starts = [], []
    for blk in range(ATTN_BLOCKS_PER_STEP):
        i = pl.program_id(2) * ATTN_BLOCKS_PER_STEP + blk
        ws = jnp.clip(Q_ROWS * i - min(WIN_ROWS, rows) // 2, 0, rows - K_ROWS)
        start = pl.multiple_of(ws * GRID_W, GRID_W)
        starts.append(start)
        plan = _window_plan(i, ws, rows)
        for p in range(pairs):
            kw = k_ref[0, p, pl.ds(start, K_ROWS * GRID_W), :]
            biases = [(_window_bias(tab_ref, 2 * p + head, plan), None) for head in (0, 1)]
            staged.extend(_pair_scores(q_ref[0, p, blk * tq:(blk + 1) * tq, :], (kw, kc_ref[0, p]), biases))
    outs = []
    for j, (scores, m) in enumerate(staged):
        blk, jj = divmod(j, 2 * pairs)
        p, head = divmod(jj, 2)
        vw = v_ref[0, p, pl.ds(starts[blk], K_ROWS * GRID_W), :]
        outs.append(_pv(head, scores, m, (vw, vc_ref[0, p])))
    for blk in range(ATTN_BLOCKS_PER_STEP):
        for p in range(pairs):
            j = blk * 2 * pairs + 2 * p
            o_ref[0, p, blk * tq:(blk + 1) * tq, :] = _pick(0, outs[j], outs[j + 1]).astype(o_ref.dtype)


def _attention(q, k, v, kc, vc, tab):
    b, _, s, _ = q.shape
    l = kc.shape[2]
    rows = s // GRID_W
    n_blk = rows // Q_ROWS
    tq = Q_ROWS * GRID_W * ATTN_BLOCKS_PER_STEP
    pp = ATTN_PAIRS_PER_STEP

    return pl.pallas_call(
        functools.partial(_attn_kernel, rows=rows, pairs=pp),
        grid=(b, N_PAIRS // pp, n_blk // ATTN_BLOCKS_PER_STEP),
        in_specs=[
            pl.BlockSpec((1, pp, tq, LANES), lambda bi, p, i: (bi, p, i, 0)),
            pl.BlockSpec((1, pp, s, LANES), lambda bi, p, i: (bi, p, 0, 0)),
            pl.BlockSpec((1, pp, s, LANES), lambda bi, p, i: (bi, p, 0, 0)),
            pl.BlockSpec((1, pp, l, LANES), lambda bi, p, i: (bi, p, 0, 0)),
            pl.BlockSpec((1, pp, l, LANES), lambda bi, p, i: (bi, p, 0, 0)),
            pl.BlockSpec((2 * pp, PAIR_STARTS, GRID_W, LANES), lambda bi, p, i: (p, 0, 0, 0)),
        ],
        out_specs=pl.BlockSpec((1, pp, tq, LANES), lambda bi, p, i: (bi, p, i, 0)),
        out_shape=jax.ShapeDtypeStruct((b, N_PAIRS, s, LANES), BF16),
        compiler_params=_params(3),
        name="attention",
    )(q, k, v, kc, vc, tab)


def _ctx_attn_kernel(q_ref, k_ref, v_ref, o_ref):
    staged = []
    for p in range(N_PAIRS):
        staged.extend(_pair_scores(q_ref[0, p], (k_ref[0, p],), [(None,), (None,)]))
    outs = [_pv(j % 2, scores, m, (v_ref[0, j // 2],)) for j, (scores, m) in enumerate(staged)]
    for p in range(N_PAIRS):
        o_ref[0, p] = _pick(0, outs[2 * p], outs[2 * p + 1]).astype(o_ref.dtype)


def _ctx_attention(q, k, v):
    b, _, l, _ = q.shape
    spec = pl.BlockSpec((1, N_PAIRS, l, LANES), lambda bi: (bi, 0, 0, 0))
    return pl.pallas_call(
        _ctx_attn_kernel,
        grid=(b,),
        in_specs=[spec, spec, spec],
        out_specs=spec,
        out_shape=jax.ShapeDtypeStruct((b, N_PAIRS, l, LANES), BF16),
        compiler_params=_params(1),
        name="ctx_attention",
    )(q, k, v)


def _dft_cos_sin(k, n, period):
    ang = (2.0 * np.pi / period) * ((k[:, None] * n[None, :]) % period)
    return np.cos(ang), np.sin(ang)


def _table(a):
    return jnp.asarray(a, F32).astype(BF16)


def _channel_dft():
    m = np.arange(FOURIER_GROUP_DIM)
    c, s = _dft_cos_sin(m, m, FOURIER_GROUP_DIM)
    eye = np.eye(FOURIER_GROUPS)
    scale = FOURIER_GROUP_DIM ** -0.5
    return np.concatenate([np.kron(eye, c.T), np.kron(eye, s.T)], axis=0) * scale


def _channel_weights(cs_ref, wf_ref):
    return jnp.dot(cs_ref[...], wf_ref[...], preferred_element_type=F32).astype(BF16)


def _channel_mix(xr, xi, cw):
    x = jnp.concatenate([xr, xi], axis=-1).astype(BF16)
    return jnp.dot(x, cw, preferred_element_type=F32)


FFT_GROUP = 8


def _fft_kernel(x_ref, w1_ref, m2_ref, cs_ref, wf_ref, pm_ref, o_ref, ar_ref, ai_ref, *, n1):
    g8 = FFT_GROUP
    for g in range(GRID_W // g8):
        xg = x_ref[0, :, g * g8:(g + 1) * g8, :].reshape(n1 * g8, FOURIER_W).astype(BF16)
        a = jnp.dot(w1_ref[...], xg, preferred_element_type=F32)
        ar_ref[:, g * g8:(g + 1) * g8, :] = a[:n1 * g8].reshape(n1, g8, FOURIER_W)
        ai_ref[:, g * g8:(g + 1) * g8, :] = a[n1 * g8:].reshape(n1, g8, FOURIER_W)
    for k1 in range(n1):
        a = jnp.concatenate([ar_ref[k1], ai_ref[k1]], axis=0).astype(BF16)
        x = jnp.dot(m2_ref[k1], a, preferred_element_type=F32)
        ar_ref[k1] = x[:GRID_W]
        ai_ref[k1] = x[GRID_W:]
    cw = _channel_weights(cs_ref, wf_ref)
    for g in range(n1 // g8):
        xr = ar_ref[g * g8:(g + 1) * g8].reshape(g8 * GRID_W, FOURIER_W)
        xi = ai_ref[g * g8:(g + 1) * g8].reshape(g8 * GRID_W, FOURIER_W)
        z = _channel_mix(xr, xi, cw).astype(BF16)
        o = jnp.dot(pm_ref[...], z, preferred_element_type=F32)
        o_ref[0, :, g * g8:(g + 1) * g8, :] = o.reshape(GRID_W, g8, FOURIER_W)


def _fourier(uf, wf_bf):
    b, s, fw = uf.shape
    n1 = s // GRID_W
    g8 = FFT_GROUP
    k1 = np.arange(n1)
    c1, s1 = _dft_cos_sin(k1, k1, n1)
    w1 = np.kron(np.concatenate([c1, -s1], axis=0) * n1 ** -0.5, np.eye(g8))
    n2 = np.arange(GRID_W)
    kk = k1[:, None] + n1 * np.arange(GRID_W)[None, :]
    ang = (2.0 * np.pi / s) * ((kk[:, :, None] * n2[None, None, :]) % s)
    c2, s2 = np.cos(ang), np.sin(ang)
    m2 = np.concatenate([np.concatenate([c2, s2], axis=2),
                         np.concatenate([-s2, c2], axis=2)], axis=1) * GRID_W ** -0.5
    perm = np.zeros((GRID_W * g8, g8 * GRID_W))
    k2g, k1g = np.meshgrid(np.arange(GRID_W), np.arange(g8), indexing="ij")
    perm[(k2g * g8 + k1g).ravel(), (k1g * GRID_W + k2g).ravel()] = 1.0
    out = pl.pallas_call(
        functools.partial(_fft_kernel, n1=n1),
        grid=(b,),
        in_specs=[
            pl.BlockSpec((1, n1, GRID_W, fw), lambda bi: (bi, 0, 0, 0)),
            pl.BlockSpec(w1.shape, lambda bi: (0, 0)),
            pl.BlockSpec((n1, 2 * GRID_W, 2 * GRID_W), lambda bi: (0, 0, 0)),
            pl.BlockSpec((2 * fw, fw), lambda bi: (0, 0)),
            pl.BlockSpec((fw, fw), lambda bi: (0, 0)),
            pl.BlockSpec(perm.shape, lambda bi: (0, 0)),
        ],
        out_specs=pl.BlockSpec((1, GRID_W, n1, fw), lambda bi: (bi, 0, 0, 0)),
        out_shape=jax.ShapeDtypeStruct((b, GRID_W, n1, fw), F32),
        scratch_shapes=[pltpu.VMEM((n1, GRID_W, fw), F32), pltpu.VMEM((n1, GRID_W, fw), F32)],
        compiler_params=_params(1),
        name="fourier",
    )(uf.reshape(b, n1, GRID_W, fw), _table(w1), _table(m2), _table(_channel_dft()), wf_bf, _table(perm))
    return out.reshape(b, s, fw)


def _ctx_fft_kernel(x_ref, wd_ref, cs_ref, wf_ref, o_ref):
    n = x_ref.shape[1]
    x = jnp.dot(wd_ref[...], x_ref[0].astype(BF16), preferred_element_type=F32)
    o_ref[0] = _channel_mix(x[:n], x[n:], _channel_weights(cs_ref, wf_ref))


def _ctx_fourier(uf, wf_bf):
    b, n, fw = uf.shape
    k = np.arange(n)
    c, s = _dft_cos_sin(k, k, n)
    wd = np.concatenate([c, -s], axis=0) * n ** -0.5
    return pl.pallas_call(
        _ctx_fft_kernel,
        grid=(b,),
        in_specs=[
            pl.BlockSpec((1, n, fw), lambda bi: (bi, 0, 0)),
            pl.BlockSpec((2 * n, n), lambda bi: (0, 0)),
            pl.BlockSpec((2 * fw, fw), lambda bi: (0, 0)),
            pl.BlockSpec((fw, fw), lambda bi: (0, 0)),
        ],
        out_specs=pl.BlockSpec((1, n, fw), lambda bi: (bi, 0, 0)),
        out_shape=jax.ShapeDtypeStruct((b, n, fw), F32),
        compiler_params=_params(1),
        name="ctx_fourier",
    )(uf, _table(wd), _table(_channel_dft()), wf_bf)


CONV_PAD = 16
CONV_TILE = 32
SUBLANES = 8


def _conv_tile(up_ref, base, dw_ref):
    first = CONV_PAD - CONV_K // 2
    halo = CONV_TILE + 2 * CONV_PAD
    win = up_ref[pl.ds(base, halo), :]
    acc = jnp.zeros((CONV_TILE, CONV_W), F32)
    for ph in range(SUBLANES):
        taps = [t for t in range(CONV_K) if (first + t) % SUBLANES == ph]
        if not taps:
            continue
        shifted = win if ph == 0 else pltpu.roll(win, halo - ph, axis=0)
        for t in taps:
            off = first + t - ph
            acc = acc + shifted[off:off + CONV_TILE] * dw_ref[t:t + 1, :]
    return acc


def _tail_kernel(x_ref, ao_ref, ga_ref, fw_ref, gf_ref, u_ref, gc_ref, sa_ref, sf_ref, sc_ref, mod_ref, gp_ref,
                 pa_ref, pf_ref, pc_ref, wo_ref, dw_ref, db_ref, lg_ref, lb_ref, pw_ref, o_ref, up_ref):
    d = x_ref.shape[-1]
    tm = x_ref.shape[1]
    n = u_ref.shape[1]
    i = pl.program_id(1)

    @pl.when(i == 0)
    def _():
        zeros = jnp.zeros((CONV_PAD, CONV_W), F32)
        up_ref[0:CONV_PAD, :] = zeros
        up_ref[CONV_PAD + n:CONV_PAD + n + CONV_PAD, :] = zeros
        up_ref[CONV_PAD:CONV_PAD + n, :] = u_ref[0].astype(F32)

    ao = jnp.concatenate([ao_ref[0, p] for p in range(N_PAIRS)], axis=-1)
    ya = jnp.dot(ao * ga_ref[0], pa_ref[...], preferred_element_type=F32)
    yf = jnp.dot((fw_ref[0] * gf_ref[0].astype(F32)).astype(BF16), pf_ref[...], preferred_element_type=F32)
    conv = jnp.concatenate(
        [_conv_tile(up_ref, pl.multiple_of(i * tm + j * CONV_TILE, CONV_TILE), dw_ref)
         for j in range(tm // CONV_TILE)], axis=0) + db_ref[...]
    xc = conv - jnp.mean(conv, axis=-1, keepdims=True)
    var = jnp.mean(xc * xc, axis=-1, keepdims=True)
    cn = xc * lax.rsqrt(var + NORM_EPS) * lg_ref[...] + lb_ref[...]
    cw = jnp.dot(_silu(cn).astype(BF16), pw_ref[...], preferred_element_type=F32) * gc_ref[0].astype(F32)
    yc = jnp.dot(cw.astype(BF16), pc_ref[...], preferred_element_type=F32)
    merged = (sa_ref[0].astype(F32) * ya + sf_ref[0].astype(F32) * yf + sc_ref[0].astype(F32) * yc)
    y = jnp.dot(merged.astype(BF16), wo_ref[...], preferred_element_type=F32)
    yn = y * lax.rsqrt(jnp.mean(y * y, axis=-1, keepdims=True) + NORM_EPS) * gp_ref[...]
    gate = mod_ref[0][:, 2 * d:]
    o_ref[0] = x_ref[0] + gate * yn


def _tail(x, ao, ga, fw, gf, u, gc, sa, sf, sc, mod_l, g_post_l, pa_bf, pf_bf, pc_bf, wo_bf,
          conv_dw_l, conv_db_l, ln_g_l, ln_b_l, w_pw_bf, mod_row, tm):
    b, s, d = x.shape
    tm = min(tm, s)

    def tok(width):
        return pl.BlockSpec((1, tm, width), lambda bi, i: (bi, i, 0))

    def full(shape):
        return pl.BlockSpec(shape, lambda bi, i: (0,) * len(shape))

    return pl.pallas_call(
        _tail_kernel,
        grid=(b, s // tm),
        in_specs=[
            tok(d),
            pl.BlockSpec((1, N_PAIRS, tm, LANES), lambda bi, i: (bi, 0, i, 0)),
            tok(ATTN_W), tok(FOURIER_W), tok(FOURIER_W),
            pl.BlockSpec((1, s, CONV_W), lambda bi, i: (bi, 0, 0)),
            tok(CONV_W), tok(d), tok(d), tok(d),
            pl.BlockSpec((1, 1, 3 * d), lambda bi, i: (mod_row(bi), 0, 0)),
            full((1, d)),
            full(pa_bf.shape), full(pf_bf.shape), full(pc_bf.shape), full(wo_bf.shape),
            full((CONV_K, CONV_W)), full((1, CONV_W)), full((1, CONV_W)), full((1, CONV_W)), full(w_pw_bf.shape),
        ],
        out_specs=tok(d),
        out_shape=jax.ShapeDtypeStruct((b, s, d), F32),
        scratch_shapes=[pltpu.VMEM((s + 2 * CONV_PAD, CONV_W), F32)],
        compiler_params=_params(2),
        name="tail",
    )(x, ao, ga, fw, gf, u, gc, sa, sf, sc, mod_l.reshape(MOD_ROWS, 1, 3 * d), g_post_l.reshape(1, d),
      pa_bf, pf_bf, pc_bf, wo_bf, conv_dw_l, conv_db_l.reshape(1, CONV_W), ln_g_l.reshape(1, CONV_W),
      ln_b_l.reshape(1, CONV_W), w_pw_bf)


INPROJ_TILE = 512
TAIL_TILE = 512


def kernel(x, c, ctx, c_ctx, w_mod, b_mod, g_pre, g_post, w_in, rpb, w_four, conv_dw, conv_db,
           conv_ln_g, conv_ln_b, w_pw, p_attn, p_four, p_conv, w_out):
    batch, seq, d = x.shape
    depth = w_mod.shape[0]
    rows = seq // GRID_W
    assert batch < MOD_ROWS and rows % (Q_ROWS * ATTN_BLOCKS_PER_STEP) == 0 and rows >= K_ROWS
    ctx_row = batch
    c_all = jnp.zeros((MOD_ROWS, d), F32).at[:batch].set(c).at[ctx_row].set(c_ctx)
    mod = _modulation(c_all, w_mod, b_mod)
    merge_sections = _merge_sections(d)
    bias_tabs = _bias_pair_tables(rpb)

    def latent_row(bi):
        return bi

    def context_row(bi):
        return ctx_row

    w_in_bf = w_in.astype(BF16)
    for l in range(depth):
        wf_bf, pw_bf = w_four[l].astype(BF16), w_pw[l].astype(BF16)
        pa_bf, pf_bf, pc_bf, wo_bf = (p_attn[l].astype(BF16), p_four[l].astype(BF16),
                                      p_conv[l].astype(BF16), w_out[l].astype(BF16))
        conv_w = (conv_dw[l], conv_db[l], conv_ln_g[l], conv_ln_b[l], pw_bf)
        tail_w = (mod[l], g_post[l], pa_bf, pf_bf, pc_bf, wo_bf)
        update_ctx = l < depth - 1

        if update_ctx:
            ctx_sections, ctx_w, ctx_layer = ALL_SECTIONS + merge_sections, w_in_bf, l
        else:
            lo, hi = KV_SECTIONS[0][1], KV_SECTIONS[-1][2]
            ctx_sections = tuple((sec[0], sec[1] - lo, sec[2] - lo) + sec[3:] for sec in KV_SECTIONS)
            ctx_w, ctx_layer = w_in_bf[l:l + 1, :, lo:hi], 0
        ctx_out = dict(zip([sec[0] for sec in ctx_sections],
                           _inproj(ctx, mod[l], g_pre[l], ctx_w, ctx_layer, ctx_sections, context_row,
                                   INPROJ_TILE)))

        lat = dict(zip([sec[0] for sec in ALL_SECTIONS + merge_sections],
                       _inproj(x, mod[l], g_pre[l], w_in_bf, l, ALL_SECTIONS + merge_sections, latent_row,
                               INPROJ_TILE)))
        ao = _attention(lat["q"], lat["k"], lat["v"], ctx_out["k"], ctx_out["v"], bias_tabs[l])
        fw = _fourier(lat["uf"], wf_bf)
        x = _tail(x, ao, lat["ga"], fw, lat["gf"], lat["u"], lat["gc"], lat["sa"], lat["sf"], lat["sc"],
                  *tail_w, *conv_w, latent_row, TAIL_TILE)

        if update_ctx:
            co = ctx_out
            ao_c = _ctx_attention(co["q"], co["k"], co["v"])
            fw_c = _ctx_fourier(co["uf"], wf_bf)
            ctx = _tail(ctx, ao_c, co["ga"], fw_c, co["gf"], co["u"], co["gc"], co["sa"], co["sf"], co["sc"],
                        *tail_w, *conv_w, context_row, TAIL_TILE)
    return x
```

```python
import functools

import jax
import jax.numpy as jnp
import numpy as np
from jax import lax
from jax.experimental import pallas as pl
from jax.experimental.pallas import tpu as pltpu

F32 = jnp.float32
BF16 = jnp.bfloat16

GRID_W = 64
N_HEADS = 8
HEAD_DIM = 64
ATTN_W = N_HEADS * HEAD_DIM
LANES = 128
N_PAIRS = ATTN_W // LANES
WIN_ROWS = 8
WIN_COLS = 16
FOURIER_GROUPS = 4
FOURIER_GROUP_DIM = 64
FOURIER_W = FOURIER_GROUPS * FOURIER_GROUP_DIM
CONV_W = 256
CONV_K = 31
NORM_EPS = 1e-6
MASKED = -1e30
LOG2E = 1.4426950408889634

Q_ROWS = 2
K_ROWS = 10
MOD_ROWS = 8

VMEM_LIMIT = 56 * 1024 * 1024


def _sigmoid(x):
    return 1.0 / (1.0 + jnp.exp(-x))


def _silu(x):
    return x * _sigmoid(x)


def _params(n_axes):
    return pltpu.CompilerParams(
        dimension_semantics=("arbitrary",) * n_axes, vmem_limit_bytes=VMEM_LIMIT)


def _mod_kernel(c_ref, w_ref, b_ref, o_ref):
    sc = _silu(c_ref[...]).astype(BF16)
    o_ref[0] = jnp.dot(sc, w_ref[0].astype(BF16), preferred_element_type=F32) + b_ref[0]


def _modulation(c_all, w_mod, b_mod):
    depth, d, d3 = w_mod.shape
    tn = d
    return pl.pallas_call(
        _mod_kernel,
        grid=(depth, d3 // tn),
        in_specs=[
            pl.BlockSpec((MOD_ROWS, d), lambda l, j: (0, 0)),
            pl.BlockSpec((1, d, tn), lambda l, j: (l, 0, j)),
            pl.BlockSpec((1, 1, tn), lambda l, j: (l, 0, j)),
        ],
        out_specs=pl.BlockSpec((1, MOD_ROWS, tn), lambda l, j: (l, 0, j)),
        out_shape=jax.ShapeDtypeStruct((depth, MOD_ROWS, d3), F32),
        compiler_params=_params(2),
        name="modulation",
    )(c_all, w_mod, b_mod.reshape(depth, 1, d3))


_SEC_Q = ("q", 0, ATTN_W, "qscale", ATTN_W, BF16, True)
_SEC_K = ("k", ATTN_W, 2 * ATTN_W, None, ATTN_W, BF16, True)
_SEC_V = ("v", 2 * ATTN_W, 3 * ATTN_W, None, ATTN_W, BF16, True)
_O = 4 * ATTN_W
_SEC_REST = (
    ("ga", 3 * ATTN_W, _O, "silu", ATTN_W, BF16, False),
    ("uf", _O, _O + FOURIER_W, None, FOURIER_W, F32, False),
    ("gf", _O + FOURIER_W, _O + 2 * FOURIER_W, "silu", FOURIER_W, BF16, False),
    ("u", _O + 2 * FOURIER_W, _O + 2 * FOURIER_W + 2 * CONV_W, "glu", CONV_W, BF16, False),
    ("gc", _O + 2 * FOURIER_W + 2 * CONV_W, _O + 2 * FOURIER_W + 3 * CONV_W, "silu", CONV_W, BF16, False),
)
_S0 = _O + 2 * FOURIER_W + 3 * CONV_W
ALL_SECTIONS = (_SEC_Q, _SEC_K, _SEC_V) + _SEC_REST
KV_SECTIONS = (_SEC_K, _SEC_V)


def _merge_sections(d):
    return tuple((n, _S0 + i * d, _S0 + (i + 1) * d, "sigmoid", d, BF16, False)
                 for i, n in enumerate(("sa", "sf", "sc")))


INPROJ_MIN_ROWS = 128
INPROJ_MAX_GROUPS = 4


def _inproj_kernel(x_ref, mod_ref, g_ref, w_ref, *o_refs, sections):
    tm, d = x_ref.shape[1], x_ref.shape[2]
    m = mod_ref[0]
    shift, scale = m[:, :d], m[:, d:2 * d]
    epilogue_rank = {"sigmoid": 0, "glu": 1, "silu": 2, "qscale": 3, None: 4}
    order = sorted(range(len(sections)), key=lambda j: epilogue_rank[sections[j][3]])
    groups = max(1, min(INPROJ_MAX_GROUPS, tm // INPROJ_MIN_ROWS))
    sub = tm // groups
    for h in range(groups):
        r = slice(h * sub, (h + 1) * sub)
        x = x_ref[0, r, :]
        y = x * lax.rsqrt(jnp.mean(x * x, axis=-1, keepdims=True) + NORM_EPS) * g_ref[...]
        hb = (y * (1.0 + scale) + shift).astype(BF16)
        for (_, lo, hi, act, _, dtype, paired), o_ref in [(sections[j], o_refs[j]) for j in order]:
            acc = jnp.dot(hb, w_ref[:, lo:hi], preferred_element_type=F32)
            if act == "qscale":
                acc = acc * (LOG2E * HEAD_DIM ** -0.5)
            elif act == "silu":
                acc = _silu(acc)
            elif act == "sigmoid":
                acc = _sigmoid(acc)
            elif act == "glu":
                half = (hi - lo) // 2
                acc = acc[:, :half] * _sigmoid(acc[:, half:])
            if paired:
                for p in range(N_PAIRS):
                    o_ref[0, p, r, :] = acc[:, p * LANES:(p + 1) * LANES].astype(dtype)
            else:
                o_ref[0, r, :] = acc.astype(dtype)


def _inproj(x, mod_l, g_pre_l, w_in_bf, layer, sections, mod_row, tm):
    b, s, d = x.shape
    tm = min(tm, s)
    in_w = w_in_bf.shape[-1]
    out_shapes, out_specs = [], []
    for (_, _, _, _, width, dtype, paired) in sections:
        if paired:
            out_shapes.append(jax.ShapeDtypeStruct((b, N_PAIRS, s, LANES), dtype))
            out_specs.append(pl.BlockSpec((1, N_PAIRS, tm, LANES), lambda bi, i: (bi, 0, i, 0)))
        else:
            out_shapes.append(jax.ShapeDtypeStruct((b, s, width), dtype))
            out_specs.append(pl.BlockSpec((1, tm, width), lambda bi, i: (bi, i, 0)))
    return pl.pallas_call(
        functools.partial(_inproj_kernel, sections=sections),
        grid=(b, s // tm),
        in_specs=[
            pl.BlockSpec((1, tm, d), lambda bi, i: (bi, i, 0)),
            pl.BlockSpec((1, 1, 3 * d), lambda bi, i: (mod_row(bi), 0, 0)),
            pl.BlockSpec((1, d), lambda bi, i: (0, 0)),
            pl.BlockSpec((None, d, in_w), lambda bi, i: (layer, 0, 0)),
        ],
        out_specs=out_specs,
        out_shape=out_shapes,
        compiler_params=_params(2),
        name="inproj",
    )(x, mod_l.reshape(MOD_ROWS, 1, 3 * d), g_pre_l.reshape(1, d), w_in_bf)


PAIR_STARTS = 2 * WIN_ROWS


def _bias_pair_tables(rpb):
    cols = np.arange(GRID_W)
    col_start = np.clip(cols - WIN_COLS // 2, 0, GRID_W - WIN_COLS)
    valid_c = (cols[None, :] >= col_start[:, None]) & (cols[None, :] < col_start[:, None] + WIN_COLS)
    dc = cols[None, :] - cols[:, None] + (WIN_COLS - 1)
    one_hot = (dc[None] == np.arange(2 * WIN_COLS - 1)[:, None, None]) & valid_c[None]
    per_dr = jnp.einsum("lhdp,pck->lhdck", rpb, jnp.asarray(one_hot, F32), precision=lax.Precision.HIGHEST)
    per_dr = jnp.where(valid_c, per_dr * LOG2E, MASKED)
    masked = jnp.full_like(per_dr[:, :, :1], MASKED)
    ext = jnp.concatenate([masked, per_dr, masked], axis=2)
    return jnp.concatenate([ext[:, :, :-1], ext[:, :, 1:]], axis=-1)


def _window_plan(i, ws, rows):
    kr = min(WIN_ROWS, rows)
    low = _head_masks()
    plan = []
    for ri in range(Q_ROWS):
        r = Q_ROWS * i + ri
        rs = jnp.clip(r - kr // 2, 0, rows - kr)
        for jj in range(K_ROWS // 2):
            krow = ws + 2 * jj
            ok_lo = (krow >= rs) & (krow < rs + kr)
            ok_hi = (krow + 1 >= rs) & (krow + 1 < rs + kr)
            ok = jnp.where(low, ok_lo.astype(jnp.int32), ok_hi.astype(jnp.int32)) > 0
            start = jnp.clip(krow - r + WIN_ROWS, 0, PAIR_STARTS - 1)
            plan.append((ok, start))
    return plan


def _window_bias(tab_ref, head, plan):
    n_jj = K_ROWS // 2
    row_blocks = []
    for ri in range(Q_ROWS):
        lane_blocks = [jnp.where(ok, tab_ref[head, start], MASKED) for ok, start in plan[ri * n_jj:(ri + 1) * n_jj]]
        row_blocks.append(jnp.concatenate(lane_blocks, axis=-1))
    return jnp.concatenate(row_blocks, axis=0)


def _head_masks():
    lane = lax.broadcasted_iota(jnp.int32, (1, LANES), 1)
    return lane < HEAD_DIM


def _pick(head, own, other):
    low = _head_masks()
    return jnp.where(low, own, other) if head == 0 else jnp.where(low, other, own)


def _pair_scores(q, keys, biases):
    dn = (((1,), (1,)), ((), ()))
    m_rows = q.shape[0]
    q2 = jnp.concatenate([_pick(head, q, jnp.zeros_like(q)) for head in (0, 1)], axis=0)
    both = [lax.dot_general(q2, k, dn, preferred_element_type=F32) for k in keys]
    out = []
    for head in (0, 1):
        scores = [s[head * m_rows:(head + 1) * m_rows] for s in both]
        scores = [s if bias is None else s + bias for s, bias in zip(scores, biases[head])]
        m = functools.reduce(jnp.maximum, [jnp.max(s, axis=-1, keepdims=True) for s in scores])
        out.append((scores, m))
    return out


def _pv(head, scores, m, vals):
    o = functools.reduce(jnp.add, [
        jnp.dot(jnp.exp2(s - m).astype(BF16), _pick(head, v, jnp.ones_like(v)), preferred_element_type=F32)
        for s, v in zip(scores, vals)])
    return o / pltpu.roll(o, HEAD_DIM, axis=1)


ATTN_BLOCKS_PER_STEP = 4
ATTN_PAIRS_PER_STEP = 4


def _attn_kernel(q_ref, k_ref, v_ref, kc_ref, vc_ref, tab_ref, o_ref, *, rows, pairs):
    tq = Q_ROWS * GRID_W
    staged, starts = [], []
    for blk in range(ATTN_BLOCKS_PER_STEP):
        i = pl.program_id(2) * ATTN_BLOCKS_PER_STEP + blk
        ws = jnp.clip(Q_ROWS * i - min(WIN_ROWS, rows) // 2, 0, rows - K_ROWS)
        start = pl.multiple_of(ws * GRID_W, GRID_W)
        starts.append(start)
        plan = _window_plan(i, ws, rows)
        for p in range(pairs):
            kw = k_ref[0, p, pl.ds(start, K_ROWS * GRID_W), :]
            biases = [(_window_bias(tab_ref, 2 * p + head, plan), None) for head in (0, 1)]
            staged.extend(_pair_scores(q_ref[0, p, blk * tq:(blk + 1) * tq, :], (kw, kc_ref[0, p]), biases))
    outs = []
    for j, (scores, m) in enumerate(staged):
        blk, jj = divmod(j, 2 * pairs)
        p, head = divmod(jj, 2)
        vw = v_ref[0, p, pl.ds(starts[blk], K_ROWS * GRID_W), :]
        outs.append(_pv(head, scores, m, (vw, vc_ref[0, p])))
    for blk in range(ATTN_BLOCKS_PER_STEP):
        for p in range(pairs):
            j = blk * 2 * pairs + 2 * p
            o_ref[0, p, blk * tq:(blk + 1) * tq, :] = _pick(0, outs[j], outs[j + 1]).astype(o_ref.dtype)


def _attention(q, k, v, kc, vc, tab):
    b, _, s, _ = q.shape
    l = kc.shape[2]
    rows = s // GRID_W
    n_blk = rows // Q_ROWS
    tq = Q_ROWS * GRID_W * ATTN_BLOCKS_PER_STEP
    pp = ATTN_PAIRS_PER_STEP

    return pl.pallas_call(
        functools.partial(_attn_kernel, rows=rows, pairs=pp),
        grid=(b, N_PAIRS // pp, n_blk // ATTN_BLOCKS_PER_STEP),
        in_specs=[
            pl.BlockSpec((1, pp, tq, LANES), lambda bi, p, i: (bi, p, i, 0)),
            pl.BlockSpec((1, pp, s, LANES), lambda bi, p, i: (bi, p, 0, 0)),
            pl.BlockSpec((1, pp, s, LANES), lambda bi, p, i: (bi, p, 0, 0)),
            pl.BlockSpec((1, pp, l, LANES), lambda bi, p, i: (bi, p, 0, 0)),
            pl.BlockSpec((1, pp, l, LANES), lambda bi, p, i: (bi, p, 0, 0)),
            pl.BlockSpec((2 * pp, PAIR_STARTS, GRID_W, LANES), lambda bi, p, i: (p, 0, 0, 0)),
        ],
        out_specs=pl.BlockSpec((1, pp, tq, LANES), lambda bi, p, i: (bi, p, i, 0)),
        out_shape=jax.ShapeDtypeStruct((b, N_PAIRS, s, LANES), BF16),
        compiler_params=_params(3),
        name="attention",
    )(q, k, v, kc, vc, tab)


def _ctx_attn_kernel(q_ref, k_ref, v_ref, o_ref):
    staged = []
    for p in range(N_PAIRS):
        staged.extend(_pair_scores(q_ref[0, p], (k_ref[0, p],), [(None,), (None,)]))
    outs = [_pv(j % 2, scores, m, (v_ref[0, j // 2],)) for j, (scores, m) in enumerate(staged)]
    for p in range(N_PAIRS):
        o_ref[0, p] = _pick(0, outs[2 * p], outs[2 * p + 1]).astype(o_ref.dtype)


def _ctx_attention(q, k, v):
    b, _, l, _ = q.shape
    spec = pl.BlockSpec((1, N_PAIRS, l, LANES), lambda bi: (bi, 0, 0, 0))
    return pl.pallas_call(
        _ctx_attn_kernel,
        grid=(b,),
        in_specs=[spec, spec, spec],
        out_specs=spec,
        out_shape=jax.ShapeDtypeStruct((b, N_PAIRS, l, LANES), BF16),
        compiler_params=_params(1),
        name="ctx_attention",
    )(q, k, v)


def _dft_cos_sin(k, n, period):
    ang = (2.0 * np.pi / period) * ((k[:, None] * n[None, :]) % period)
    return np.cos(ang), np.sin(ang)


def _table(a):
    return jnp.asarray(a, F32).astype(BF16)


def _channel_dft():
    m = np.arange(FOURIER_GROUP_DIM)
    c, s = _dft_cos_sin(m, m, FOURIER_GROUP_DIM)
    eye = np.eye(FOURIER_GROUPS)
    scale = FOURIER_GROUP_DIM ** -0.5
    return np.concatenate([np.kron(eye, c.T), np.kron(eye, s.T)], axis=0) * scale


def _channel_weights(cs_ref, wf_ref):
    return jnp.dot(cs_ref[...], wf_ref[...], preferred_element_type=F32).astype(BF16)


def _channel_mix(xr, xi, cw):
    x = jnp.concatenate([xr, xi], axis=-1).astype(BF16)
    return jnp.dot(x, cw, preferred_element_type=F32)


FFT_GROUP = 8


def _fft_kernel(x_ref, w1_ref, m2_ref, cs_ref, wf_ref, pm_ref, o_ref, ar_ref, ai_ref, *, n1):
    g8 = FFT_GROUP
    for g in range(GRID_W // g8):
        xg = x_ref[0, :, g * g8:(g + 1) * g8, :].reshape(n1 * g8, FOURIER_W).astype(BF16)
        a = jnp.dot(w1_ref[...], xg, preferred_element_type=F32)
        ar_ref[:, g * g8:(g + 1) * g8, :] = a[:n1 * g8].reshape(n1, g8, FOURIER_W)
        ai_ref[:, g * g8:(g + 1) * g8, :] = a[n1 * g8:].reshape(n1, g8, FOURIER_W)
    for k1 in range(n1):
        a = jnp.concatenate([ar_ref[k1], ai_ref[k1]], axis=0).astype(BF16)
        x = jnp.dot(m2_ref[k1], a, preferred_element_type=F32)
        ar_ref[k1] = x[:GRID_W]
        ai_ref[k1] = x[GRID_W:]
    cw = _channel_weights(cs_ref, wf_ref)
    for g in range(n1 // g8):
        xr = ar_ref[g * g8:(g + 1) * g8].reshape(g8 * GRID_W, FOURIER_W)
        xi = ai_ref[g * g8:(g + 1) * g8].reshape(g8 * GRID_W, FOURIER_W)
        z = _channel_mix(xr, xi, cw).astype(BF16)
        o = jnp.dot(pm_ref[...], z, preferred_element_type=F32)
        o_ref[0, :, g * g8:(g + 1) * g8, :] = o.reshape(GRID_W, g8, FOURIER_W)


def _fourier(uf, wf_bf):
    b, s, fw = uf.shape
    n1 = s // GRID_W
    g8 = FFT_GROUP
    k1 = np.arange(n1)
    c1, s1 = _dft_cos_sin(k1, k1, n1)
    w1 = np.kron(np.concatenate([c1, -s1], axis=0) * n1 ** -0.5, np.eye(g8))
    n2 = np.arange(GRID_W)
    kk = k1[:, None] + n1 * np.arange(GRID_W)[None, :]
    ang = (2.0 * np.pi / s) * ((kk[:, :, None] * n2[None, None, :]) % s)
    c2, s2 = np.cos(ang), np.sin(ang)
    m2 = np.concatenate([np.concatenate([c2, s2], axis=2),
                         np.concatenate([-s2, c2], axis=2)], axis=1) * GRID_W ** -0.5
    perm = np.zeros((GRID_W * g8, g8 * GRID_W))
    k2g, k1g = np.meshgrid(np.arange(GRID_W), np.arange(g8), indexing="ij")
    perm[(k2g * g8 + k1g).ravel(), (k1g * GRID_W + k2g).ravel()] = 1.0
    out = pl.pallas_call(
        functools.partial(_fft_kernel, n1=n1),
        grid=(b,),
        in_specs=[
            pl.BlockSpec((1, n1, GRID_W, fw), lambda bi: (bi, 0, 0, 0)),
            pl.BlockSpec(w1.shape, lambda bi: (0, 0)),
            pl.BlockSpec((n1, 2 * GRID_W, 2 * GRID_W), lambda bi: (0, 0, 0)),
            pl.BlockSpec((2 * fw, fw), lambda bi: (0, 0)),
            pl.BlockSpec((fw, fw), lambda bi: (0, 0)),
            pl.BlockSpec(perm.shape, lambda bi: (0, 0)),
        ],
        out_specs=pl.BlockSpec((1, GRID_W, n1, fw), lambda bi: (bi, 0, 0, 0)),
        out_shape=jax.ShapeDtypeStruct((b, GRID_W, n1, fw), F32),
        scratch_shapes=[pltpu.VMEM((n1, GRID_W, fw), F32), pltpu.VMEM((n1, GRID_W, fw), F32)],
        compiler_params=_params(1),
        name="fourier",
    )(uf.reshape(b, n1, GRID_W, fw), _table(w1), _table(m2), _table(_channel_dft()), wf_bf, _table(perm))
    return out.reshape(b, s, fw)


def _ctx_fft_kernel(x_ref, wd_ref, cs_ref, wf_ref, o_ref):
    n = x_ref.shape[1]
    x = jnp.dot(wd_ref[...], x_ref[0].astype(BF16), preferred_element_type=F32)
    o_ref[0] = _channel_mix(x[:n], x[n:], _channel_weights(cs_ref, wf_ref))


def _ctx_fourier(uf, wf_bf):
    b, n, fw = uf.shape
    k = np.arange(n)
    c, s = _dft_cos_sin(k, k, n)
    wd = np.concatenate([c, -s], axis=0) * n ** -0.5
    return pl.pallas_call(
        _ctx_fft_kernel,
        grid=(b,),
        in_specs=[
            pl.BlockSpec((1, n, fw), lambda bi: (bi, 0, 0)),
            pl.BlockSpec((2 * n, n), lambda bi: (0, 0)),
            pl.BlockSpec((2 * fw, fw), lambda bi: (0, 0)),
            pl.BlockSpec((fw, fw), lambda bi: (0, 0)),
        ],
        out_specs=pl.BlockSpec((1, n, fw), lambda bi: (bi, 0, 0)),
        out_shape=jax.ShapeDtypeStruct((b, n, fw), F32),
        compiler_params=_params(1),
        name="ctx_fourier",
    )(uf, _table(wd), _table(_channel_dft()), wf_bf)


CONV_PAD = 16
CONV_TILE = 64
TAIL_CHUNK = 256
SUBLANES = 8


def _conv_tile(up_ref, base, dw_ref):
    first = CONV_PAD - CONV_K // 2
    halo = CONV_TILE + 2 * CONV_PAD
    win = up_ref[pl.ds(base, halo), :]
    acc = jnp.zeros((CONV_TILE, CONV_W), F32)
    for ph in range(SUBLANES):
        taps = [t for t in range(CONV_K) if (first + t) % SUBLANES == ph]
        if not taps:
            continue
        shifted = win if ph == 0 else pltpu.roll(win, halo - ph, axis=0)
        for t in taps:
            off = first + t - ph
            acc = acc + shifted[off:off + CONV_TILE] * dw_ref[t:t + 1, :]
    return acc


def _tail_kernel(x_ref, ao_ref, ga_ref, fw_ref, gf_ref, u_ref, gc_ref, sa_ref, sf_ref, sc_ref, mod_ref, gp_ref,
                 pa_ref, pf_ref, pc_ref, wo_ref, dw_ref, db_ref, lg_ref, lb_ref, pw_ref, o_ref, up_ref):
    d = x_ref.shape[-1]
    tm = x_ref.shape[1]
    n = u_ref.shape[1]
    i = pl.program_id(1)

    @pl.when(i == 0)
    def _():
        zeros = jnp.zeros((CONV_PAD, CONV_W), F32)
        up_ref[0:CONV_PAD, :] = zeros
        up_ref[CONV_PAD + n:CONV_PAD + n + CONV_PAD, :] = zeros
        up_ref[CONV_PAD:CONV_PAD + n, :] = u_ref[0].astype(F32)

    ao = jnp.concatenate([ao_ref[0, p] for p in range(N_PAIRS)], axis=-1)
    a_in = ao * ga_ref[0]
    f_in = (fw_ref[0] * gf_ref[0].astype(F32)).astype(BF16)
    n_pieces = tm // CONV_TILE
    n_chunks = d // TAIL_CHUNK
    per_dot = -(-n_pieces // (2 * n_chunks))
    conv_pieces, ya, yf = [], [], []

    def some_conv():
        for j in range(len(conv_pieces), min(len(conv_pieces) + per_dot, n_pieces)):
            conv_pieces.append(_conv_tile(up_ref, pl.multiple_of(i * tm + j * CONV_TILE, CONV_TILE), dw_ref))

    for c in range(n_chunks):
        cols = slice(c * TAIL_CHUNK, (c + 1) * TAIL_CHUNK)
        ya.append(jnp.dot(a_in, pa_ref[:, cols], preferred_element_type=F32))
        some_conv()
        yf.append(jnp.dot(f_in, pf_ref[:, cols], preferred_element_type=F32))
        some_conv()
    ya, yf = jnp.concatenate(ya, axis=-1), jnp.concatenate(yf, axis=-1)
    conv = jnp.concatenate(conv_pieces, axis=0) + db_ref[...]
    xc = conv - jnp.mean(conv, axis=-1, keepdims=True)
    var = jnp.mean(xc * xc, axis=-1, keepdims=True)
    cn = xc * lax.rsqrt(var + NORM_EPS) * lg_ref[...] + lb_ref[...]
    cw = jnp.dot(_silu(cn).astype(BF16), pw_ref[...], preferred_element_type=F32) * gc_ref[0].astype(F32)
    yc = jnp.dot(cw.astype(BF16), pc_ref[...], preferred_element_type=F32)
    merged = (sa_ref[0].astype(F32) * ya + sf_ref[0].astype(F32) * yf + sc_ref[0].astype(F32) * yc)
    y = jnp.dot(merged.astype(BF16), wo_ref[...], preferred_element_type=F32)
    yn = y * lax.rsqrt(jnp.mean(y * y, axis=-1, keepdims=True) + NORM_EPS) * gp_ref[...]
    gate = mod_ref[0][:, 2 * d:]
    o_ref[0] = x_ref[0] + gate * yn


def _tail(x, ao, ga, fw, gf, u, gc, sa, sf, sc, mod_l, g_post_l, pa_bf, pf_bf, pc_bf, wo_bf,
          conv_dw_l, conv_db_l, ln_g_l, ln_b_l, w_pw_bf, mod_row, tm):
    b, s, d = x.shape
    tm = min(tm, s)

    def tok(width):
        return pl.BlockSpec((1, tm, width), lambda bi, i: (bi, i, 0))

    def full(shape):
        return pl.BlockSpec(shape, lambda bi, i: (0,) * len(shape))

    return pl.pallas_call(
        _tail_kernel,
        grid=(b, s // tm),
        in_specs=[
            tok(d),
            pl.BlockSpec((1, N_PAIRS, tm, LANES), lambda bi, i: (bi, 0, i, 0)),
            tok(ATTN_W), tok(FOURIER_W), tok(FOURIER_W),
            pl.BlockSpec((1, s, CONV_W), lambda bi, i: (bi, 0, 0)),
            tok(CONV_W), tok(d), tok(d), tok(d),
            pl.BlockSpec((1, 1, 3 * d), lambda bi, i: (mod_row(bi), 0, 0)),
            full((1, d)),
            full(pa_bf.shape), full(pf_bf.shape), full(pc_bf.shape), full(wo_bf.shape),
            full((CONV_K, CONV_W)), full((1, CONV_W)), full((1, CONV_W)), full((1, CONV_W)), full(w_pw_bf.shape),
        ],
        out_specs=tok(d),
        out_shape=jax.ShapeDtypeStruct((b, s, d), F32),
        scratch_shapes=[pltpu.VMEM((s + 2 * CONV_PAD, CONV_W), F32)],
        compiler_params=_params(2),
        name="tail",
    )(x, ao, ga, fw, gf, u, gc, sa, sf, sc, mod_l.reshape(MOD_ROWS, 1, 3 * d), g_post_l.reshape(1, d),
      pa_bf, pf_bf, pc_bf, wo_bf, conv_dw_l, conv_db_l.reshape(1, CONV_W), ln_g_l.reshape(1, CONV_W),
      ln_b_l.reshape(1, CONV_W), w_pw_bf)


INPROJ_TILE = 512
TAIL_TILE = 512


def kernel(x, c, ctx, c_ctx, w_mod, b_mod, g_pre, g_post, w_in, rpb, w_four, conv_dw, conv_db,
           conv_ln_g, conv_ln_b, w_pw, p_attn, p_four, p_conv, w_out):
    batch, seq, d = x.shape
    depth = w_mod.shape[0]
    rows = seq // GRID_W
    assert batch < MOD_ROWS and rows % (Q_ROWS * ATTN_BLOCKS_PER_STEP) == 0 and rows >= K_ROWS
    ctx_row = batch
    c_all = jnp.zeros((MOD_ROWS, d), F32).at[:batch].set(c).at[ctx_row].set(c_ctx)
    mod = _modulation(c_all, w_mod, b_mod)
    merge_sections = _merge_sections(d)
    bias_tabs = _bias_pair_tables(rpb)

    def latent_row(bi):
        return bi

    def context_row(bi):
        return ctx_row

    w_in_bf = w_in.astype(BF16)
    for l in range(depth):
        wf_bf, pw_bf = w_four[l].astype(BF16), w_pw[l].astype(BF16)
        pa_bf, pf_bf, pc_bf, wo_bf = (p_attn[l].astype(BF16), p_four[l].astype(BF16),
                                      p_conv[l].astype(BF16), w_out[l].astype(BF16))
        conv_w = (conv_dw[l], conv_db[l], conv_ln_g[l], conv_ln_b[l], pw_bf)
        tail_w = (mod[l], g_post[l], pa_bf, pf_bf, pc_bf, wo_bf)
        update_ctx = l < depth - 1

        if update_ctx:
            ctx_sections, ctx_w, ctx_layer = ALL_SECTIONS + merge_sections, w_in_bf, l
        else:
            lo, hi = KV_SECTIONS[0][1], KV_SECTIONS[-1][2]
            ctx_sections = tuple((sec[0], sec[1] - lo, sec[2] - lo) + sec[3:] for sec in KV_SECTIONS)
            ctx_w, ctx_layer = w_in_bf[l:l + 1, :, lo:hi], 0
        ctx_out = dict(zip([sec[0] for sec in ctx_sections],
                           _inproj(ctx, mod[l], g_pre[l], ctx_w, ctx_layer, ctx_sections, context_row,
                                   INPROJ_TILE)))

        lat = dict(zip([sec[0] for sec in ALL_SECTIONS + merge_sections],
                       _inproj(x, mod[l], g_pre[l], w_in_bf, l, ALL_SECTIONS + merge_sections, latent_row,
                               INPROJ_TILE)))
        ao = _attention(lat["q"], lat["k"], lat["v"], ctx_out["k"], ctx_out["v"], bias_tabs[l])
        fw = _fourier(lat["uf"], wf_bf)
        x = _tail(x, ao, lat["ga"], fw, lat["gf"], lat["u"], lat["gc"], lat["sa"], lat["sf"], lat["sc"],
                  *tail_w, *conv_w, latent_row, TAIL_TILE)

        if update_ctx:
            co = ctx_out
            ao_c = _ctx_attention(co["q"], co["k"], co["v"])
            fw_c = _ctx_fourier(co["uf"], wf_bf)
            ctx = _tail(ctx, ao_c, co["ga"], fw_c, co["gf"], co["u"], co["gc"], co["sa"], co["sf"], co["sc"],
                        *tail_w, *conv_w, context_row, TAIL_TILE)
    return x
```

```python
import functools

import jax
import jax.numpy as jnp
import numpy as np
from jax import lax
from jax.experimental import pallas as pl
from jax.experimental.pallas import tpu as pltpu

F32 = jnp.float32
BF16 = jnp.bfloat16

GRID_W = 64
N_HEADS = 8
HEAD_DIM = 64
ATTN_W = N_HEADS * HEAD_DIM
LANES = 128
N_PAIRS = ATTN_W // LANES
WIN_ROWS = 8
WIN_COLS = 16
FOURIER_GROUPS = 4
FOURIER_GROUP_DIM = 64
FOURIER_W = FOURIER_GROUPS * FOURIER_GROUP_DIM
CONV_W = 256
CONV_K = 31
NORM_EPS = 1e-6
MASKED = -1e30
LOG2E = 1.4426950408889634

Q_ROWS = 2
K_ROWS = 10
MOD_ROWS = 8

VMEM_LIMIT = 56 * 1024 * 1024


def _sigmoid(x):
    return 1.0 / (1.0 + jnp.exp(-x))


def _silu(x):
    return x * _sigmoid(x)


def _params(n_axes):
    return pltpu.CompilerParams(
        dimension_semantics=("arbitrary",) * n_axes, vmem_limit_bytes=VMEM_LIMIT)


def _mod_kernel(c_ref, w_ref, b_ref, o_ref):
    sc = _silu(c_ref[...]).astype(BF16)
    o_ref[0] = jnp.dot(sc, w_ref[0].astype(BF16), preferred_element_type=F32) + b_ref[0]


def _modulation(c_all, w_mod, b_mod):
    depth, d, d3 = w_mod.shape
    tn = d
    return pl.pallas_call(
        _mod_kernel,
        grid=(depth, d3 // tn),
        in_specs=[
            pl.BlockSpec((MOD_ROWS, d), lambda l, j: (0, 0)),
            pl.BlockSpec((1, d, tn), lambda l, j: (l, 0, j)),
            pl.BlockSpec((1, 1, tn), lambda l, j: (l, 0, j)),
        ],
        out_specs=pl.BlockSpec((1, MOD_ROWS, tn), lambda l, j: (l, 0, j)),
        out_shape=jax.ShapeDtypeStruct((depth, MOD_ROWS, d3), F32),
        compiler_params=_params(2),
        name="modulation",
    )(c_all, w_mod, b_mod.reshape(depth, 1, d3))


_SEC_Q = ("q", 0, ATTN_W, "qscale", ATTN_W, BF16, True)
_SEC_K = ("k", ATTN_W, 2 * ATTN_W, None, ATTN_W, BF16, True)
_SEC_V = ("v", 2 * ATTN_W, 3 * ATTN_W, None, ATTN_W, BF16, True)
_O = 4 * ATTN_W
_SEC_REST = (
    ("ga", 3 * ATTN_W, _O, "silu", ATTN_W, BF16, False),
    ("uf", _O, _O + FOURIER_W, None, FOURIER_W, F32, False),
    ("gf", _O + FOURIER_W, _O + 2 * FOURIER_W, "silu", FOURIER_W, BF16, False),
    ("u", _O + 2 * FOURIER_W, _O + 2 * FOURIER_W + 2 * CONV_W, "glu", CONV_W, BF16, False),
    ("gc", _O + 2 * FOURIER_W + 2 * CONV_W, _O + 2 * FOURIER_W + 3 * CONV_W, "silu", CONV_W, BF16, False),
)
_S0 = _O + 2 * FOURIER_W + 3 * CONV_W
ALL_SECTIONS = (_SEC_Q, _SEC_K, _SEC_V) + _SEC_REST
KV_SECTIONS = (_SEC_K, _SEC_V)


def _merge_sections(d):
    return tuple((n, _S0 + i * d, _S0 + (i + 1) * d, "sigmoid", d, BF16, False)
                 for i, n in enumerate(("sa", "sf", "sc")))


INPROJ_MIN_ROWS = 128
INPROJ_MAX_GROUPS = 4


def _inproj_kernel(x_ref, mod_ref, g_ref, w_ref, *o_refs, sections):
    tm, d = x_ref.shape[1], x_ref.shape[2]
    m = mod_ref[0]
    shift, scale = m[:, :d], m[:, d:2 * d]
    epilogue_rank = {"sigmoid": 0, "glu": 1, "silu": 2, "qscale": 3, None: 4}
    order = sorted(range(len(sections)), key=lambda j: epilogue_rank[sections[j][3]])
    groups = max(1, min(INPROJ_MAX_GROUPS, tm // INPROJ_MIN_ROWS))
    sub = tm // groups
    for h in range(groups):
        r = slice(h * sub, (h + 1) * sub)
        x = x_ref[0, r, :]
        y = x * lax.rsqrt(jnp.mean(x * x, axis=-1, keepdims=True) + NORM_EPS) * g_ref[...]
        hb = (y * (1.0 + scale) + shift).astype(BF16)
        for (_, lo, hi, act, _, dtype, paired), o_ref in [(sections[j], o_refs[j]) for j in order]:
            acc = jnp.dot(hb, w_ref[:, lo:hi], preferred_element_type=F32)
            if act == "qscale":
                acc = acc * (LOG2E * HEAD_DIM ** -0.5)
            elif act == "silu":
                acc = _silu(acc)
            elif act == "sigmoid":
                acc = _sigmoid(acc)
            elif act == "glu":
                half = (hi - lo) // 2
                acc = acc[:, :half] * _sigmoid(acc[:, half:])
            if paired:
                for p in range(N_PAIRS):
                    o_ref[0, p, r, :] = acc[:, p * LANES:(p + 1) * LANES].astype(dtype)
            else:
                o_ref[0, r, :] = acc.astype(dtype)


def _inproj(x, mod_l, g_pre_l, w_in_bf, layer, sections, mod_row, tm):
    b, s, d = x.shape
    tm = min(tm, s)
    assert s % tm == 0 and tm % SUBLANES == 0
    in_w = w_in_bf.shape[-1]
    out_shapes, out_specs = [], []
    for (_, _, _, _, width, dtype, paired) in sections:
        if paired:
            out_shapes.append(jax.ShapeDtypeStruct((b, N_PAIRS, s, LANES), dtype))
            out_specs.append(pl.BlockSpec((1, N_PAIRS, tm, LANES), lambda bi, i: (bi, 0, i, 0)))
        else:
            out_shapes.append(jax.ShapeDtypeStruct((b, s, width), dtype))
            out_specs.append(pl.BlockSpec((1, tm, width), lambda bi, i: (bi, i, 0)))
    return pl.pallas_call(
        functools.partial(_inproj_kernel, sections=sections),
        grid=(b, s // tm),
        in_specs=[
            pl.BlockSpec((1, tm, d), lambda bi, i: (bi, i, 0)),
            pl.BlockSpec((1, 1, 3 * d), lambda bi, i: (mod_row(bi), 0, 0)),
            pl.BlockSpec((1, d), lambda bi, i: (0, 0)),
            pl.BlockSpec((None, d, in_w), lambda bi, i: (layer, 0, 0)),
        ],
        out_specs=out_specs,
        out_shape=out_shapes,
        compiler_params=_params(2),
        name="inproj",
    )(x, mod_l.reshape(MOD_ROWS, 1, 3 * d), g_pre_l.reshape(1, d), w_in_bf)


PAIR_STARTS = 2 * WIN_ROWS


def _bias_pair_tables(rpb):
    cols = np.arange(GRID_W)
    col_start = np.clip(cols - WIN_COLS // 2, 0, GRID_W - WIN_COLS)
    valid_c = (cols[None, :] >= col_start[:, None]) & (cols[None, :] < col_start[:, None] + WIN_COLS)
    dc = cols[None, :] - cols[:, None] + (WIN_COLS - 1)
    one_hot = (dc[None] == np.arange(2 * WIN_COLS - 1)[:, None, None]) & valid_c[None]
    per_dr = jnp.einsum("lhdp,pck->lhdck", rpb, jnp.asarray(one_hot, F32), precision=lax.Precision.HIGHEST)
    per_dr = jnp.where(valid_c, per_dr * LOG2E, MASKED)
    masked = jnp.full_like(per_dr[:, :, :1], MASKED)
    ext = jnp.concatenate([masked, per_dr, masked], axis=2)
    return jnp.concatenate([ext[:, :, :-1], ext[:, :, 1:]], axis=-1)


def _window_plan(i, ws, rows):
    kr = min(WIN_ROWS, rows)
    low = _head_masks()
    plan = []
    for ri in range(Q_ROWS):
        r = Q_ROWS * i + ri
        rs = jnp.clip(r - kr // 2, 0, rows - kr)
        for jj in range(K_ROWS // 2):
            krow = ws + 2 * jj
            ok_lo = (krow >= rs) & (krow < rs + kr)
            ok_hi = (krow + 1 >= rs) & (krow + 1 < rs + kr)
            ok = jnp.where(low, ok_lo.astype(jnp.int32), ok_hi.astype(jnp.int32)) > 0
            start = jnp.clip(krow - r + WIN_ROWS, 0, PAIR_STARTS - 1)
            plan.append((ok, start))
    return plan


def _window_bias(tab_ref, head, plan):
    n_jj = K_ROWS // 2
    row_blocks = []
    for ri in range(Q_ROWS):
        lane_blocks = [jnp.where(ok, tab_ref[head, start], MASKED) for ok, start in plan[ri * n_jj:(ri + 1) * n_jj]]
        row_blocks.append(jnp.concatenate(lane_blocks, axis=-1))
    return jnp.concatenate(row_blocks, axis=0)


def _head_masks():
    lane = lax.broadcasted_iota(jnp.int32, (1, LANES), 1)
    return lane < HEAD_DIM


def _pick(head, own, other):
    low = _head_masks()
    return jnp.where(low, own, other) if head == 0 else jnp.where(low, other, own)


def _pair_scores(q, keys, biases):
    dn = (((1,), (1,)), ((), ()))
    m_rows = q.shape[0]
    q2 = jnp.concatenate([_pick(head, q, jnp.zeros_like(q)) for head in (0, 1)], axis=0)
    both = [lax.dot_general(q2, k, dn, preferred_element_type=F32) for k in keys]
    out = []
    for head in (0, 1):
        scores = [s[head * m_rows:(head + 1) * m_rows] for s in both]
        scores = [s if bias is None else s + bias for s, bias in zip(scores, biases[head])]
        m = functools.reduce(jnp.maximum, [jnp.max(s, axis=-1, keepdims=True) for s in scores])
        out.append((scores, m))
    return out


def _pv(head, scores, m, vals):
    o = functools.reduce(jnp.add, [
        jnp.dot(jnp.exp2(s - m).astype(BF16), _pick(head, v, jnp.ones_like(v)), preferred_element_type=F32)
        for s, v in zip(scores, vals)])
    return o / pltpu.roll(o, HEAD_DIM, axis=1)


ATTN_BLOCKS_PER_STEP = 4
ATTN_PAIRS_PER_STEP = 4


def _attn_kernel(q_ref, k_ref, v_ref, kc_ref, vc_ref, tab_ref, o_ref, *, rows, pairs):
    tq = Q_ROWS * GRID_W
    staged, starts = [], []
    for blk in range(ATTN_BLOCKS_PER_STEP):
        i = pl.program_id(2) * ATTN_BLOCKS_PER_STEP + blk
        ws = jnp.clip(Q_ROWS * i - min(WIN_ROWS, rows) // 2, 0, rows - K_ROWS)
        start = pl.multiple_of(ws * GRID_W, GRID_W)
        starts.append(start)
        plan = _window_plan(i, ws, rows)
        for p in range(pairs):
            kw = k_ref[0, p, pl.ds(start, K_ROWS * GRID_W), :]
            biases = [(_window_bias(tab_ref, 2 * p + head, plan), None) for head in (0, 1)]
            staged.extend(_pair_scores(q_ref[0, p, blk * tq:(blk + 1) * tq, :], (kw, kc_ref[0, p]), biases))
    outs = []
    for j, (scores, m) in enumerate(staged):
        blk, jj = divmod(j, 2 * pairs)
        p, head = divmod(jj, 2)
        vw = v_ref[0, p, pl.ds(starts[blk], K_ROWS * GRID_W), :]
        outs.append(_pv(head, scores, m, (vw, vc_ref[0, p])))
    for blk in range(ATTN_BLOCKS_PER_STEP):
        for p in range(pairs):
            j = blk * 2 * pairs + 2 * p
            o_ref[0, p, blk * tq:(blk + 1) * tq, :] = _pick(0, outs[j], outs[j + 1]).astype(o_ref.dtype)


def _attention(q, k, v, kc, vc, tab):
    b, _, s, _ = q.shape
    l = kc.shape[2]
    rows = s // GRID_W
    n_blk = rows // Q_ROWS
    tq = Q_ROWS * GRID_W * ATTN_BLOCKS_PER_STEP
    pp = ATTN_PAIRS_PER_STEP
    assert s == rows * GRID_W and n_blk % ATTN_BLOCKS_PER_STEP == 0 and rows >= K_ROWS and N_PAIRS % pp == 0

    return pl.pallas_call(
        functools.partial(_attn_kernel, rows=rows, pairs=pp),
        grid=(b, N_PAIRS // pp, n_blk // ATTN_BLOCKS_PER_STEP),
        in_specs=[
            pl.BlockSpec((1, pp, tq, LANES), lambda bi, p, i: (bi, p, i, 0)),
            pl.BlockSpec((1, pp, s, LANES), lambda bi, p, i: (bi, p, 0, 0)),
            pl.BlockSpec((1, pp, s, LANES), lambda bi, p, i: (bi, p, 0, 0)),
            pl.BlockSpec((1, pp, l, LANES), lambda bi, p, i: (bi, p, 0, 0)),
            pl.BlockSpec((1, pp, l, LANES), lambda bi, p, i: (bi, p, 0, 0)),
            pl.BlockSpec((2 * pp, PAIR_STARTS, GRID_W, LANES), lambda bi, p, i: (p, 0, 0, 0)),
        ],
        out_specs=pl.BlockSpec((1, pp, tq, LANES), lambda bi, p, i: (bi, p, i, 0)),
        out_shape=jax.ShapeDtypeStruct((b, N_PAIRS, s, LANES), BF16),
        compiler_params=_params(3),
        name="attention",
    )(q, k, v, kc, vc, tab)


def _ctx_attn_kernel(q_ref, k_ref, v_ref, o_ref):
    staged = []
    for p in range(N_PAIRS):
        staged.extend(_pair_scores(q_ref[0, p], (k_ref[0, p],), [(None,), (None,)]))
    outs = [_pv(j % 2, scores, m, (v_ref[0, j // 2],)) for j, (scores, m) in enumerate(staged)]
    for p in range(N_PAIRS):
        o_ref[0, p] = _pick(0, outs[2 * p], outs[2 * p + 1]).astype(o_ref.dtype)


def _ctx_attention(q, k, v):
    b, _, l, _ = q.shape
    spec = pl.BlockSpec((1, N_PAIRS, l, LANES), lambda bi: (bi, 0, 0, 0))
    return pl.pallas_call(
        _ctx_attn_kernel,
        grid=(b,),
        in_specs=[spec, spec, spec],
        out_specs=spec,
        out_shape=jax.ShapeDtypeStruct((b, N_PAIRS, l, LANES), BF16),
        compiler_params=_params(1),
        name="ctx_attention",
    )(q, k, v)


def _dft_cos_sin(k, n, period):
    ang = (2.0 * np.pi / period) * ((k[:, None] * n[None, :]) % period)
    return np.cos(ang), np.sin(ang)


def _table(a):
    return jnp.asarray(a, F32).astype(BF16)


def _channel_dft():
    m = np.arange(FOURIER_GROUP_DIM)
    c, s = _dft_cos_sin(m, m, FOURIER_GROUP_DIM)
    eye = np.eye(FOURIER_GROUPS)
    scale = FOURIER_GROUP_DIM ** -0.5
    return np.concatenate([np.kron(eye, c.T), np.kron(eye, s.T)], axis=0) * scale


def _channel_weights(cs_ref, wf_ref):
    return jnp.dot(cs_ref[...], wf_ref[...], preferred_element_type=F32).astype(BF16)


def _channel_mix(xr, xi, cw):
    x = jnp.concatenate([xr, xi], axis=-1).astype(BF16)
    return jnp.dot(x, cw, preferred_element_type=F32)


FFT_GROUP = 8


def _fft_kernel(x_ref, w1_ref, m2_ref, cs_ref, wf_ref, pm_ref, o_ref, ar_ref, ai_ref, *, n1):
    g8 = FFT_GROUP
    for g in range(GRID_W // g8):
        xg = x_ref[0, :, g * g8:(g + 1) * g8, :].reshape(n1 * g8, FOURIER_W).astype(BF16)
        a = jnp.dot(w1_ref[...], xg, preferred_element_type=F32)
        ar_ref[:, g * g8:(g + 1) * g8, :] = a[:n1 * g8].reshape(n1, g8, FOURIER_W)
        ai_ref[:, g * g8:(g + 1) * g8, :] = a[n1 * g8:].reshape(n1, g8, FOURIER_W)
    for k1 in range(n1):
        a = jnp.concatenate([ar_ref[k1], ai_ref[k1]], axis=0).astype(BF16)
        x = jnp.dot(m2_ref[k1], a, preferred_element_type=F32)
        ar_ref[k1] = x[:GRID_W]
        ai_ref[k1] = x[GRID_W:]
    cw = _channel_weights(cs_ref, wf_ref)
    for g in range(n1 // g8):
        xr = ar_ref[g * g8:(g + 1) * g8].reshape(g8 * GRID_W, FOURIER_W)
        xi = ai_ref[g * g8:(g + 1) * g8].reshape(g8 * GRID_W, FOURIER_W)
        z = _channel_mix(xr, xi, cw).astype(BF16)
        o = jnp.dot(pm_ref[...], z, preferred_element_type=F32)
        o_ref[0, :, g * g8:(g + 1) * g8, :] = o.reshape(GRID_W, g8, FOURIER_W)


def _fourier(uf, wf_bf):
    b, s, fw = uf.shape
    n1 = s // GRID_W
    g8 = FFT_GROUP
    assert s == n1 * GRID_W and n1 % g8 == 0 and fw == FOURIER_W
    k1 = np.arange(n1)
    c1, s1 = _dft_cos_sin(k1, k1, n1)
    w1 = np.kron(np.concatenate([c1, -s1], axis=0) * n1 ** -0.5, np.eye(g8))
    n2 = np.arange(GRID_W)
    kk = k1[:, None] + n1 * np.arange(GRID_W)[None, :]
    ang = (2.0 * np.pi / s) * ((kk[:, :, None] * n2[None, None, :]) % s)
    c2, s2 = np.cos(ang), np.sin(ang)
    m2 = np.concatenate([np.concatenate([c2, s2], axis=2),
                         np.concatenate([-s2, c2], axis=2)], axis=1) * GRID_W ** -0.5
    perm = np.zeros((GRID_W * g8, g8 * GRID_W))
    k2g, k1g = np.meshgrid(np.arange(GRID_W), np.arange(g8), indexing="ij")
    perm[(k2g * g8 + k1g).ravel(), (k1g * GRID_W + k2g).ravel()] = 1.0
    out = pl.pallas_call(
        functools.partial(_fft_kernel, n1=n1),
        grid=(b,),
        in_specs=[
            pl.BlockSpec((1, n1, GRID_W, fw), lambda bi: (bi, 0, 0, 0)),
            pl.BlockSpec(w1.shape, lambda bi: (0, 0)),
            pl.BlockSpec((n1, 2 * GRID_W, 2 * GRID_W), lambda bi: (0, 0, 0)),
            pl.BlockSpec((2 * fw, fw), lambda bi: (0, 0)),
            pl.BlockSpec((fw, fw), lambda bi: (0, 0)),
            pl.BlockSpec(perm.shape, lambda bi: (0, 0)),
        ],
        out_specs=pl.BlockSpec((1, GRID_W, n1, fw), lambda bi: (bi, 0, 0, 0)),
        out_shape=jax.ShapeDtypeStruct((b, GRID_W, n1, fw), F32),
        scratch_shapes=[pltpu.VMEM((n1, GRID_W, fw), F32), pltpu.VMEM((n1, GRID_W, fw), F32)],
        compiler_params=_params(1),
        name="fourier",
    )(uf.reshape(b, n1, GRID_W, fw), _table(w1), _table(m2), _table(_channel_dft()), wf_bf, _table(perm))
    return out.reshape(b, s, fw)


def _ctx_fft_kernel(x_ref, wd_ref, cs_ref, wf_ref, o_ref):
    n = x_ref.shape[1]
    x = jnp.dot(wd_ref[...], x_ref[0].astype(BF16), preferred_element_type=F32)
    o_ref[0] = _channel_mix(x[:n], x[n:], _channel_weights(cs_ref, wf_ref))


def _ctx_fourier(uf, wf_bf):
    b, n, fw = uf.shape
    k = np.arange(n)
    c, s = _dft_cos_sin(k, k, n)
    wd = np.concatenate([c, -s], axis=0) * n ** -0.5
    return pl.pallas_call(
        _ctx_fft_kernel,
        grid=(b,),
        in_specs=[
            pl.BlockSpec((1, n, fw), lambda bi: (bi, 0, 0)),
            pl.BlockSpec((2 * n, n), lambda bi: (0, 0)),
            pl.BlockSpec((2 * fw, fw), lambda bi: (0, 0)),
            pl.BlockSpec((fw, fw), lambda bi: (0, 0)),
        ],
        out_specs=pl.BlockSpec((1, n, fw), lambda bi: (bi, 0, 0)),
        out_shape=jax.ShapeDtypeStruct((b, n, fw), F32),
        compiler_params=_params(1),
        name="ctx_fourier",
    )(uf, _table(wd), _table(_channel_dft()), wf_bf)


CONV_PAD = 16
CONV_TILE = 64
TAIL_CHUNK = 256
SUBLANES = 8


def _conv_tile(up_ref, base, dw_ref):
    first = CONV_PAD - CONV_K // 2
    halo = CONV_TILE + 2 * CONV_PAD
    win = up_ref[pl.ds(base, halo), :]
    acc = jnp.zeros((CONV_TILE, CONV_W), F32)
    for ph in range(SUBLANES):
        taps = [t for t in range(CONV_K) if (first + t) % SUBLANES == ph]
        if not taps:
            continue
        shifted = win if ph == 0 else pltpu.roll(win, halo - ph, axis=0)
        for t in taps:
            off = first + t - ph
            acc = acc + shifted[off:off + CONV_TILE] * dw_ref[t:t + 1, :]
    return acc


def _tail_kernel(x_ref, ao_ref, ga_ref, fw_ref, gf_ref, u_ref, gc_ref, sa_ref, sf_ref, sc_ref, mod_ref, gp_ref,
                 pa_ref, pf_ref, pc_ref, wo_ref, dw_ref, db_ref, lg_ref, lb_ref, pw_ref, o_ref, up_ref):
    d = x_ref.shape[-1]
    tm = x_ref.shape[1]
    n = u_ref.shape[1]
    i = pl.program_id(1)

    @pl.when(i == 0)
    def _():
        zeros = jnp.zeros((CONV_PAD, CONV_W), F32)
        up_ref[0:CONV_PAD, :] = zeros
        up_ref[CONV_PAD + n:CONV_PAD + n + CONV_PAD, :] = zeros
        up_ref[CONV_PAD:CONV_PAD + n, :] = u_ref[0].astype(F32)

    ao = jnp.concatenate([ao_ref[0, p] for p in range(N_PAIRS)], axis=-1)
    a_in = ao * ga_ref[0]
    f_in = (fw_ref[0] * gf_ref[0].astype(F32)).astype(BF16)
    n_pieces = tm // CONV_TILE
    n_chunks = d // TAIL_CHUNK
    per_dot = -(-n_pieces // (2 * n_chunks))
    conv_pieces, ya, yf = [], [], []

    def some_conv():
        for j in range(len(conv_pieces), min(len(conv_pieces) + per_dot, n_pieces)):
            conv_pieces.append(_conv_tile(up_ref, pl.multiple_of(i * tm + j * CONV_TILE, CONV_TILE), dw_ref))

    for c in range(n_chunks):
        cols = slice(c * TAIL_CHUNK, (c + 1) * TAIL_CHUNK)
        ya.append(jnp.dot(a_in, pa_ref[:, cols], preferred_element_type=F32))
        some_conv()
        yf.append(jnp.dot(f_in, pf_ref[:, cols], preferred_element_type=F32))
        some_conv()
    ya, yf = jnp.concatenate(ya, axis=-1), jnp.concatenate(yf, axis=-1)
    conv = jnp.concatenate(conv_pieces, axis=0) + db_ref[...]
    xc = conv - jnp.mean(conv, axis=-1, keepdims=True)
    var = jnp.mean(xc * xc, axis=-1, keepdims=True)
    cn = xc * lax.rsqrt(var + NORM_EPS) * lg_ref[...] + lb_ref[...]
    cw = jnp.dot(_silu(cn).astype(BF16), pw_ref[...], preferred_element_type=F32) * gc_ref[0].astype(F32)
    yc = jnp.dot(cw.astype(BF16), pc_ref[...], preferred_element_type=F32)
    merged = (sa_ref[0].astype(F32) * ya + sf_ref[0].astype(F32) * yf + sc_ref[0].astype(F32) * yc)
    y = jnp.dot(merged.astype(BF16), wo_ref[...], preferred_element_type=F32)
    yn = y * lax.rsqrt(jnp.mean(y * y, axis=-1, keepdims=True) + NORM_EPS) * gp_ref[...]
    gate = mod_ref[0][:, 2 * d:]
    o_ref[0] = x_ref[0] + gate * yn


def _tail(x, ao, ga, fw, gf, u, gc, sa, sf, sc, mod_l, g_post_l, pa_bf, pf_bf, pc_bf, wo_bf,
          conv_dw_l, conv_db_l, ln_g_l, ln_b_l, w_pw_bf, mod_row, tm):
    b, s, d = x.shape
    tm = min(tm, s)
    assert s % tm == 0 and tm % CONV_TILE == 0 and d % TAIL_CHUNK == 0

    def tok(width):
        return pl.BlockSpec((1, tm, width), lambda bi, i: (bi, i, 0))

    def full(shape):
        return pl.BlockSpec(shape, lambda bi, i: (0,) * len(shape))

    return pl.pallas_call(
        _tail_kernel,
        grid=(b, s // tm),
        in_specs=[
            tok(d),
            pl.BlockSpec((1, N_PAIRS, tm, LANES), lambda bi, i: (bi, 0, i, 0)),
            tok(ATTN_W), tok(FOURIER_W), tok(FOURIER_W),
            pl.BlockSpec((1, s, CONV_W), lambda bi, i: (bi, 0, 0)),
            tok(CONV_W), tok(d), tok(d), tok(d),
            pl.BlockSpec((1, 1, 3 * d), lambda bi, i: (mod_row(bi), 0, 0)),
            full((1, d)),
            full(pa_bf.shape), full(pf_bf.shape), full(pc_bf.shape), full(wo_bf.shape),
            full((CONV_K, CONV_W)), full((1, CONV_W)), full((1, CONV_W)), full((1, CONV_W)), full(w_pw_bf.shape),
        ],
        out_specs=tok(d),
        out_shape=jax.ShapeDtypeStruct((b, s, d), F32),
        scratch_shapes=[pltpu.VMEM((s + 2 * CONV_PAD, CONV_W), F32)],
        compiler_params=_params(2),
        name="tail",
    )(x, ao, ga, fw, gf, u, gc, sa, sf, sc, mod_l.reshape(MOD_ROWS, 1, 3 * d), g_post_l.reshape(1, d),
      pa_bf, pf_bf, pc_bf, wo_bf, conv_dw_l, conv_db_l.reshape(1, CONV_W), ln_g_l.reshape(1, CONV_W),
      ln_b_l.reshape(1, CONV_W), w_pw_bf)


INPROJ_TILE = 512
TAIL_TILE = 512


def kernel(x, c, ctx, c_ctx, w_mod, b_mod, g_pre, g_post, w_in, rpb, w_four, conv_dw, conv_db,
           conv_ln_g, conv_ln_b, w_pw, p_attn, p_four, p_conv, w_out):
    batch, seq, d = x.shape
    depth = w_mod.shape[0]
    rows = seq // GRID_W
    assert batch < MOD_ROWS and rows % (Q_ROWS * ATTN_BLOCKS_PER_STEP) == 0 and rows >= K_ROWS
    ctx_row = batch
    c_all = jnp.zeros((MOD_ROWS, d), F32).at[:batch].set(c).at[ctx_row].set(c_ctx)
    mod = _modulation(c_all, w_mod, b_mod)
    merge_sections = _merge_sections(d)
    bias_tabs = _bias_pair_tables(rpb)

    def latent_row(bi):
        return bi

    def context_row(bi):
        return ctx_row

    w_in_bf = w_in.astype(BF16)
    for l in range(depth):
        wf_bf, pw_bf = w_four[l].astype(BF16), w_pw[l].astype(BF16)
        pa_bf, pf_bf, pc_bf, wo_bf = (p_attn[l].astype(BF16), p_four[l].astype(BF16),
                                      p_conv[l].astype(BF16), w_out[l].astype(BF16))
        conv_w = (conv_dw[l], conv_db[l], conv_ln_g[l], conv_ln_b[l], pw_bf)
        tail_w = (mod[l], g_post[l], pa_bf, pf_bf, pc_bf, wo_bf)
        update_ctx = l < depth - 1

        if update_ctx:
            ctx_sections, ctx_w, ctx_layer = ALL_SECTIONS + merge_sections, w_in_bf, l
        else:
            lo, hi = KV_SECTIONS[0][1], KV_SECTIONS[-1][2]
            ctx_sections = tuple((sec[0], sec[1] - lo, sec[2] - lo) + sec[3:] for sec in KV_SECTIONS)
            ctx_w, ctx_layer = w_in_bf[l:l + 1, :, lo:hi], 0
        ctx_out = dict(zip([sec[0] for sec in ctx_sections],
                           _inproj(ctx, mod[l], g_pre[l], ctx_w, ctx_layer, ctx_sections, context_row,
                                   INPROJ_TILE)))

        lat = dict(zip([sec[0] for sec in ALL_SECTIONS + merge_sections],
                       _inproj(x, mod[l], g_pre[l], w_in_bf, l, ALL_SECTIONS + merge_sections, latent_row,
                               INPROJ_TILE)))
        ao = _attention(lat["q"], lat["k"], lat["v"], ctx_out["k"], ctx_out["v"], bias_tabs[l])
        fw = _fourier(lat["uf"], wf_bf)
        x = _tail(x, ao, lat["ga"], fw, lat["gf"], lat["u"], lat["gc"], lat["sa"], lat["sf"], lat["sc"],
                  *tail_w, *conv_w, latent_row, TAIL_TILE)

        if update_ctx:
            co = ctx_out
            ao_c = _ctx_attention(co["q"], co["k"], co["v"])
            fw_c = _ctx_fourier(co["uf"], wf_bf)
            ctx = _tail(ctx, ao_c, co["ga"], fw_c, co["gf"], co["u"], co["gc"], co["sa"], co["sf"], co["sc"],
                        *tail_w, *conv_w, context_row, TAIL_TILE)
    return x
```

```python
import functools

import jax
import jax.numpy as jnp
import numpy as np
from jax import lax
from jax.experimental import pallas as pl
from jax.experimental.pallas import tpu as pltpu

F32 = jnp.float32
BF16 = jnp.bfloat16

GRID_W = 64
N_HEADS = 8
HEAD_DIM = 64
ATTN_W = N_HEADS * HEAD_DIM
LANES = 128
N_PAIRS = ATTN_W // LANES
WIN_ROWS = 8
WIN_COLS = 16
FOURIER_GROUPS = 4
FOURIER_GROUP_DIM = 64
FOURIER_W = FOURIER_GROUPS * FOURIER_GROUP_DIM
CONV_W = 256
CONV_K = 31
NORM_EPS = 1e-6
MASKED = -1e30
LOG2E = 1.4426950408889634

Q_ROWS = 2
K_ROWS = 10
MOD_ROWS = 8

VMEM_LIMIT = 56 * 1024 * 1024


def _sigmoid(x):
    return 1.0 / (1.0 + jnp.exp(-x))


def _silu(x):
    return x * _sigmoid(x)


def _params(n_axes):
    return pltpu.CompilerParams(
        dimension_semantics=("arbitrary",) * n_axes, vmem_limit_bytes=VMEM_LIMIT)


def _mod_kernel(c_ref, w_ref, b_ref, o_ref):
    sc = _silu(c_ref[...]).astype(BF16)
    o_ref[0] = jnp.dot(sc, w_ref[0].astype(BF16), preferred_element_type=F32) + b_ref[0]


def _modulation(c_all, w_mod, b_mod):
    depth, d, d3 = w_mod.shape
    tn = d
    return pl.pallas_call(
        _mod_kernel,
        grid=(depth, d3 // tn),
        in_specs=[
            pl.BlockSpec((MOD_ROWS, d), lambda l, j: (0, 0)),
            pl.BlockSpec((1, d, tn), lambda l, j: (l, 0, j)),
            pl.BlockSpec((1, 1, tn), lambda l, j: (l, 0, j)),
        ],
        out_specs=pl.BlockSpec((1, MOD_ROWS, tn), lambda l, j: (l, 0, j)),
        out_shape=jax.ShapeDtypeStruct((depth, MOD_ROWS, d3), F32),
        compiler_params=_params(2),
        name="modulation",
    )(c_all, w_mod, b_mod.reshape(depth, 1, d3))


KT_CHUNK = LANES
_SEC_Q = ("q", 0, ATTN_W, "qscale", ATTN_W, BF16, "pairs")
_SEC_K = ("k", ATTN_W, 2 * ATTN_W, None, ATTN_W, BF16, "pairs_t")
_SEC_V = ("v", 2 * ATTN_W, 3 * ATTN_W, None, ATTN_W, BF16, "pairs")
_O = 4 * ATTN_W
_SEC_REST = (
    ("ga", 3 * ATTN_W, _O, "silu", ATTN_W, BF16, "flat"),
    ("uf", _O, _O + FOURIER_W, None, FOURIER_W, F32, "flat"),
    ("gf", _O + FOURIER_W, _O + 2 * FOURIER_W, "silu", FOURIER_W, BF16, "flat"),
    ("u", _O + 2 * FOURIER_W, _O + 2 * FOURIER_W + 2 * CONV_W, "glu", CONV_W, BF16, "flat"),
    ("gc", _O + 2 * FOURIER_W + 2 * CONV_W, _O + 2 * FOURIER_W + 3 * CONV_W, "silu", CONV_W, BF16, "flat"),
)
_S0 = _O + 2 * FOURIER_W + 3 * CONV_W
ALL_SECTIONS = (_SEC_Q, _SEC_K, _SEC_V) + _SEC_REST
KV_SECTIONS = (_SEC_K, _SEC_V)


def _merge_sections(d):
    return tuple((n, _S0 + i * d, _S0 + (i + 1) * d, "sigmoid", d, BF16, "flat")
                 for i, n in enumerate(("sa", "sf", "sc")))


INPROJ_MIN_ROWS = 128
INPROJ_MAX_GROUPS = 4


def _inproj_kernel(x_ref, mod_ref, g_ref, w_ref, *o_refs, sections):
    tm, d = x_ref.shape[1], x_ref.shape[2]
    m = mod_ref[0]
    shift, scale = m[:, :d], m[:, d:2 * d]
    epilogue_rank = {"sigmoid": 0, "glu": 1, "silu": 2, "qscale": 3, None: 4}
    order = sorted(range(len(sections)), key=lambda j: epilogue_rank[sections[j][3]])
    groups = max(1, min(INPROJ_MAX_GROUPS, tm // INPROJ_MIN_ROWS))
    sub = tm // groups
    for h in range(groups):
        r = slice(h * sub, (h + 1) * sub)
        x = x_ref[0, r, :]
        y = x * lax.rsqrt(jnp.mean(x * x, axis=-1, keepdims=True) + NORM_EPS) * g_ref[...]
        hb = (y * (1.0 + scale) + shift).astype(BF16)
        for (_, lo, hi, act, _, dtype, layout), o_ref in [(sections[j], o_refs[j]) for j in order]:
            acc = jnp.dot(hb, w_ref[:, lo:hi], preferred_element_type=F32)
            if act == "qscale":
                acc = acc * (LOG2E * HEAD_DIM ** -0.5)
            elif act == "silu":
                acc = _silu(acc)
            elif act == "sigmoid":
                acc = _sigmoid(acc)
            elif act == "glu":
                half = (hi - lo) // 2
                acc = acc[:, :half] * _sigmoid(acc[:, half:])
            if layout == "pairs":
                for p in range(N_PAIRS):
                    o_ref[0, p, r, :] = acc[:, p * LANES:(p + 1) * LANES].astype(dtype)
            elif layout == "pairs_t":
                for p in range(N_PAIRS):
                    for c in range(sub // KT_CHUNK):
                        piece = acc[c * KT_CHUNK:(c + 1) * KT_CHUNK, p * LANES:(p + 1) * LANES]
                        o_ref[0, p, h * (sub // KT_CHUNK) + c] = piece.T.astype(dtype)
            else:
                o_ref[0, r, :] = acc.astype(dtype)


def _inproj(x, mod_l, g_pre_l, w_in_bf, layer, sections, mod_row, tm):
    b, s, d = x.shape
    tm = min(tm, s)
    assert s % tm == 0 and tm % SUBLANES == 0
    in_w = w_in_bf.shape[-1]
    out_shapes, out_specs = [], []
    for (_, _, _, _, width, dtype, layout) in sections:
        if layout == "pairs":
            out_shapes.append(jax.ShapeDtypeStruct((b, N_PAIRS, s, LANES), dtype))
            out_specs.append(pl.BlockSpec((1, N_PAIRS, tm, LANES), lambda bi, i: (bi, 0, i, 0)))
        elif layout == "pairs_t":
            assert tm % KT_CHUNK == 0 and (tm // max(1, min(INPROJ_MAX_GROUPS, tm // INPROJ_MIN_ROWS))) % KT_CHUNK == 0
            out_shapes.append(jax.ShapeDtypeStruct((b, N_PAIRS, s // KT_CHUNK, LANES, KT_CHUNK), dtype))
            out_specs.append(pl.BlockSpec((1, N_PAIRS, tm // KT_CHUNK, LANES, KT_CHUNK),
                                          lambda bi, i: (bi, 0, i, 0, 0)))
        else:
            out_shapes.append(jax.ShapeDtypeStruct((b, s, width), dtype))
            out_specs.append(pl.BlockSpec((1, tm, width), lambda bi, i: (bi, i, 0)))
    return pl.pallas_call(
        functools.partial(_inproj_kernel, sections=sections),
        grid=(b, s // tm),
        in_specs=[
            pl.BlockSpec((1, tm, d), lambda bi, i: (bi, i, 0)),
            pl.BlockSpec((1, 1, 3 * d), lambda bi, i: (mod_row(bi), 0, 0)),
            pl.BlockSpec((1, d), lambda bi, i: (0, 0)),
            pl.BlockSpec((None, d, in_w), lambda bi, i: (layer, 0, 0)),
        ],
        out_specs=out_specs,
        out_shape=out_shapes,
        compiler_params=_params(2),
        name="inproj",
    )(x, mod_l.reshape(MOD_ROWS, 1, 3 * d), g_pre_l.reshape(1, d), w_in_bf)


PAIR_STARTS = 2 * WIN_ROWS


def _bias_pair_tables(rpb):
    cols = np.arange(GRID_W)
    col_start = np.clip(cols - WIN_COLS // 2, 0, GRID_W - WIN_COLS)
    valid_c = (cols[None, :] >= col_start[:, None]) & (cols[None, :] < col_start[:, None] + WIN_COLS)
    dc = cols[None, :] - cols[:, None] + (WIN_COLS - 1)
    one_hot = (dc[None] == np.arange(2 * WIN_COLS - 1)[:, None, None]) & valid_c[None]
    per_dr = jnp.einsum("lhdp,pck->lhdck", rpb, jnp.asarray(one_hot, F32), precision=lax.Precision.HIGHEST)
    per_dr = jnp.where(valid_c, per_dr * LOG2E, MASKED)
    masked = jnp.full_like(per_dr[:, :, :1], MASKED)
    ext = jnp.concatenate([masked, per_dr, masked], axis=2)
    return jnp.concatenate([ext[:, :, :-1], ext[:, :, 1:]], axis=-1)


def _window_plan(i, ws, rows):
    kr = min(WIN_ROWS, rows)
    low = _head_masks()
    plan = []
    for ri in range(Q_ROWS):
        r = Q_ROWS * i + ri
        rs = jnp.clip(r - kr // 2, 0, rows - kr)
        for jj in range(K_ROWS // 2):
            krow = ws + 2 * jj
            ok_lo = (krow >= rs) & (krow < rs + kr)
            ok_hi = (krow + 1 >= rs) & (krow + 1 < rs + kr)
            ok = jnp.where(low, ok_lo.astype(jnp.int32), ok_hi.astype(jnp.int32)) > 0
            start = jnp.clip(krow - r + WIN_ROWS, 0, PAIR_STARTS - 1)
            plan.append((ok, start))
    return plan


def _window_bias(tab_ref, head, plan):
    n_jj = K_ROWS // 2
    row_blocks = []
    for ri in range(Q_ROWS):
        lane_blocks = [jnp.where(ok, tab_ref[head, start], MASKED) for ok, start in plan[ri * n_jj:(ri + 1) * n_jj]]
        row_blocks.append(jnp.concatenate(lane_blocks, axis=-1))
    return jnp.concatenate(row_blocks, axis=0)


def _head_masks():
    lane = lax.broadcasted_iota(jnp.int32, (1, LANES), 1)
    return lane < HEAD_DIM


def _pick(head, own, other):
    low = _head_masks()
    return jnp.where(low, own, other) if head == 0 else jnp.where(low, other, own)


def _pair_scores(q, keys, biases):
    m_rows = q.shape[0]
    q2 = jnp.concatenate([_pick(head, q, jnp.zeros_like(q)) for head in (0, 1)], axis=0)
    both = [jnp.dot(q2, kt, preferred_element_type=F32) for kt in keys]
    out = []
    for head in (0, 1):
        scores = [s[head * m_rows:(head + 1) * m_rows] for s in both]
        scores = [s if bias is None else s + bias for s, bias in zip(scores, biases[head])]
        m = functools.reduce(jnp.maximum, [jnp.max(s, axis=-1, keepdims=True) for s in scores])
        out.append((scores, m))
    return out


def _pv(head, scores, m, vals):
    o = functools.reduce(jnp.add, [
        jnp.dot(jnp.exp2(s - m).astype(BF16), _pick(head, v, jnp.ones_like(v)), preferred_element_type=F32)
        for s, v in zip(scores, vals)])
    return o / pltpu.roll(o, HEAD_DIM, axis=1)


ATTN_BLOCKS_PER_STEP = 4
ATTN_PAIRS_PER_STEP = 4


def _key_chunks(kt_ref, p, first, count):
    return jnp.concatenate([kt_ref[0, p, first + c] for c in range(count)], axis=-1)


def _attn_kernel(q_ref, kt_ref, v_ref, kct_ref, vc_ref, tab_ref, o_ref, *, rows, pairs):
    tq = Q_ROWS * GRID_W
    win_chunks = K_ROWS * GRID_W // KT_CHUNK
    ctx_chunks = kct_ref.shape[2]
    staged, starts = [], []
    for blk in range(ATTN_BLOCKS_PER_STEP):
        i = pl.program_id(2) * ATTN_BLOCKS_PER_STEP + blk
        ws = jnp.clip(Q_ROWS * i - min(WIN_ROWS, rows) // 2, 0, rows - K_ROWS)
        start = pl.multiple_of(ws * GRID_W, GRID_W)
        starts.append(start)
        plan = _window_plan(i, ws, rows)
        for p in range(pairs):
            kw = _key_chunks(kt_ref, p, ws * GRID_W // KT_CHUNK, win_chunks)
            kc = _key_chunks(kct_ref, p, 0, ctx_chunks)
            biases = [(_window_bias(tab_ref, 2 * p + head, plan), None) for head in (0, 1)]
            staged.extend(_pair_scores(q_ref[0, p, blk * tq:(blk + 1) * tq, :], (kw, kc), biases))
    outs = []
    for j, (scores, m) in enumerate(staged):
        blk, jj = divmod(j, 2 * pairs)
        p, head = divmod(jj, 2)
        vw = v_ref[0, p, pl.ds(starts[blk], K_ROWS * GRID_W), :]
        outs.append(_pv(head, scores, m, (vw, vc_ref[0, p])))
    for blk in range(ATTN_BLOCKS_PER_STEP):
        for p in range(pairs):
            j = blk * 2 * pairs + 2 * p
            o_ref[0, p, blk * tq:(blk + 1) * tq, :] = _pick(0, outs[j], outs[j + 1]).astype(o_ref.dtype)


def _attention(q, kt, v, kct, vc, tab):
    b, _, s, _ = q.shape
    l = vc.shape[2]
    rows = s // GRID_W
    n_blk = rows // Q_ROWS
    tq = Q_ROWS * GRID_W * ATTN_BLOCKS_PER_STEP
    pp = ATTN_PAIRS_PER_STEP
    assert s == rows * GRID_W and n_blk % ATTN_BLOCKS_PER_STEP == 0 and rows >= K_ROWS and N_PAIRS % pp == 0
    chunk_rows = KT_CHUNK // GRID_W
    assert Q_ROWS % chunk_rows == 0 and (min(WIN_ROWS, rows) // 2) % chunk_rows == 0
    assert (rows - K_ROWS) % chunk_rows == 0 and K_ROWS % chunk_rows == 0 and l % KT_CHUNK == 0

    return pl.pallas_call(
        functools.partial(_attn_kernel, rows=rows, pairs=pp),
        grid=(b, N_PAIRS // pp, n_blk // ATTN_BLOCKS_PER_STEP),
        in_specs=[
            pl.BlockSpec((1, pp, tq, LANES), lambda bi, p, i: (bi, p, i, 0)),
            pl.BlockSpec((1, pp, s // KT_CHUNK, LANES, KT_CHUNK), lambda bi, p, i: (bi, p, 0, 0, 0)),
            pl.BlockSpec((1, pp, s, LANES), lambda bi, p, i: (bi, p, 0, 0)),
            pl.BlockSpec((1, pp, l // KT_CHUNK, LANES, KT_CHUNK), lambda bi, p, i: (bi, p, 0, 0, 0)),
            pl.BlockSpec((1, pp, l, LANES), lambda bi, p, i: (bi, p, 0, 0)),
            pl.BlockSpec((2 * pp, PAIR_STARTS, GRID_W, LANES), lambda bi, p, i: (p, 0, 0, 0)),
        ],
        out_specs=pl.BlockSpec((1, pp, tq, LANES), lambda bi, p, i: (bi, p, i, 0)),
        out_shape=jax.ShapeDtypeStruct((b, N_PAIRS, s, LANES), BF16),
        compiler_params=_params(3),
        name="attention",
    )(q, kt, v, kct, vc, tab)


def _ctx_attn_kernel(q_ref, kt_ref, v_ref, o_ref):
    staged = []
    for p in range(N_PAIRS):
        keys = _key_chunks(kt_ref, p, 0, kt_ref.shape[2])
        staged.extend(_pair_scores(q_ref[0, p], (keys,), [(None,), (None,)]))
    outs = [_pv(j % 2, scores, m, (v_ref[0, j // 2],)) for j, (scores, m) in enumerate(staged)]
    for p in range(N_PAIRS):
        o_ref[0, p] = _pick(0, outs[2 * p], outs[2 * p + 1]).astype(o_ref.dtype)


def _ctx_attention(q, kt, v):
    b, _, l, _ = q.shape
    spec = pl.BlockSpec((1, N_PAIRS, l, LANES), lambda bi: (bi, 0, 0, 0))
    kt_spec = pl.BlockSpec((1, N_PAIRS, l // KT_CHUNK, LANES, KT_CHUNK), lambda bi: (bi, 0, 0, 0, 0))
    return pl.pallas_call(
        _ctx_attn_kernel,
        grid=(b,),
        in_specs=[spec, kt_spec, spec],
        out_specs=spec,
        out_shape=jax.ShapeDtypeStruct((b, N_PAIRS, l, LANES), BF16),
        compiler_params=_params(1),
        name="ctx_attention",
    )(q, kt, v)


def _dft_cos_sin(k, n, period):
    ang = (2.0 * np.pi / period) * ((k[:, None] * n[None, :]) % period)
    return np.cos(ang), np.sin(ang)


def _table(a):
    return jnp.asarray(a, F32).astype(BF16)


def _channel_dft():
    m = np.arange(FOURIER_GROUP_DIM)
    c, s = _dft_cos_sin(m, m, FOURIER_GROUP_DIM)
    eye = np.eye(FOURIER_GROUPS)
    scale = FOURIER_GROUP_DIM ** -0.5
    return np.concatenate([np.kron(eye, c.T), np.kron(eye, s.T)], axis=0) * scale


def _channel_weights(cs_ref, wf_ref):
    return jnp.dot(cs_ref[...], wf_ref[...], preferred_element_type=F32).astype(BF16)


def _channel_mix(xr, xi, cw):
    x = jnp.concatenate([xr, xi], axis=-1).astype(BF16)
    return jnp.dot(x, cw, preferred_element_type=F32)


FFT_GROUP = 8


def _fft_kernel(x_ref, w1_ref, m2_ref, cs_ref, wf_ref, pm_ref, o_ref, ar_ref, ai_ref, *, n1):
    g8 = FFT_GROUP
    for g in range(GRID_W // g8):
        xg = x_ref[0, :, g * g8:(g + 1) * g8, :].reshape(n1 * g8, FOURIER_W).astype(BF16)
        a = jnp.dot(w1_ref[...], xg, preferred_element_type=F32)
        ar_ref[:, g * g8:(g + 1) * g8, :] = a[:n1 * g8].reshape(n1, g8, FOURIER_W)
        ai_ref[:, g * g8:(g + 1) * g8, :] = a[n1 * g8:].reshape(n1, g8, FOURIER_W)
    for k1 in range(n1):
        a = jnp.concatenate([ar_ref[k1], ai_ref[k1]], axis=0).astype(BF16)
        x = jnp.dot(m2_ref[k1], a, preferred_element_type=F32)
        ar_ref[k1] = x[:GRID_W]
        ai_ref[k1] = x[GRID_W:]
    cw = _channel_weights(cs_ref, wf_ref)
    for g in range(n1 // g8):
        xr = ar_ref[g * g8:(g + 1) * g8].reshape(g8 * GRID_W, FOURIER_W)
        xi = ai_ref[g * g8:(g + 1) * g8].reshape(g8 * GRID_W, FOURIER_W)
        z = _channel_mix(xr, xi, cw).astype(BF16)
        o = jnp.dot(pm_ref[...], z, preferred_element_type=F32)
        o_ref[0, :, g * g8:(g + 1) * g8, :] = o.reshape(GRID_W, g8, FOURIER_W)


def _fourier(uf, wf_bf):
    b, s, fw = uf.shape
    n1 = s // GRID_W
    g8 = FFT_GROUP
    assert s == n1 * GRID_W and n1 % g8 == 0 and fw == FOURIER_W
    k1 = np.arange(n1)
    c1, s1 = _dft_cos_sin(k1, k1, n1)
    w1 = np.kron(np.concatenate([c1, -s1], axis=0) * n1 ** -0.5, np.eye(g8))
    n2 = np.arange(GRID_W)
    kk = k1[:, None] + n1 * np.arange(GRID_W)[None, :]
    ang = (2.0 * np.pi / s) * ((kk[:, :, None] * n2[None, None, :]) % s)
    c2, s2 = np.cos(ang), np.sin(ang)
    m2 = np.concatenate([np.concatenate([c2, s2], axis=2),
                         np.concatenate([-s2, c2], axis=2)], axis=1) * GRID_W ** -0.5
    perm = np.zeros((GRID_W * g8, g8 * GRID_W))
    k2g, k1g = np.meshgrid(np.arange(GRID_W), np.arange(g8), indexing="ij")
    perm[(k2g * g8 + k1g).ravel(), (k1g * GRID_W + k2g).ravel()] = 1.0
    out = pl.pallas_call(
        functools.partial(_fft_kernel, n1=n1),
        grid=(b,),
        in_specs=[
            pl.BlockSpec((1, n1, GRID_W, fw), lambda bi: (bi, 0, 0, 0)),
            pl.BlockSpec(w1.shape, lambda bi: (0, 0)),
            pl.BlockSpec((n1, 2 * GRID_W, 2 * GRID_W), lambda bi: (0, 0, 0)),
            pl.BlockSpec((2 * fw, fw), lambda bi: (0, 0)),
            pl.BlockSpec((fw, fw), lambda bi: (0, 0)),
            pl.BlockSpec(perm.shape, lambda bi: (0, 0)),
        ],
        out_specs=pl.BlockSpec((1, GRID_W, n1, fw), lambda bi: (bi, 0, 0, 0)),
        out_shape=jax.ShapeDtypeStruct((b, GRID_W, n1, fw), F32),
        scratch_shapes=[pltpu.VMEM((n1, GRID_W, fw), F32), pltpu.VMEM((n1, GRID_W, fw), F32)],
        compiler_params=_params(1),
        name="fourier",
    )(uf.reshape(b, n1, GRID_W, fw), _table(w1), _table(m2), _table(_channel_dft()), wf_bf, _table(perm))
    return out.reshape(b, s, fw)


def _ctx_fft_kernel(x_ref, wd_ref, cs_ref, wf_ref, o_ref):
    n = x_ref.shape[1]
    x = jnp.dot(wd_ref[...], x_ref[0].astype(BF16), preferred_element_type=F32)
    o_ref[0] = _channel_mix(x[:n], x[n:], _channel_weights(cs_ref, wf_ref))


def _ctx_fourier(uf, wf_bf):
    b, n, fw = uf.shape
    k = np.arange(n)
    c, s = _dft_cos_sin(k, k, n)
    wd = np.concatenate([c, -s], axis=0) * n ** -0.5
    return pl.pallas_call(
        _ctx_fft_kernel,
        grid=(b,),
        in_specs=[
            pl.BlockSpec((1, n, fw), lambda bi: (bi, 0, 0)),
            pl.BlockSpec((2 * n, n), lambda bi: (0, 0)),
            pl.BlockSpec((2 * fw, fw), lambda bi: (0, 0)),
            pl.BlockSpec((fw, fw), lambda bi: (0, 0)),
        ],
        out_specs=pl.BlockSpec((1, n, fw), lambda bi: (bi, 0, 0)),
        out_shape=jax.ShapeDtypeStruct((b, n, fw), F32),
        compiler_params=_params(1),
        name="ctx_fourier",
    )(uf, _table(wd), _table(_channel_dft()), wf_bf)


CONV_PAD = 16
CONV_TILE = 64
TAIL_CHUNK = 256
SUBLANES = 8


def _conv_tile(up_ref, base, dw_ref):
    first = CONV_PAD - CONV_K // 2
    halo = CONV_TILE + 2 * CONV_PAD
    win = up_ref[pl.ds(base, halo), :]
    acc = jnp.zeros((CONV_TILE, CONV_W), F32)
    for ph in range(SUBLANES):
        taps = [t for t in range(CONV_K) if (first + t) % SUBLANES == ph]
        if not taps:
            continue
        shifted = win if ph == 0 else pltpu.roll(win, halo - ph, axis=0)
        for t in taps:
            off = first + t - ph
            acc = acc + shifted[off:off + CONV_TILE] * dw_ref[t:t + 1, :]
    return acc


def _tail_kernel(x_ref, ao_ref, ga_ref, fw_ref, gf_ref, u_ref, gc_ref, sa_ref, sf_ref, sc_ref, mod_ref, gp_ref,
                 pa_ref, pf_ref, pc_ref, wo_ref, dw_ref, db_ref, lg_ref, lb_ref, pw_ref, o_ref, up_ref):
    d = x_ref.shape[-1]
    tm = x_ref.shape[1]
    n = u_ref.shape[1]
    i = pl.program_id(1)

    @pl.when(i == 0)
    def _():
        zeros = jnp.zeros((CONV_PAD, CONV_W), F32)
        up_ref[0:CONV_PAD, :] = zeros
        up_ref[CONV_PAD + n:CONV_PAD + n + CONV_PAD, :] = zeros
        up_ref[CONV_PAD:CONV_PAD + n, :] = u_ref[0].astype(F32)

    ao = jnp.concatenate([ao_ref[0, p] for p in range(N_PAIRS)], axis=-1)
    a_in = ao * ga_ref[0]
    f_in = (fw_ref[0] * gf_ref[0].astype(F32)).astype(BF16)
    n_pieces = tm // CONV_TILE
    n_chunks = d // TAIL_CHUNK
    per_dot = -(-n_pieces // (2 * n_chunks))
    conv_pieces, ya, yf = [], [], []

    def some_conv():
        for j in range(len(conv_pieces), min(len(conv_pieces) + per_dot, n_pieces)):
            conv_pieces.append(_conv_tile(up_ref, pl.multiple_of(i * tm + j * CONV_TILE, CONV_TILE), dw_ref))

    for c in range(n_chunks):
        cols = slice(c * TAIL_CHUNK, (c + 1) * TAIL_CHUNK)
        ya.append(jnp.dot(a_in, pa_ref[:, cols], preferred_element_type=F32))
        some_conv()
        yf.append(jnp.dot(f_in, pf_ref[:, cols], preferred_element_type=F32))
        some_conv()
    ya, yf = jnp.concatenate(ya, axis=-1), jnp.concatenate(yf, axis=-1)
    conv = jnp.concatenate(conv_pieces, axis=0) + db_ref[...]
    xc = conv - jnp.mean(conv, axis=-1, keepdims=True)
    var = jnp.mean(xc * xc, axis=-1, keepdims=True)
    cn = xc * lax.rsqrt(var + NORM_EPS) * lg_ref[...] + lb_ref[...]
    cw = jnp.dot(_silu(cn).astype(BF16), pw_ref[...], preferred_element_type=F32) * gc_ref[0].astype(F32)
    yc = jnp.dot(cw.astype(BF16), pc_ref[...], preferred_element_type=F32)
    merged = (sa_ref[0].astype(F32) * ya + sf_ref[0].astype(F32) * yf + sc_ref[0].astype(F32) * yc)
    y = jnp.dot(merged.astype(BF16), wo_ref[...], preferred_element_type=F32)
    yn = y * lax.rsqrt(jnp.mean(y * y, axis=-1, keepdims=True) + NORM_EPS) * gp_ref[...]
    gate = mod_ref[0][:, 2 * d:]
    o_ref[0] = x_ref[0] + gate * yn


def _tail(x, ao, ga, fw, gf, u, gc, sa, sf, sc, mod_l, g_post_l, pa_bf, pf_bf, pc_bf, wo_bf,
          conv_dw_l, conv_db_l, ln_g_l, ln_b_l, w_pw_bf, mod_row, tm):
    b, s, d = x.shape
    tm = min(tm, s)
    assert s % tm == 0 and tm % CONV_TILE == 0 and d % TAIL_CHUNK == 0

    def tok(width):
        return pl.BlockSpec((1, tm, width), lambda bi, i: (bi, i, 0))

    def full(shape):
        return pl.BlockSpec(shape, lambda bi, i: (0,) * len(shape))

    return pl.pallas_call(
        _tail_kernel,
        grid=(b, s // tm),
        in_specs=[
            tok(d),
            pl.BlockSpec((1, N_PAIRS, tm, LANES), lambda bi, i: (bi, 0, i, 0)),
            tok(ATTN_W), tok(FOURIER_W), tok(FOURIER_W),
            pl.BlockSpec((1, s, CONV_W), lambda bi, i: (bi, 0, 0)),
            tok(CONV_W), tok(d), tok(d), tok(d),
            pl.BlockSpec((1, 1, 3 * d), lambda bi, i: (mod_row(bi), 0, 0)),
            full((1, d)),
            full(pa_bf.shape), full(pf_bf.shape), full(pc_bf.shape), full(wo_bf.shape),
            full((CONV_K, CONV_W)), full((1, CONV_W)), full((1, CONV_W)), full((1, CONV_W)), full(w_pw_bf.shape),
        ],
        out_specs=tok(d),
        out_shape=jax.ShapeDtypeStruct((b, s, d), F32),
        scratch_shapes=[pltpu.VMEM((s + 2 * CONV_PAD, CONV_W), F32)],
        compiler_params=_params(2),
        name="tail",
    )(x, ao, ga, fw, gf, u, gc, sa, sf, sc, mod_l.reshape(MOD_ROWS, 1, 3 * d), g_post_l.reshape(1, d),
      pa_bf, pf_bf, pc_bf, wo_bf, conv_dw_l, conv_db_l.reshape(1, CONV_W), ln_g_l.reshape(1, CONV_W),
      ln_b_l.reshape(1, CONV_W), w_pw_bf)


INPROJ_TILE = 512
TAIL_TILE = 512


def kernel(x, c, ctx, c_ctx, w_mod, b_mod, g_pre, g_post, w_in, rpb, w_four, conv_dw, conv_db,
           conv_ln_g, conv_ln_b, w_pw, p_attn, p_four, p_conv, w_out):
    batch, seq, d = x.shape
    depth = w_mod.shape[0]
    rows = seq // GRID_W
    assert batch < MOD_ROWS and rows % (Q_ROWS * ATTN_BLOCKS_PER_STEP) == 0 and rows >= K_ROWS
    ctx_row = batch
    c_all = jnp.zeros((MOD_ROWS, d), F32).at[:batch].set(c).at[ctx_row].set(c_ctx)
    mod = _modulation(c_all, w_mod, b_mod)
    merge_sections = _merge_sections(d)
    bias_tabs = _bias_pair_tables(rpb)

    def latent_row(bi):
        return bi

    def context_row(bi):
        return ctx_row

    w_in_bf = w_in.astype(BF16)
    for l in range(depth):
        wf_bf, pw_bf = w_four[l].astype(BF16), w_pw[l].astype(BF16)
        pa_bf, pf_bf, pc_bf, wo_bf = (p_attn[l].astype(BF16), p_four[l].astype(BF16),
                                      p_conv[l].astype(BF16), w_out[l].astype(BF16))
        conv_w = (conv_dw[l], conv_db[l], conv_ln_g[l], conv_ln_b[l], pw_bf)
        tail_w = (mod[l], g_post[l], pa_bf, pf_bf, pc_bf, wo_bf)
        update_ctx = l < depth - 1

        if update_ctx:
            ctx_sections, ctx_w, ctx_layer = ALL_SECTIONS + merge_sections, w_in_bf, l
        else:
            lo, hi = KV_SECTIONS[0][1], KV_SECTIONS[-1][2]
            ctx_sections = tuple((sec[0], sec[1] - lo, sec[2] - lo) + sec[3:] for sec in KV_SECTIONS)
            ctx_w, ctx_layer = w_in_bf[l:l + 1, :, lo:hi], 0
        ctx_out = dict(zip([sec[0] for sec in ctx_sections],
                           _inproj(ctx, mod[l], g_pre[l], ctx_w, ctx_layer, ctx_sections, context_row,
                                   INPROJ_TILE)))

        lat = dict(zip([sec[0] for sec in ALL_SECTIONS + merge_sections],
                       _inproj(x, mod[l], g_pre[l], w_in_bf, l, ALL_SECTIONS + merge_sections, latent_row,
                               INPROJ_TILE)))
        ao = _attention(lat["q"], lat["k"], lat["v"], ctx_out["k"], ctx_out["v"], bias_tabs[l])
        fw = _fourier(lat["uf"], wf_bf)
        x = _tail(x, ao, lat["ga"], fw, lat["gf"], lat["u"], lat["gc"], lat["sa"], lat["sf"], lat["sc"],
                  *tail_w, *conv_w, latent_row, TAIL_TILE)

        if update_ctx:
            co = ctx_out
            ao_c = _ctx_attention(co["q"], co["k"], co["v"])
            fw_c = _ctx_fourier(co["uf"], wf_bf)
            ctx = _tail(ctx, ao_c, co["ga"], fw_c, co["gf"], co["u"], co["gc"], co["sa"], co["sf"], co["sc"],
                        *tail_w, *conv_w, context_row, TAIL_TILE)
    return x
```

```python
import functools

import jax
import jax.numpy as jnp
import numpy as np
from jax import lax
from jax.experimental import pallas as pl
from jax.experimental.pallas import tpu as pltpu

F32 = jnp.float32
BF16 = jnp.bfloat16

GRID_W = 64
N_HEADS = 8
HEAD_DIM = 64
ATTN_W = N_HEADS * HEAD_DIM
LANES = 128
N_PAIRS = ATTN_W // LANES
WIN_ROWS = 8
WIN_COLS = 16
FOURIER_GROUPS = 4
FOURIER_GROUP_DIM = 64
FOURIER_W = FOURIER_GROUPS * FOURIER_GROUP_DIM
CONV_W = 256
CONV_K = 31
NORM_EPS = 1e-6
MASKED = -1e30
LOG2E = 1.4426950408889634

Q_ROWS = 2
K_ROWS = 10
MOD_ROWS = 8

VMEM_LIMIT = 56 * 1024 * 1024


def _sigmoid(x):
    return 1.0 / (1.0 + jnp.exp(-x))


def _silu(x):
    return x * _sigmoid(x)


def _params(n_axes):
    return pltpu.CompilerParams(
        dimension_semantics=("arbitrary",) * n_axes, vmem_limit_bytes=VMEM_LIMIT)


def _mod_kernel(c_ref, w_ref, b_ref, o_ref):
    sc = _silu(c_ref[...]).astype(BF16)
    o_ref[0] = jnp.dot(sc, w_ref[0].astype(BF16), preferred_element_type=F32) + b_ref[0]


def _modulation(c_all, w_mod, b_mod):
    depth, d, d3 = w_mod.shape
    tn = d
    return pl.pallas_call(
        _mod_kernel,
        grid=(depth, d3 // tn),
        in_specs=[
            pl.BlockSpec((MOD_ROWS, d), lambda l, j: (0, 0)),
            pl.BlockSpec((1, d, tn), lambda l, j: (l, 0, j)),
            pl.BlockSpec((1, 1, tn), lambda l, j: (l, 0, j)),
        ],
        out_specs=pl.BlockSpec((1, MOD_ROWS, tn), lambda l, j: (l, 0, j)),
        out_shape=jax.ShapeDtypeStruct((depth, MOD_ROWS, d3), F32),
        compiler_params=_params(2),
        name="modulation",
    )(c_all, w_mod, b_mod.reshape(depth, 1, d3))


KT_CHUNK = LANES
_SEC_Q = ("q", 0, ATTN_W, "qscale", ATTN_W, BF16, "pairs")
_SEC_K = ("k", ATTN_W, 2 * ATTN_W, None, ATTN_W, BF16, "pairs_t")
_SEC_V = ("v", 2 * ATTN_W, 3 * ATTN_W, None, ATTN_W, BF16, "pairs")
_O = 4 * ATTN_W
_SEC_REST = (
    ("ga", 3 * ATTN_W, _O, "silu", ATTN_W, BF16, "flat"),
    ("uf", _O, _O + FOURIER_W, None, FOURIER_W, F32, "flat"),
    ("gf", _O + FOURIER_W, _O + 2 * FOURIER_W, "silu", FOURIER_W, BF16, "flat"),
    ("u", _O + 2 * FOURIER_W, _O + 2 * FOURIER_W + 2 * CONV_W, "glu", CONV_W, BF16, "flat"),
    ("gc", _O + 2 * FOURIER_W + 2 * CONV_W, _O + 2 * FOURIER_W + 3 * CONV_W, "silu", CONV_W, BF16, "flat"),
)
_S0 = _O + 2 * FOURIER_W + 3 * CONV_W
ALL_SECTIONS = (_SEC_Q, _SEC_K, _SEC_V) + _SEC_REST
KV_SECTIONS = (_SEC_K, _SEC_V)


def _merge_sections(d):
    return tuple((n, _S0 + i * d, _S0 + (i + 1) * d, "sigmoid", d, BF16, "flat")
                 for i, n in enumerate(("sa", "sf", "sc")))


INPROJ_MIN_ROWS = 128
INPROJ_MAX_GROUPS = 4


def _inproj_kernel(x_ref, mod_ref, g_ref, w_ref, *o_refs, sections):
    tm, d = x_ref.shape[1], x_ref.shape[2]
    m = mod_ref[0]
    shift, scale = m[:, :d], m[:, d:2 * d]
    epilogue_rank = {"sigmoid": 0, "glu": 1, "silu": 2, "qscale": 3, None: 4}
    order = sorted(range(len(sections)), key=lambda j: epilogue_rank[sections[j][3]])
    groups = max(1, min(INPROJ_MAX_GROUPS, tm // INPROJ_MIN_ROWS))
    sub = tm // groups
    for h in range(groups):
        r = slice(h * sub, (h + 1) * sub)
        x = x_ref[0, r, :]
        y = x * lax.rsqrt(jnp.mean(x * x, axis=-1, keepdims=True) + NORM_EPS) * g_ref[...]
        hb = (y * (1.0 + scale) + shift).astype(BF16)
        for (_, lo, hi, act, _, dtype, layout), o_ref in [(sections[j], o_refs[j]) for j in order]:
            acc = jnp.dot(hb, w_ref[:, lo:hi], preferred_element_type=F32)
            if act == "qscale":
                acc = acc * (LOG2E * HEAD_DIM ** -0.5)
            elif act == "silu":
                acc = _silu(acc)
            elif act == "sigmoid":
                acc = _sigmoid(acc)
            elif act == "glu":
                half = (hi - lo) // 2
                acc = acc[:, :half] * _sigmoid(acc[:, half:])
            if layout == "pairs":
                for p in range(N_PAIRS):
                    o_ref[0, p, r, :] = acc[:, p * LANES:(p + 1) * LANES].astype(dtype)
            elif layout == "pairs_t":
                for p in range(N_PAIRS):
                    for c in range(sub // KT_CHUNK):
                        piece = acc[c * KT_CHUNK:(c + 1) * KT_CHUNK, p * LANES:(p + 1) * LANES]
                        o_ref[0, p, h * (sub // KT_CHUNK) + c] = piece.T.astype(dtype)
            else:
                o_ref[0, r, :] = acc.astype(dtype)


def _inproj(x, mod_l, g_pre_l, w_in_bf, layer, sections, mod_row, tm):
    b, s, d = x.shape
    tm = min(tm, s)
    assert s % tm == 0 and tm % SUBLANES == 0
    in_w = w_in_bf.shape[-1]
    out_shapes, out_specs = [], []
    for (_, _, _, _, width, dtype, layout) in sections:
        if layout == "pairs":
            out_shapes.append(jax.ShapeDtypeStruct((b, N_PAIRS, s, LANES), dtype))
            out_specs.append(pl.BlockSpec((1, N_PAIRS, tm, LANES), lambda bi, i: (bi, 0, i, 0)))
        elif layout == "pairs_t":
            assert tm % KT_CHUNK == 0 and (tm // max(1, min(INPROJ_MAX_GROUPS, tm // INPROJ_MIN_ROWS))) % KT_CHUNK == 0
            out_shapes.append(jax.ShapeDtypeStruct((b, N_PAIRS, s // KT_CHUNK, LANES, KT_CHUNK), dtype))
            out_specs.append(pl.BlockSpec((1, N_PAIRS, tm // KT_CHUNK, LANES, KT_CHUNK),
                                          lambda bi, i: (bi, 0, i, 0, 0)))
        else:
            out_shapes.append(jax.ShapeDtypeStruct((b, s, width), dtype))
            out_specs.append(pl.BlockSpec((1, tm, width), lambda bi, i: (bi, i, 0)))
    return pl.pallas_call(
        functools.partial(_inproj_kernel, sections=sections),
        grid=(b, s // tm),
        in_specs=[
            pl.BlockSpec((1, tm, d), lambda bi, i: (bi, i, 0)),
            pl.BlockSpec((1, 1, 3 * d), lambda bi, i: (mod_row(bi), 0, 0)),
            pl.BlockSpec((1, d), lambda bi, i: (0, 0)),
            pl.BlockSpec((None, d, in_w), lambda bi, i: (layer, 0, 0)),
        ],
        out_specs=out_specs,
        out_shape=out_shapes,
        compiler_params=_params(2),
        name="inproj",
    )(x, mod_l.reshape(MOD_ROWS, 1, 3 * d), g_pre_l.reshape(1, d), w_in_bf)


PAIR_STARTS = 2 * WIN_ROWS


def _bias_pair_tables(rpb):
    cols = np.arange(GRID_W)
    col_start = np.clip(cols - WIN_COLS // 2, 0, GRID_W - WIN_COLS)
    valid_c = (cols[None, :] >= col_start[:, None]) & (cols[None, :] < col_start[:, None] + WIN_COLS)
    dc = cols[None, :] - cols[:, None] + (WIN_COLS - 1)
    one_hot = (dc[None] == np.arange(2 * WIN_COLS - 1)[:, None, None]) & valid_c[None]
    per_dr = jnp.einsum("lhdp,pck->lhdck", rpb, jnp.asarray(one_hot, F32), precision=lax.Precision.HIGHEST)
    per_dr = jnp.where(valid_c, per_dr * LOG2E, MASKED)
    masked = jnp.full_like(per_dr[:, :, :1], MASKED)
    ext = jnp.concatenate([masked, per_dr, masked], axis=2)
    return jnp.concatenate([ext[:, :, :-1], ext[:, :, 1:]], axis=-1)


def _window_plan(i, ws, rows):
    kr = min(WIN_ROWS, rows)
    low = _head_masks()
    plan = []
    for ri in range(Q_ROWS):
        r = Q_ROWS * i + ri
        rs = jnp.clip(r - kr // 2, 0, rows - kr)
        for jj in range(K_ROWS // 2):
            krow = ws + 2 * jj
            ok_lo = (krow >= rs) & (krow < rs + kr)
            ok_hi = (krow + 1 >= rs) & (krow + 1 < rs + kr)
            ok = jnp.where(low, ok_lo.astype(jnp.int32), ok_hi.astype(jnp.int32)) > 0
            start = jnp.clip(krow - r + WIN_ROWS, 0, PAIR_STARTS - 1)
            plan.append((ok, start))
    return plan


def _window_bias(tab_ref, head, plan):
    n_jj = K_ROWS // 2
    row_blocks = []
    for ri in range(Q_ROWS):
        lane_blocks = [jnp.where(ok, tab_ref[head, start], MASKED) for ok, start in plan[ri * n_jj:(ri + 1) * n_jj]]
        row_blocks.append(jnp.concatenate(lane_blocks, axis=-1))
    return jnp.concatenate(row_blocks, axis=0)


def _head_masks():
    lane = lax.broadcasted_iota(jnp.int32, (1, LANES), 1)
    return lane < HEAD_DIM


def _pick(head, own, other):
    low = _head_masks()
    return jnp.where(low, own, other) if head == 0 else jnp.where(low, other, own)


def _pair_scores(q, keys, biases):
    out = []
    for head in (0, 1):
        dims = slice(head * HEAD_DIM, (head + 1) * HEAD_DIM)
        scores = [jnp.dot(q[:, dims], kt[dims, :], preferred_element_type=F32) for kt in keys]
        scores = [s if bias is None else s + bias for s, bias in zip(scores, biases[head])]
        m = functools.reduce(jnp.maximum, [jnp.max(s, axis=-1, keepdims=True) for s in scores])
        out.append((scores, m))
    return out


def _pv(head, scores, m, vals):
    o = functools.reduce(jnp.add, [
        jnp.dot(jnp.exp2(s - m).astype(BF16), _pick(head, v, jnp.ones_like(v)), preferred_element_type=F32)
        for s, v in zip(scores, vals)])
    return o / pltpu.roll(o, HEAD_DIM, axis=1)


ATTN_BLOCKS_PER_STEP = 4
ATTN_PAIRS_PER_STEP = 4


def _key_chunks(kt_ref, p, first, count):
    return jnp.concatenate([kt_ref[0, p, first + c] for c in range(count)], axis=-1)


def _attn_kernel(q_ref, kt_ref, v_ref, kct_ref, vc_ref, tab_ref, o_ref, *, rows, pairs):
    tq = Q_ROWS * GRID_W
    win_chunks = K_ROWS * GRID_W // KT_CHUNK
    ctx_chunks = kct_ref.shape[2]
    staged, starts = [], []
    for blk in range(ATTN_BLOCKS_PER_STEP):
        i = pl.program_id(2) * ATTN_BLOCKS_PER_STEP + blk
        ws = jnp.clip(Q_ROWS * i - min(WIN_ROWS, rows) // 2, 0, rows - K_ROWS)
        start = pl.multiple_of(ws * GRID_W, GRID_W)
        starts.append(start)
        plan = _window_plan(i, ws, rows)
        for p in range(pairs):
            kw = _key_chunks(kt_ref, p, ws * GRID_W // KT_CHUNK, win_chunks)
            kc = _key_chunks(kct_ref, p, 0, ctx_chunks)
            biases = [(_window_bias(tab_ref, 2 * p + head, plan), None) for head in (0, 1)]
            staged.extend(_pair_scores(q_ref[0, p, blk * tq:(blk + 1) * tq, :], (kw, kc), biases))
    outs = []
    for j, (scores, m) in enumerate(staged):
        blk, jj = divmod(j, 2 * pairs)
        p, head = divmod(jj, 2)
        vw = v_ref[0, p, pl.ds(starts[blk], K_ROWS * GRID_W), :]
        outs.append(_pv(head, scores, m, (vw, vc_ref[0, p])))
    for blk in range(ATTN_BLOCKS_PER_STEP):
        for p in range(pairs):
            j = blk * 2 * pairs + 2 * p
            o_ref[0, p, blk * tq:(blk + 1) * tq, :] = _pick(0, outs[j], outs[j + 1]).astype(o_ref.dtype)


def _attention(q, kt, v, kct, vc, tab):
    b, _, s, _ = q.shape
    l = vc.shape[2]
    rows = s // GRID_W
    n_blk = rows // Q_ROWS
    tq = Q_ROWS * GRID_W * ATTN_BLOCKS_PER_STEP
    pp = ATTN_PAIRS_PER_STEP
    assert s == rows * GRID_W and n_blk % ATTN_BLOCKS_PER_STEP == 0 and rows >= K_ROWS and N_PAIRS % pp == 0
    chunk_rows = KT_CHUNK // GRID_W
    assert Q_ROWS % chunk_rows == 0 and (min(WIN_ROWS, rows) // 2) % chunk_rows == 0
    assert (rows - K_ROWS) % chunk_rows == 0 and K_ROWS % chunk_rows == 0 and l % KT_CHUNK == 0

    return pl.pallas_call(
        functools.partial(_attn_kernel, rows=rows, pairs=pp),
        grid=(b, N_PAIRS // pp, n_blk // ATTN_BLOCKS_PER_STEP),
        in_specs=[
            pl.BlockSpec((1, pp, tq, LANES), lambda bi, p, i: (bi, p, i, 0)),
            pl.BlockSpec((1, pp, s // KT_CHUNK, LANES, KT_CHUNK), lambda bi, p, i: (bi, p, 0, 0, 0)),
            pl.BlockSpec((1, pp, s, LANES), lambda bi, p, i: (bi, p, 0, 0)),
            pl.BlockSpec((1, pp, l // KT_CHUNK, LANES, KT_CHUNK), lambda bi, p, i: (bi, p, 0, 0, 0)),
            pl.BlockSpec((1, pp, l, LANES), lambda bi, p, i: (bi, p, 0, 0)),
            pl.BlockSpec((2 * pp, PAIR_STARTS, GRID_W, LANES), lambda bi, p, i: (p, 0, 0, 0)),
        ],
        out_specs=pl.BlockSpec((1, pp, tq, LANES), lambda bi, p, i: (bi, p, i, 0)),
        out_shape=jax.ShapeDtypeStruct((b, N_PAIRS, s, LANES), BF16),
        compiler_params=_params(3),
        name="attention",
    )(q, kt, v, kct, vc, tab)


def _ctx_attn_kernel(q_ref, kt_ref, v_ref, o_ref):
    staged = []
    for p in range(N_PAIRS):
        keys = _key_chunks(kt_ref, p, 0, kt_ref.shape[2])
        staged.extend(_pair_scores(q_ref[0, p], (keys,), [(None,), (None,)]))
    outs = [_pv(j % 2, scores, m, (v_ref[0, j // 2],)) for j, (scores, m) in enumerate(staged)]
    for p in range(N_PAIRS):
        o_ref[0, p] = _pick(0, outs[2 * p], outs[2 * p + 1]).astype(o_ref.dtype)


def _ctx_attention(q, kt, v):
    b, _, l, _ = q.shape
    spec = pl.BlockSpec((1, N_PAIRS, l, LANES), lambda bi: (bi, 0, 0, 0))
    kt_spec = pl.BlockSpec((1, N_PAIRS, l // KT_CHUNK, LANES, KT_CHUNK), lambda bi: (bi, 0, 0, 0, 0))
    return pl.pallas_call(
        _ctx_attn_kernel,
        grid=(b,),
        in_specs=[spec, kt_spec, spec],
        out_specs=spec,
        out_shape=jax.ShapeDtypeStruct((b, N_PAIRS, l, LANES), BF16),
        compiler_params=_params(1),
        name="ctx_attention",
    )(q, kt, v)


def _dft_cos_sin(k, n, period):
    ang = (2.0 * np.pi / period) * ((k[:, None] * n[None, :]) % period)
    return np.cos(ang), np.sin(ang)


def _table(a):
    return jnp.asarray(a, F32).astype(BF16)


def _channel_dft():
    m = np.arange(FOURIER_GROUP_DIM)
    c, s = _dft_cos_sin(m, m, FOURIER_GROUP_DIM)
    eye = np.eye(FOURIER_GROUPS)
    scale = FOURIER_GROUP_DIM ** -0.5
    return np.concatenate([np.kron(eye, c.T), np.kron(eye, s.T)], axis=0) * scale


def _channel_weights(cs_ref, wf_ref):
    return jnp.dot(cs_ref[...], wf_ref[...], preferred_element_type=F32).astype(BF16)


def _channel_mix(xr, xi, cw):
    x = jnp.concatenate([xr, xi], axis=-1).astype(BF16)
    return jnp.dot(x, cw, preferred_element_type=F32)


FFT_GROUP = 8


def _fft_kernel(x_ref, w1_ref, m2_ref, cs_ref, wf_ref, pm_ref, o_ref, ar_ref, ai_ref, *, n1):
    g8 = FFT_GROUP
    for g in range(GRID_W // g8):
        xg = x_ref[0, :, g * g8:(g + 1) * g8, :].reshape(n1 * g8, FOURIER_W).astype(BF16)
        a = jnp.dot(w1_ref[...], xg, preferred_element_type=F32)
        ar_ref[:, g * g8:(g + 1) * g8, :] = a[:n1 * g8].reshape(n1, g8, FOURIER_W)
        ai_ref[:, g * g8:(g + 1) * g8, :] = a[n1 * g8:].reshape(n1, g8, FOURIER_W)
    for k1 in range(n1):
        a = jnp.concatenate([ar_ref[k1], ai_ref[k1]], axis=0).astype(BF16)
        x = jnp.dot(m2_ref[k1], a, preferred_element_type=F32)
        ar_ref[k1] = x[:GRID_W]
        ai_ref[k1] = x[GRID_W:]
    cw = _channel_weights(cs_ref, wf_ref)
    for g in range(n1 // g8):
        xr = ar_ref[g * g8:(g + 1) * g8].reshape(g8 * GRID_W, FOURIER_W)
        xi = ai_ref[g * g8:(g + 1) * g8].reshape(g8 * GRID_W, FOURIER_W)
        z = _channel_mix(xr, xi, cw).astype(BF16)
        o = jnp.dot(pm_ref[...], z, preferred_element_type=F32)
        o_ref[0, :, g * g8:(g + 1) * g8, :] = o.reshape(GRID_W, g8, FOURIER_W)


def _fourier(uf, wf_bf):
    b, s, fw = uf.shape
    n1 = s // GRID_W
    g8 = FFT_GROUP
    assert s == n1 * GRID_W and n1 % g8 == 0 and fw == FOURIER_W
    k1 = np.arange(n1)
    c1, s1 = _dft_cos_sin(k1, k1, n1)
    w1 = np.kron(np.concatenate([c1, -s1], axis=0) * n1 ** -0.5, np.eye(g8))
    n2 = np.arange(GRID_W)
    kk = k1[:, None] + n1 * np.arange(GRID_W)[None, :]
    ang = (2.0 * np.pi / s) * ((kk[:, :, None] * n2[None, None, :]) % s)
    c2, s2 = np.cos(ang), np.sin(ang)
    m2 = np.concatenate([np.concatenate([c2, s2], axis=2),
                         np.concatenate([-s2, c2], axis=2)], axis=1) * GRID_W ** -0.5
    perm = np.zeros((GRID_W * g8, g8 * GRID_W))
    k2g, k1g = np.meshgrid(np.arange(GRID_W), np.arange(g8), indexing="ij")
    perm[(k2g * g8 + k1g).ravel(), (k1g * GRID_W + k2g).ravel()] = 1.0
    out = pl.pallas_call(
        functools.partial(_fft_kernel, n1=n1),
        grid=(b,),
        in_specs=[
            pl.BlockSpec((1, n1, GRID_W, fw), lambda bi: (bi, 0, 0, 0)),
            pl.BlockSpec(w1.shape, lambda bi: (0, 0)),
            pl.BlockSpec((n1, 2 * GRID_W, 2 * GRID_W), lambda bi: (0, 0, 0)),
            pl.BlockSpec((2 * fw, fw), lambda bi: (0, 0)),
            pl.BlockSpec((fw, fw), lambda bi: (0, 0)),
            pl.BlockSpec(perm.shape, lambda bi: (0, 0)),
        ],
        out_specs=pl.BlockSpec((1, GRID_W, n1, fw), lambda bi: (bi, 0, 0, 0)),
        out_shape=jax.ShapeDtypeStruct((b, GRID_W, n1, fw), F32),
        scratch_shapes=[pltpu.VMEM((n1, GRID_W, fw), F32), pltpu.VMEM((n1, GRID_W, fw), F32)],
        compiler_params=_params(1),
        name="fourier",
    )(uf.reshape(b, n1, GRID_W, fw), _table(w1), _table(m2), _table(_channel_dft()), wf_bf, _table(perm))
    return out.reshape(b, s, fw)


def _ctx_fft_kernel(x_ref, wd_ref, cs_ref, wf_ref, o_ref):
    n = x_ref.shape[1]
    x = jnp.dot(wd_ref[...], x_ref[0].astype(BF16), preferred_element_type=F32)
    o_ref[0] = _channel_mix(x[:n], x[n:], _channel_weights(cs_ref, wf_ref))


def _ctx_fourier(uf, wf_bf):
    b, n, fw = uf.shape
    k = np.arange(n)
    c, s = _dft_cos_sin(k, k, n)
    wd = np.concatenate([c, -s], axis=0) * n ** -0.5
    return pl.pallas_call(
        _ctx_fft_kernel,
        grid=(b,),
        in_specs=[
            pl.BlockSpec((1, n, fw), lambda bi: (bi, 0, 0)),
            pl.BlockSpec((2 * n, n), lambda bi: (0, 0)),
            pl.BlockSpec((2 * fw, fw), lambda bi: (0, 0)),
            pl.BlockSpec((fw, fw), lambda bi: (0, 0)),
        ],
        out_specs=pl.BlockSpec((1, n, fw), lambda bi: (bi, 0, 0)),
        out_shape=jax.ShapeDtypeStruct((b, n, fw), F32),
        compiler_params=_params(1),
        name="ctx_fourier",
    )(uf, _table(wd), _table(_channel_dft()), wf_bf)


CONV_PAD = 16
CONV_TILE = 64
TAIL_CHUNK = 256
SUBLANES = 8


def _conv_tile(up_ref, base, dw_ref):
    first = CONV_PAD - CONV_K // 2
    halo = CONV_TILE + 2 * CONV_PAD
    win = up_ref[pl.ds(base, halo), :]
    acc = jnp.zeros((CONV_TILE, CONV_W), F32)
    for ph in range(SUBLANES):
        taps = [t for t in range(CONV_K) if (first + t) % SUBLANES == ph]
        if not taps:
            continue
        shifted = win if ph == 0 else pltpu.roll(win, halo - ph, axis=0)
        for t in taps:
            off = first + t - ph
            acc = acc + shifted[off:off + CONV_TILE] * dw_ref[t:t + 1, :]
    return acc


def _tail_kernel(x_ref, ao_ref, ga_ref, fw_ref, gf_ref, u_ref, gc_ref, sa_ref, sf_ref, sc_ref, mod_ref, gp_ref,
                 pa_ref, pf_ref, pc_ref, wo_ref, dw_ref, db_ref, lg_ref, lb_ref, pw_ref, o_ref, up_ref):
    d = x_ref.shape[-1]
    tm = x_ref.shape[1]
    n = u_ref.shape[1]
    i = pl.program_id(1)

    @pl.when(i == 0)
    def _():
        zeros = jnp.zeros((CONV_PAD, CONV_W), F32)
        up_ref[0:CONV_PAD, :] = zeros
        up_ref[CONV_PAD + n:CONV_PAD + n + CONV_PAD, :] = zeros
        up_ref[CONV_PAD:CONV_PAD + n, :] = u_ref[0].astype(F32)

    ao = jnp.concatenate([ao_ref[0, p] for p in range(N_PAIRS)], axis=-1)
    a_in = ao * ga_ref[0]
    f_in = (fw_ref[0] * gf_ref[0].astype(F32)).astype(BF16)
    n_pieces = tm // CONV_TILE
    n_chunks = d // TAIL_CHUNK
    per_dot = -(-n_pieces // (2 * n_chunks))
    conv_pieces, ya, yf = [], [], []

    def some_conv():
        for j in range(len(conv_pieces), min(len(conv_pieces) + per_dot, n_pieces)):
            conv_pieces.append(_conv_tile(up_ref, pl.multiple_of(i * tm + j * CONV_TILE, CONV_TILE), dw_ref))

    for c in range(n_chunks):
        cols = slice(c * TAIL_CHUNK, (c + 1) * TAIL_CHUNK)
        ya.append(jnp.dot(a_in, pa_ref[:, cols], preferred_element_type=F32))
        some_conv()
        yf.append(jnp.dot(f_in, pf_ref[:, cols], preferred_element_type=F32))
        some_conv()
    ya, yf = jnp.concatenate(ya, axis=-1), jnp.concatenate(yf, axis=-1)
    conv = jnp.concatenate(conv_pieces, axis=0) + db_ref[...]
    xc = conv - jnp.mean(conv, axis=-1, keepdims=True)
    var = jnp.mean(xc * xc, axis=-1, keepdims=True)
    cn = xc * lax.rsqrt(var + NORM_EPS) * lg_ref[...] + lb_ref[...]
    cw = jnp.dot(_silu(cn).astype(BF16), pw_ref[...], preferred_element_type=F32) * gc_ref[0].astype(F32)
    yc = jnp.dot(cw.astype(BF16), pc_ref[...], preferred_element_type=F32)
    merged = (sa_ref[0].astype(F32) * ya + sf_ref[0].astype(F32) * yf + sc_ref[0].astype(F32) * yc)
    y = jnp.dot(merged.astype(BF16), wo_ref[...], preferred_element_type=F32)
    yn = y * lax.rsqrt(jnp.mean(y * y, axis=-1, keepdims=True) + NORM_EPS) * gp_ref[...]
    gate = mod_ref[0][:, 2 * d:]
    o_ref[0] = x_ref[0] + gate * yn


def _tail(x, ao, ga, fw, gf, u, gc, sa, sf, sc, mod_l, g_post_l, pa_bf, pf_bf, pc_bf, wo_bf,
          conv_dw_l, conv_db_l, ln_g_l, ln_b_l, w_pw_bf, mod_row, tm):
    b, s, d = x.shape
    tm = min(tm, s)
    assert s % tm == 0 and tm % CONV_TILE == 0 and d % TAIL_CHUNK == 0

    def tok(width):
        return pl.BlockSpec((1, tm, width), lambda bi, i: (bi, i, 0))

    def full(shape):
        return pl.BlockSpec(shape, lambda bi, i: (0,) * len(shape))

    return pl.pallas_call(
        _tail_kernel,
        grid=(b, s // tm),
        in_specs=[
            tok(d),
            pl.BlockSpec((1, N_PAIRS, tm, LANES), lambda bi, i: (bi, 0, i, 0)),
            tok(ATTN_W), tok(FOURIER_W), tok(FOURIER_W),
            pl.BlockSpec((1, s, CONV_W), lambda bi, i: (bi, 0, 0)),
            tok(CONV_W), tok(d), tok(d), tok(d),
            pl.BlockSpec((1, 1, 3 * d), lambda bi, i: (mod_row(bi), 0, 0)),
            full((1, d)),
            full(pa_bf.shape), full(pf_bf.shape), full(pc_bf.shape), full(wo_bf.shape),
            full((CONV_K, CONV_W)), full((1, CONV_W)), full((1, CONV_W)), full((1, CONV_W)), full(w_pw_bf.shape),
        ],
        out_specs=tok(d),
        out_shape=jax.ShapeDtypeStruct((b, s, d), F32),
        scratch_shapes=[pltpu.VMEM((s + 2 * CONV_PAD, CONV_W), F32)],
        compiler_params=_params(2),
        name="tail",
    )(x, ao, ga, fw, gf, u, gc, sa, sf, sc, mod_l.reshape(MOD_ROWS, 1, 3 * d), g_post_l.reshape(1, d),
      pa_bf, pf_bf, pc_bf, wo_bf, conv_dw_l, conv_db_l.reshape(1, CONV_W), ln_g_l.reshape(1, CONV_W),
      ln_b_l.reshape(1, CONV_W), w_pw_bf)


INPROJ_TILE = 512
TAIL_TILE = 512


def kernel(x, c, ctx, c_ctx, w_mod, b_mod, g_pre, g_post, w_in, rpb, w_four, conv_dw, conv_db,
           conv_ln_g, conv_ln_b, w_pw, p_attn, p_four, p_conv, w_out):
    batch, seq, d = x.shape
    depth = w_mod.shape[0]
    rows = seq // GRID_W
    assert batch < MOD_ROWS and rows % (Q_ROWS * ATTN_BLOCKS_PER_STEP) == 0 and rows >= K_ROWS
    ctx_row = batch
    c_all = jnp.zeros((MOD_ROWS, d), F32).at[:batch].set(c).at[ctx_row].set(c_ctx)
    mod = _modulation(c_all, w_mod, b_mod)
    merge_sections = _merge_sections(d)
    bias_tabs = _bias_pair_tables(rpb)

    def latent_row(bi):
        return bi

    def context_row(bi):
        return ctx_row

    w_in_bf = w_in.astype(BF16)
    for l in range(depth):
        wf_bf, pw_bf = w_four[l].astype(BF16), w_pw[l].astype(BF16)
        pa_bf, pf_bf, pc_bf, wo_bf = (p_attn[l].astype(BF16), p_four[l].astype(BF16),
                                      p_conv[l].astype(BF16), w_out[l].astype(BF16))
        conv_w = (conv_dw[l], conv_db[l], conv_ln_g[l], conv_ln_b[l], pw_bf)
        tail_w = (mod[l], g_post[l], pa_bf, pf_bf, pc_bf, wo_bf)
        update_ctx = l < depth - 1

        if update_ctx:
            ctx_sections, ctx_w, ctx_layer = ALL_SECTIONS + merge_sections, w_in_bf, l
        else:
            lo, hi = KV_SECTIONS[0][1], KV_SECTIONS[-1][2]
            ctx_sections = tuple((sec[0], sec[1] - lo, sec[2] - lo) + sec[3:] for sec in KV_SECTIONS)
            ctx_w, ctx_layer = w_in_bf[l:l + 1, :, lo:hi], 0
        ctx_out = dict(zip([sec[0] for sec in ctx_sections],
                           _inproj(ctx, mod[l], g_pre[l], ctx_w, ctx_layer, ctx_sections, context_row,
                                   INPROJ_TILE)))

        lat = dict(zip([sec[0] for sec in ALL_SECTIONS + merge_sections],
                       _inproj(x, mod[l], g_pre[l], w_in_bf, l, ALL_SECTIONS + merge_sections, latent_row,
                               INPROJ_TILE)))
        ao = _attention(lat["q"], lat["k"], lat["v"], ctx_out["k"], ctx_out["v"], bias_tabs[l])
        fw = _fourier(lat["uf"], wf_bf)
        x = _tail(x, ao, lat["ga"], fw, lat["gf"], lat["u"], lat["gc"], lat["sa"], lat["sf"], lat["sc"],
                  *tail_w, *conv_w, latent_row, TAIL_TILE)

        if update_ctx:
            co = ctx_out
            ao_c = _ctx_attention(co["q"], co["k"], co["v"])
            fw_c = _ctx_fourier(co["uf"], wf_bf)
            ctx = _tail(ctx, ao_c, co["ga"], fw_c, co["gf"], co["u"], co["gc"], co["sa"], co["sf"], co["sc"],
                        *tail_w, *conv_w, context_row, TAIL_TILE)
    return x
```

```python
import functools

import jax
import jax.numpy as jnp
import numpy as np
from jax import lax
from jax.experimental import pallas as pl
from jax.experimental.pallas import tpu as pltpu

F32 = jnp.float32
BF16 = jnp.bfloat16

GRID_W = 64
N_HEADS = 8
HEAD_DIM = 64
ATTN_W = N_HEADS * HEAD_DIM
LANES = 128
N_PAIRS = ATTN_W // LANES
WIN_ROWS = 8
WIN_COLS = 16
FOURIER_GROUPS = 4
FOURIER_GROUP_DIM = 64
FOURIER_W = FOURIER_GROUPS * FOURIER_GROUP_DIM
CONV_W = 256
CONV_K = 31
NORM_EPS = 1e-6
MASKED = -1e30
LOG2E = 1.4426950408889634

Q_ROWS = 2
K_ROWS = 10
MOD_ROWS = 8

VMEM_LIMIT = 56 * 1024 * 1024


def _sigmoid(x):
    return 1.0 / (1.0 + jnp.exp(-x))


def _silu(x):
    return x * _sigmoid(x)


def _params(n_axes):
    return pltpu.CompilerParams(
        dimension_semantics=("arbitrary",) * n_axes, vmem_limit_bytes=VMEM_LIMIT)


def _mod_kernel(c_ref, w_ref, b_ref, o_ref):
    sc = _silu(c_ref[...]).astype(BF16)
    o_ref[0] = jnp.dot(sc, w_ref[0].astype(BF16), preferred_element_type=F32) + b_ref[0]


def _modulation(c_all, w_mod, b_mod):
    depth, d, d3 = w_mod.shape
    tn = d
    return pl.pallas_call(
        _mod_kernel,
        grid=(depth, d3 // tn),
        in_specs=[
            pl.BlockSpec((MOD_ROWS, d), lambda l, j: (0, 0)),
            pl.BlockSpec((1, d, tn), lambda l, j: (l, 0, j)),
            pl.BlockSpec((1, 1, tn), lambda l, j: (l, 0, j)),
        ],
        out_specs=pl.BlockSpec((1, MOD_ROWS, tn), lambda l, j: (l, 0, j)),
        out_shape=jax.ShapeDtypeStruct((depth, MOD_ROWS, d3), F32),
        compiler_params=_params(2),
        name="modulation",
    )(c_all, w_mod, b_mod.reshape(depth, 1, d3))


KT_CHUNK = LANES
_SEC_Q = ("q", 0, ATTN_W, "qscale", ATTN_W, BF16, "pairs")
_SEC_K = ("k", ATTN_W, 2 * ATTN_W, None, ATTN_W, BF16, "pairs_t")
_SEC_V = ("v", 2 * ATTN_W, 3 * ATTN_W, None, ATTN_W, BF16, "pairs_1")
_O = 4 * ATTN_W
_SEC_REST = (
    ("ga", 3 * ATTN_W, _O, "silu", ATTN_W, BF16, "flat"),
    ("uf", _O, _O + FOURIER_W, None, FOURIER_W, F32, "flat"),
    ("gf", _O + FOURIER_W, _O + 2 * FOURIER_W, "silu", FOURIER_W, BF16, "flat"),
    ("u", _O + 2 * FOURIER_W, _O + 2 * FOURIER_W + 2 * CONV_W, "glu", CONV_W, BF16, "flat"),
    ("gc", _O + 2 * FOURIER_W + 2 * CONV_W, _O + 2 * FOURIER_W + 3 * CONV_W, "silu", CONV_W, BF16, "flat"),
)
_S0 = _O + 2 * FOURIER_W + 3 * CONV_W
ALL_SECTIONS = (_SEC_Q, _SEC_K, _SEC_V) + _SEC_REST
KV_SECTIONS = (_SEC_K, _SEC_V)


def _merge_sections(d):
    return tuple((n, _S0 + i * d, _S0 + (i + 1) * d, "sigmoid", d, BF16, "flat")
                 for i, n in enumerate(("sa", "sf", "sc")))


INPROJ_MIN_ROWS = 128
INPROJ_MAX_GROUPS = 4


def _inproj_kernel(x_ref, mod_ref, g_ref, w_ref, *o_refs, sections):
    tm, d = x_ref.shape[1], x_ref.shape[2]
    m = mod_ref[0]
    shift, scale = m[:, :d], m[:, d:2 * d]
    epilogue_rank = {"sigmoid": 0, "glu": 1, "silu": 2, "qscale": 3, None: 4}
    order = sorted(range(len(sections)), key=lambda j: epilogue_rank[sections[j][3]])
    groups = max(1, min(INPROJ_MAX_GROUPS, tm // INPROJ_MIN_ROWS))
    sub = tm // groups
    for h in range(groups):
        r = slice(h * sub, (h + 1) * sub)
        x = x_ref[0, r, :]
        y = x * lax.rsqrt(jnp.mean(x * x, axis=-1, keepdims=True) + NORM_EPS) * g_ref[...]
        hb = (y * (1.0 + scale) + shift).astype(BF16)
        for (_, lo, hi, act, _, dtype, layout), o_ref in [(sections[j], o_refs[j]) for j in order]:
            acc = jnp.dot(hb, w_ref[:, lo:hi], preferred_element_type=F32)
            if act == "qscale":
                acc = acc * (LOG2E * HEAD_DIM ** -0.5)
            elif act == "silu":
                acc = _silu(acc)
            elif act == "sigmoid":
                acc = _sigmoid(acc)
            elif act == "glu":
                half = (hi - lo) // 2
                acc = acc[:, :half] * _sigmoid(acc[:, half:])
            if layout == "pairs":
                for p in range(N_PAIRS):
                    o_ref[0, p, r, :] = acc[:, p * LANES:(p + 1) * LANES].astype(dtype)
            elif layout == "pairs_1":
                for p in range(N_PAIRS):
                    tile = acc[:, p * LANES:(p + 1) * LANES]
                    for head in (0, 1):
                        o_ref[0, p, head, r, :] = _pick(head, tile, jnp.ones_like(tile)).astype(dtype)
            elif layout == "pairs_t":
                for p in range(N_PAIRS):
                    for c in range(sub // KT_CHUNK):
                        piece = acc[c * KT_CHUNK:(c + 1) * KT_CHUNK, p * LANES:(p + 1) * LANES]
                        o_ref[0, p, h * (sub // KT_CHUNK) + c] = piece.T.astype(dtype)
            else:
                o_ref[0, r, :] = acc.astype(dtype)


def _inproj(x, mod_l, g_pre_l, w_in_bf, layer, sections, mod_row, tm):
    b, s, d = x.shape
    tm = min(tm, s)
    assert s % tm == 0 and tm % SUBLANES == 0
    in_w = w_in_bf.shape[-1]
    out_shapes, out_specs = [], []
    for (_, _, _, _, width, dtype, layout) in sections:
        if layout == "pairs":
            out_shapes.append(jax.ShapeDtypeStruct((b, N_PAIRS, s, LANES), dtype))
            out_specs.append(pl.BlockSpec((1, N_PAIRS, tm, LANES), lambda bi, i: (bi, 0, i, 0)))
        elif layout == "pairs_1":
            out_shapes.append(jax.ShapeDtypeStruct((b, N_PAIRS, 2, s, LANES), dtype))
            out_specs.append(pl.BlockSpec((1, N_PAIRS, 2, tm, LANES), lambda bi, i: (bi, 0, 0, i, 0)))
        elif layout == "pairs_t":
            assert tm % KT_CHUNK == 0 and (tm // max(1, min(INPROJ_MAX_GROUPS, tm // INPROJ_MIN_ROWS))) % KT_CHUNK == 0
            out_shapes.append(jax.ShapeDtypeStruct((b, N_PAIRS, s // KT_CHUNK, LANES, KT_CHUNK), dtype))
            out_specs.append(pl.BlockSpec((1, N_PAIRS, tm // KT_CHUNK, LANES, KT_CHUNK),
                                          lambda bi, i: (bi, 0, i, 0, 0)))
        else:
            out_shapes.append(jax.ShapeDtypeStruct((b, s, width), dtype))
            out_specs.append(pl.BlockSpec((1, tm, width), lambda bi, i: (bi, i, 0)))
    return pl.pallas_call(
        functools.partial(_inproj_kernel, sections=sections),
        grid=(b, s // tm),
        in_specs=[
            pl.BlockSpec((1, tm, d), lambda bi, i: (bi, i, 0)),
            pl.BlockSpec((1, 1, 3 * d), lambda bi, i: (mod_row(bi), 0, 0)),
            pl.BlockSpec((1, d), lambda bi, i: (0, 0)),
            pl.BlockSpec((None, d, in_w), lambda bi, i: (layer, 0, 0)),
        ],
        out_specs=out_specs,
        out_shape=out_shapes,
        compiler_params=_params(2),
        name="inproj",
    )(x, mod_l.reshape(MOD_ROWS, 1, 3 * d), g_pre_l.reshape(1, d), w_in_bf)


PAIR_STARTS = 2 * WIN_ROWS


def _bias_pair_tables(rpb):
    cols = np.arange(GRID_W)
    col_start = np.clip(cols - WIN_COLS // 2, 0, GRID_W - WIN_COLS)
    valid_c = (cols[None, :] >= col_start[:, None]) & (cols[None, :] < col_start[:, None] + WIN_COLS)
    dc = cols[None, :] - cols[:, None] + (WIN_COLS - 1)
    one_hot = (dc[None] == np.arange(2 * WIN_COLS - 1)[:, None, None]) & valid_c[None]
    per_dr = jnp.einsum("lhdp,pck->lhdck", rpb, jnp.asarray(one_hot, F32), precision=lax.Precision.HIGHEST)
    per_dr = jnp.where(valid_c, per_dr * LOG2E, MASKED)
    masked = jnp.full_like(per_dr[:, :, :1], MASKED)
    ext = jnp.concatenate([masked, per_dr, masked], axis=2)
    return jnp.concatenate([ext[:, :, :-1], ext[:, :, 1:]], axis=-1)


def _window_plan(i, ws, rows):
    kr = min(WIN_ROWS, rows)
    low = _head_masks()
    plan = []
    for ri in range(Q_ROWS):
        r = Q_ROWS * i + ri
        rs = jnp.clip(r - kr // 2, 0, rows - kr)
        for jj in range(K_ROWS // 2):
            krow = ws + 2 * jj
            ok_lo = (krow >= rs) & (krow < rs + kr)
            ok_hi = (krow + 1 >= rs) & (krow + 1 < rs + kr)
            ok = jnp.where(low, ok_lo.astype(jnp.int32), ok_hi.astype(jnp.int32)) > 0
            start = jnp.clip(krow - r + WIN_ROWS, 0, PAIR_STARTS - 1)
            plan.append((ok, start))
    return plan


def _window_bias(tab_ref, head, plan):
    n_jj = K_ROWS // 2
    row_blocks = []
    for ri in range(Q_ROWS):
        lane_blocks = [jnp.where(ok, tab_ref[head, start], MASKED) for ok, start in plan[ri * n_jj:(ri + 1) * n_jj]]
        row_blocks.append(jnp.concatenate(lane_blocks, axis=-1))
    return jnp.concatenate(row_blocks, axis=0)


def _head_masks():
    lane = lax.broadcasted_iota(jnp.int32, (1, LANES), 1)
    return lane < HEAD_DIM


def _pick(head, own, other):
    low = _head_masks()
    return jnp.where(low, own, other) if head == 0 else jnp.where(low, other, own)


def _pair_scores(q, keys, biases):
    out = []
    for head in (0, 1):
        dims = slice(head * HEAD_DIM, (head + 1) * HEAD_DIM)
        scores = [jnp.dot(q[:, dims], kt[dims, :], preferred_element_type=F32) for kt in keys]
        scores = [s if bias is None else s + bias for s, bias in zip(scores, biases[head])]
        m = functools.reduce(jnp.maximum, [jnp.max(s, axis=-1, keepdims=True) for s in scores])
        out.append((scores, m))
    return out


def _pv(head, scores, m, vals):
    o = functools.reduce(jnp.add, [
        jnp.dot(jnp.exp2(s - m).astype(BF16), v, preferred_element_type=F32) for s, v in zip(scores, vals)])
    return o / pltpu.roll(o, HEAD_DIM, axis=1)


ATTN_BLOCKS_PER_STEP = 4
ATTN_PAIRS_PER_STEP = 4


def _key_chunks(kt_ref, p, first, count):
    return jnp.concatenate([kt_ref[0, p, first + c] for c in range(count)], axis=-1)


def _attn_kernel(q_ref, kt_ref, v_ref, kct_ref, vc_ref, tab_ref, o_ref, *, rows, pairs):
    tq = Q_ROWS * GRID_W
    win_chunks = K_ROWS * GRID_W // KT_CHUNK
    ctx_chunks = kct_ref.shape[2]
    staged, starts = [], []
    for blk in range(ATTN_BLOCKS_PER_STEP):
        i = pl.program_id(2) * ATTN_BLOCKS_PER_STEP + blk
        ws = jnp.clip(Q_ROWS * i - min(WIN_ROWS, rows) // 2, 0, rows - K_ROWS)
        start = pl.multiple_of(ws * GRID_W, GRID_W)
        starts.append(start)
        plan = _window_plan(i, ws, rows)
        for p in range(pairs):
            kw = _key_chunks(kt_ref, p, ws * GRID_W // KT_CHUNK, win_chunks)
            kc = _key_chunks(kct_ref, p, 0, ctx_chunks)
            biases = [(_window_bias(tab_ref, 2 * p + head, plan), None) for head in (0, 1)]
            staged.extend(_pair_scores(q_ref[0, p, blk * tq:(blk + 1) * tq, :], (kw, kc), biases))
    outs = []
    for j, (scores, m) in enumerate(staged):
        blk, jj = divmod(j, 2 * pairs)
        p, head = divmod(jj, 2)
        vw = v_ref[0, p, head, pl.ds(starts[blk], K_ROWS * GRID_W), :]
        outs.append(_pv(head, scores, m, (vw, vc_ref[0, p, head])))
    for blk in range(ATTN_BLOCKS_PER_STEP):
        for p in range(pairs):
            j = blk * 2 * pairs + 2 * p
            o_ref[0, p, blk * tq:(blk + 1) * tq, :] = _pick(0, outs[j], outs[j + 1]).astype(o_ref.dtype)


def _attention(q, kt, v, kct, vc, tab):
    b, _, s, _ = q.shape
    l = vc.shape[3]
    rows = s // GRID_W
    n_blk = rows // Q_ROWS
    tq = Q_ROWS * GRID_W * ATTN_BLOCKS_PER_STEP
    pp = ATTN_PAIRS_PER_STEP
    assert s == rows * GRID_W and n_blk % ATTN_BLOCKS_PER_STEP == 0 and rows >= K_ROWS and N_PAIRS % pp == 0
    chunk_rows = KT_CHUNK // GRID_W
    assert Q_ROWS % chunk_rows == 0 and (min(WIN_ROWS, rows) // 2) % chunk_rows == 0
    assert (rows - K_ROWS) % chunk_rows == 0 and K_ROWS % chunk_rows == 0 and l % KT_CHUNK == 0

    return pl.pallas_call(
        functools.partial(_attn_kernel, rows=rows, pairs=pp),
        grid=(b, N_PAIRS // pp, n_blk // ATTN_BLOCKS_PER_STEP),
        in_specs=[
            pl.BlockSpec((1, pp, tq, LANES), lambda bi, p, i: (bi, p, i, 0)),
            pl.BlockSpec((1, pp, s // KT_CHUNK, LANES, KT_CHUNK), lambda bi, p, i: (bi, p, 0, 0, 0)),
            pl.BlockSpec((1, pp, 2, s, LANES), lambda bi, p, i: (bi, p, 0, 0, 0)),
            pl.BlockSpec((1, pp, l // KT_CHUNK, LANES, KT_CHUNK), lambda bi, p, i: (bi, p, 0, 0, 0)),
            pl.BlockSpec((1, pp, 2, l, LANES), lambda bi, p, i: (bi, p, 0, 0, 0)),
            pl.BlockSpec((2 * pp, PAIR_STARTS, GRID_W, LANES), lambda bi, p, i: (p, 0, 0, 0)),
        ],
        out_specs=pl.BlockSpec((1, pp, tq, LANES), lambda bi, p, i: (bi, p, i, 0)),
        out_shape=jax.ShapeDtypeStruct((b, N_PAIRS, s, LANES), BF16),
        compiler_params=_params(3),
        name="attention",
    )(q, kt, v, kct, vc, tab)


def _ctx_attn_kernel(q_ref, kt_ref, v_ref, o_ref):
    staged = []
    for p in range(N_PAIRS):
        keys = _key_chunks(kt_ref, p, 0, kt_ref.shape[2])
        staged.extend(_pair_scores(q_ref[0, p], (keys,), [(None,), (None,)]))
    outs = [_pv(j % 2, scores, m, (v_ref[0, j // 2, j % 2],)) for j, (scores, m) in enumerate(staged)]
    for p in range(N_PAIRS):
        o_ref[0, p] = _pick(0, outs[2 * p], outs[2 * p + 1]).astype(o_ref.dtype)


def _ctx_attention(q, kt, v):
    b, _, l, _ = q.shape
    spec = pl.BlockSpec((1, N_PAIRS, l, LANES), lambda bi: (bi, 0, 0, 0))
    kt_spec = pl.BlockSpec((1, N_PAIRS, l // KT_CHUNK, LANES, KT_CHUNK), lambda bi: (bi, 0, 0, 0, 0))
    return pl.pallas_call(
        _ctx_attn_kernel,
        grid=(b,),
        in_specs=[spec, kt_spec, pl.BlockSpec((1, N_PAIRS, 2, l, LANES), lambda bi: (bi, 0, 0, 0, 0))],
        out_specs=spec,
        out_shape=jax.ShapeDtypeStruct((b, N_PAIRS, l, LANES), BF16),
        compiler_params=_params(1),
        name="ctx_attention",
    )(q, kt, v)


def _dft_cos_sin(k, n, period):
    ang = (2.0 * np.pi / period) * ((k[:, None] * n[None, :]) % period)
    return np.cos(ang), np.sin(ang)


def _table(a):
    return jnp.asarray(a, F32).astype(BF16)


def _channel_dft():
    m = np.arange(FOURIER_GROUP_DIM)
    c, s = _dft_cos_sin(m, m, FOURIER_GROUP_DIM)
    eye = np.eye(FOURIER_GROUPS)
    scale = FOURIER_GROUP_DIM ** -0.5
    return np.concatenate([np.kron(eye, c.T), np.kron(eye, s.T)], axis=0) * scale


def _channel_weights(cs_ref, wf_ref):
    return jnp.dot(cs_ref[...], wf_ref[...], preferred_element_type=F32).astype(BF16)


def _channel_mix(xr, xi, cw):
    x = jnp.concatenate([xr, xi], axis=-1).astype(BF16)
    return jnp.dot(x, cw, preferred_element_type=F32)


FFT_GROUP = 8


def _fft_kernel(x_ref, w1_ref, m2_ref, cs_ref, wf_ref, pm_ref, o_ref, ar_ref, ai_ref, *, n1):
    g8 = FFT_GROUP
    for g in range(GRID_W // g8):
        xg = x_ref[0, :, g * g8:(g + 1) * g8, :].reshape(n1 * g8, FOURIER_W).astype(BF16)
        a = jnp.dot(w1_ref[...], xg, preferred_element_type=F32)
        ar_ref[:, g * g8:(g + 1) * g8, :] = a[:n1 * g8].reshape(n1, g8, FOURIER_W)
        ai_ref[:, g * g8:(g + 1) * g8, :] = a[n1 * g8:].reshape(n1, g8, FOURIER_W)
    for k1 in range(n1):
        a = jnp.concatenate([ar_ref[k1], ai_ref[k1]], axis=0).astype(BF16)
        x = jnp.dot(m2_ref[k1], a, preferred_element_type=F32)
        ar_ref[k1] = x[:GRID_W]
        ai_ref[k1] = x[GRID_W:]
    cw = _channel_weights(cs_ref, wf_ref)
    for g in range(n1 // g8):
        xr = ar_ref[g * g8:(g + 1) * g8].reshape(g8 * GRID_W, FOURIER_W)
        xi = ai_ref[g * g8:(g + 1) * g8].reshape(g8 * GRID_W, FOURIER_W)
        z = _channel_mix(xr, xi, cw).astype(BF16)
        o = jnp.dot(pm_ref[...], z, preferred_element_type=F32)
        o_ref[0, :, g * g8:(g + 1) * g8, :] = o.reshape(GRID_W, g8, FOURIER_W)


def _fourier(uf, wf_bf):
    b, s, fw = uf.shape
    n1 = s // GRID_W
    g8 = FFT_GROUP
    assert s == n1 * GRID_W and n1 % g8 == 0 and fw == FOURIER_W
    k1 = np.arange(n1)
    c1, s1 = _dft_cos_sin(k1, k1, n1)
    w1 = np.kron(np.concatenate([c1, -s1], axis=0) * n1 ** -0.5, np.eye(g8))
    n2 = np.arange(GRID_W)
    kk = k1[:, None] + n1 * np.arange(GRID_W)[None, :]
    ang = (2.0 * np.pi / s) * ((kk[:, :, None] * n2[None, None, :]) % s)
    c2, s2 = np.cos(ang), np.sin(ang)
    m2 = np.concatenate([np.concatenate([c2, s2], axis=2),
                         np.concatenate([-s2, c2], axis=2)], axis=1) * GRID_W ** -0.5
    perm = np.zeros((GRID_W * g8, g8 * GRID_W))
    k2g, k1g = np.meshgrid(np.arange(GRID_W), np.arange(g8), indexing="ij")
    perm[(k2g * g8 + k1g).ravel(), (k1g * GRID_W + k2g).ravel()] = 1.0
    out = pl.pallas_call(
        functools.partial(_fft_kernel, n1=n1),
        grid=(b,),
        in_specs=[
            pl.BlockSpec((1, n1, GRID_W, fw), lambda bi: (bi, 0, 0, 0)),
            pl.BlockSpec(w1.shape, lambda bi: (0, 0)),
            pl.BlockSpec((n1, 2 * GRID_W, 2 * GRID_W), lambda bi: (0, 0, 0)),
            pl.BlockSpec((2 * fw, fw), lambda bi: (0, 0)),
            pl.BlockSpec((fw, fw), lambda bi: (0, 0)),
            pl.BlockSpec(perm.shape, lambda bi: (0, 0)),
        ],
        out_specs=pl.BlockSpec((1, GRID_W, n1, fw), lambda bi: (bi, 0, 0, 0)),
        out_shape=jax.ShapeDtypeStruct((b, GRID_W, n1, fw), F32),
        scratch_shapes=[pltpu.VMEM((n1, GRID_W, fw), F32), pltpu.VMEM((n1, GRID_W, fw), F32)],
        compiler_params=_params(1),
        name="fourier",
    )(uf.reshape(b, n1, GRID_W, fw), _table(w1), _table(m2), _table(_channel_dft()), wf_bf, _table(perm))
    return out.reshape(b, s, fw)


def _ctx_fft_kernel(x_ref, wd_ref, cs_ref, wf_ref, o_ref):
    n = x_ref.shape[1]
    x = jnp.dot(wd_ref[...], x_ref[0].astype(BF16), preferred_element_type=F32)
    o_ref[0] = _channel_mix(x[:n], x[n:], _channel_weights(cs_ref, wf_ref))


def _ctx_fourier(uf, wf_bf):
    b, n, fw = uf.shape
    k = np.arange(n)
    c, s = _dft_cos_sin(k, k, n)
    wd = np.concatenate([c, -s], axis=0) * n ** -0.5
    return pl.pallas_call(
        _ctx_fft_kernel,
        grid=(b,),
        in_specs=[
            pl.BlockSpec((1, n, fw), lambda bi: (bi, 0, 0)),
            pl.BlockSpec((2 * n, n), lambda bi: (0, 0)),
            pl.BlockSpec((2 * fw, fw), lambda bi: (0, 0)),
            pl.BlockSpec((fw, fw), lambda bi: (0, 0)),
        ],
        out_specs=pl.BlockSpec((1, n, fw), lambda bi: (bi, 0, 0)),
        out_shape=jax.ShapeDtypeStruct((b, n, fw), F32),
        compiler_params=_params(1),
        name="ctx_fourier",
    )(uf, _table(wd), _table(_channel_dft()), wf_bf)


CONV_PAD = 16
CONV_TILE = 64
TAIL_CHUNK = 256
SUBLANES = 8


def _conv_tile(up_ref, base, dw_ref):
    first = CONV_PAD - CONV_K // 2
    halo = CONV_TILE + 2 * CONV_PAD
    win = up_ref[pl.ds(base, halo), :]
    acc = jnp.zeros((CONV_TILE, CONV_W), F32)
    for ph in range(SUBLANES):
        taps = [t for t in range(CONV_K) if (first + t) % SUBLANES == ph]
        if not taps:
            continue
        shifted = win if ph == 0 else pltpu.roll(win, halo - ph, axis=0)
        for t in taps:
            off = first + t - ph
            acc = acc + shifted[off:off + CONV_TILE] * dw_ref[t:t + 1, :]
    return acc


def _tail_kernel(x_ref, ao_ref, ga_ref, fw_ref, gf_ref, u_ref, gc_ref, sa_ref, sf_ref, sc_ref, mod_ref, gp_ref,
                 pa_ref, pf_ref, pc_ref, wo_ref, dw_ref, db_ref, lg_ref, lb_ref, pw_ref, o_ref, up_ref):
    d = x_ref.shape[-1]
    tm = x_ref.shape[1]
    n = u_ref.shape[1]
    i = pl.program_id(1)

    @pl.when(i == 0)
    def _():
        zeros = jnp.zeros((CONV_PAD, CONV_W), F32)
        up_ref[0:CONV_PAD, :] = zeros
        up_ref[CONV_PAD + n:CONV_PAD + n + CONV_PAD, :] = zeros
        up_ref[CONV_PAD:CONV_PAD + n, :] = u_ref[0].astype(F32)

    ao = jnp.concatenate([ao_ref[0, p] for p in range(N_PAIRS)], axis=-1)
    a_in = ao * ga_ref[0]
    f_in = (fw_ref[0] * gf_ref[0].astype(F32)).astype(BF16)
    n_pieces = tm // CONV_TILE
    n_chunks = d // TAIL_CHUNK
    per_dot = -(-n_pieces // (2 * n_chunks))
    conv_pieces, ya, yf = [], [], []

    def some_conv():
        for j in range(len(conv_pieces), min(len(conv_pieces) + per_dot, n_pieces)):
            conv_pieces.append(_conv_tile(up_ref, pl.multiple_of(i * tm + j * CONV_TILE, CONV_TILE), dw_ref))

    for c in range(n_chunks):
        cols = slice(c * TAIL_CHUNK, (c + 1) * TAIL_CHUNK)
        ya.append(jnp.dot(a_in, pa_ref[:, cols], preferred_element_type=F32))
        some_conv()
        yf.append(jnp.dot(f_in, pf_ref[:, cols], preferred_element_type=F32))
        some_conv()
    ya, yf = jnp.concatenate(ya, axis=-1), jnp.concatenate(yf, axis=-1)
    conv = jnp.concatenate(conv_pieces, axis=0) + db_ref[...]
    xc = conv - jnp.mean(conv, axis=-1, keepdims=True)
    var = jnp.mean(xc * xc, axis=-1, keepdims=True)
    cn = xc * lax.rsqrt(var + NORM_EPS) * lg_ref[...] + lb_ref[...]
    cw = jnp.dot(_silu(cn).astype(BF16), pw_ref[...], preferred_element_type=F32) * gc_ref[0].astype(F32)
    yc = jnp.dot(cw.astype(BF16), pc_ref[...], preferred_element_type=F32)
    merged = (sa_ref[0].astype(F32) * ya + sf_ref[0].astype(F32) * yf + sc_ref[0].astype(F32) * yc)
    y = jnp.dot(merged.astype(BF16), wo_ref[...], preferred_element_type=F32)
    yn = y * lax.rsqrt(jnp.mean(y * y, axis=-1, keepdims=True) + NORM_EPS) * gp_ref[...]
    gate = mod_ref[0][:, 2 * d:]
    o_ref[0] = x_ref[0] + gate * yn


def _tail(x, ao, ga, fw, gf, u, gc, sa, sf, sc, mod_l, g_post_l, pa_bf, pf_bf, pc_bf, wo_bf,
          conv_dw_l, conv_db_l, ln_g_l, ln_b_l, w_pw_bf, mod_row, tm):
    b, s, d = x.shape
    tm = min(tm, s)
    assert s % tm == 0 and tm % CONV_TILE == 0 and d % TAIL_CHUNK == 0

    def tok(width):
        return pl.BlockSpec((1, tm, width), lambda bi, i: (bi, i, 0))

    def full(shape):
        return pl.BlockSpec(shape, lambda bi, i: (0,) * len(shape))

    return pl.pallas_call(
        _tail_kernel,
        grid=(b, s // tm),
        in_specs=[
            tok(d),
            pl.BlockSpec((1, N_PAIRS, tm, LANES), lambda bi, i: (bi, 0, i, 0)),
            tok(ATTN_W), tok(FOURIER_W), tok(FOURIER_W),
            pl.BlockSpec((1, s, CONV_W), lambda bi, i: (bi, 0, 0)),
            tok(CONV_W), tok(d), tok(d), tok(d),
            pl.BlockSpec((1, 1, 3 * d), lambda bi, i: (mod_row(bi), 0, 0)),
            full((1, d)),
            full(pa_bf.shape), full(pf_bf.shape), full(pc_bf.shape), full(wo_bf.shape),
            full((CONV_K, CONV_W)), full((1, CONV_W)), full((1, CONV_W)), full((1, CONV_W)), full(w_pw_bf.shape),
        ],
        out_specs=tok(d),
        out_shape=jax.ShapeDtypeStruct((b, s, d), F32),
        scratch_shapes=[pltpu.VMEM((s + 2 * CONV_PAD, CONV_W), F32)],
        compiler_params=_params(2),
        name="tail",
    )(x, ao, ga, fw, gf, u, gc, sa, sf, sc, mod_l.reshape(MOD_ROWS, 1, 3 * d), g_post_l.reshape(1, d),
      pa_bf, pf_bf, pc_bf, wo_bf, conv_dw_l, conv_db_l.reshape(1, CONV_W), ln_g_l.reshape(1, CONV_W),
      ln_b_l.reshape(1, CONV_W), w_pw_bf)


INPROJ_TILE = 512
TAIL_TILE = 512


def kernel(x, c, ctx, c_ctx, w_mod, b_mod, g_pre, g_post, w_in, rpb, w_four, conv_dw, conv_db,
           conv_ln_g, conv_ln_b, w_pw, p_attn, p_four, p_conv, w_out):
    batch, seq, d = x.shape
    depth = w_mod.shape[0]
    rows = seq // GRID_W
    assert batch < MOD_ROWS and rows % (Q_ROWS * ATTN_BLOCKS_PER_STEP) == 0 and rows >= K_ROWS
    ctx_row = batch
    c_all = jnp.zeros((MOD_ROWS, d), F32).at[:batch].set(c).at[ctx_row].set(c_ctx)
    mod = _modulation(c_all, w_mod, b_mod)
    merge_sections = _merge_sections(d)
    bias_tabs = _bias_pair_tables(rpb)

    def latent_row(bi):
        return bi

    def context_row(bi):
        return ctx_row

    w_in_bf = w_in.astype(BF16)
    for l in range(depth):
        wf_bf, pw_bf = w_four[l].astype(BF16), w_pw[l].astype(BF16)
        pa_bf, pf_bf, pc_bf, wo_bf = (p_attn[l].astype(BF16), p_four[l].astype(BF16),
                                      p_conv[l].astype(BF16), w_out[l].astype(BF16))
        conv_w = (conv_dw[l], conv_db[l], conv_ln_g[l], conv_ln_b[l], pw_bf)
        tail_w = (mod[l], g_post[l], pa_bf, pf_bf, pc_bf, wo_bf)
        update_ctx = l < depth - 1

        if update_ctx:
            ctx_sections, ctx_w, ctx_layer = ALL_SECTIONS + merge_sections, w_in_bf, l
        else:
            lo, hi = KV_SECTIONS[0][1], KV_SECTIONS[-1][2]
            ctx_sections = tuple((sec[0], sec[1] - lo, sec[2] - lo) + sec[3:] for sec in KV_SECTIONS)
            ctx_w, ctx_layer = w_in_bf[l:l + 1, :, lo:hi], 0
        ctx_out = dict(zip([sec[0] for sec in ctx_sections],
                           _inproj(ctx, mod[l], g_pre[l], ctx_w, ctx_layer, ctx_sections, context_row,
                                   INPROJ_TILE)))

        lat = dict(zip([sec[0] for sec in ALL_SECTIONS + merge_sections],
                       _inproj(x, mod[l], g_pre[l], w_in_bf, l, ALL_SECTIONS + merge_sections, latent_row,
                               INPROJ_TILE)))
        ao = _attention(lat["q"], lat["k"], lat["v"], ctx_out["k"], ctx_out["v"], bias_tabs[l])
        fw = _fourier(lat["uf"], wf_bf)
        x = _tail(x, ao, lat["ga"], fw, lat["gf"], lat["u"], lat["gc"], lat["sa"], lat["sf"], lat["sc"],
                  *tail_w, *conv_w, latent_row, TAIL_TILE)

        if update_ctx:
            co = ctx_out
            ao_c = _ctx_attention(co["q"], co["k"], co["v"])
            fw_c = _ctx_fourier(co["uf"], wf_bf)
            ctx = _tail(ctx, ao_c, co["ga"], fw_c, co["gf"], co["u"], co["gc"], co["sa"], co["sf"], co["sc"],
                        *tail_w, *conv_w, context_row, TAIL_TILE)
    return x
```

```python
import functools

import jax
import jax.numpy as jnp
import numpy as np
from jax import lax
from jax.experimental import pallas as pl
from jax.experimental.pallas import tpu as pltpu

F32 = jnp.float32
BF16 = jnp.bfloat16

GRID_W = 64
N_HEADS = 8
HEAD_DIM = 64
ATTN_W = N_HEADS * HEAD_DIM
LANES = 128
N_PAIRS = ATTN_W // LANES
WIN_ROWS = 8
WIN_COLS = 16
FOURIER_GROUPS = 4
FOURIER_GROUP_DIM = 64
FOURIER_W = FOURIER_GROUPS * FOURIER_GROUP_DIM
CONV_W = 256
CONV_K = 31
NORM_EPS = 1e-6
MASKED = -1e30
LOG2E = 1.4426950408889634

Q_ROWS = 2
K_ROWS = 10
MOD_ROWS = 8

VMEM_LIMIT = 56 * 1024 * 1024


def _sigmoid(x):
    return 1.0 / (1.0 + jnp.exp(-x))


def _silu(x):
    return x * _sigmoid(x)


def _params(n_axes):
    return pltpu.CompilerParams(
        dimension_semantics=("arbitrary",) * n_axes, vmem_limit_bytes=VMEM_LIMIT)


def _mod_kernel(c_ref, w_ref, b_ref, o_ref):
    sc = _silu(c_ref[...]).astype(BF16)
    o_ref[0] = jnp.dot(sc, w_ref[0].astype(BF16), preferred_element_type=F32) + b_ref[0]


def _modulation(c_all, w_mod, b_mod):
    depth, d, d3 = w_mod.shape
    tn = d
    return pl.pallas_call(
        _mod_kernel,
        grid=(depth, d3 // tn),
        in_specs=[
            pl.BlockSpec((MOD_ROWS, d), lambda l, j: (0, 0)),
            pl.BlockSpec((1, d, tn), lambda l, j: (l, 0, j)),
            pl.BlockSpec((1, 1, tn), lambda l, j: (l, 0, j)),
        ],
        out_specs=pl.BlockSpec((1, MOD_ROWS, tn), lambda l, j: (l, 0, j)),
        out_shape=jax.ShapeDtypeStruct((depth, MOD_ROWS, d3), F32),
        compiler_params=_params(2),
        name="modulation",
    )(c_all, w_mod, b_mod.reshape(depth, 1, d3))


KT_CHUNK = LANES
_SEC_Q = ("q", 0, ATTN_W, "qscale", ATTN_W, BF16, "pairs")
_SEC_K = ("k", ATTN_W, 2 * ATTN_W, None, ATTN_W, BF16, "pairs_t")
_SEC_V = ("v", 2 * ATTN_W, 3 * ATTN_W, None, ATTN_W, BF16, "pairs_1")
_O = 4 * ATTN_W
_SEC_REST = (
    ("ga", 3 * ATTN_W, _O, "silu", ATTN_W, BF16, "flat"),
    ("uf", _O, _O + FOURIER_W, None, FOURIER_W, F32, "flat"),
    ("gf", _O + FOURIER_W, _O + 2 * FOURIER_W, "silu", FOURIER_W, BF16, "flat"),
    ("u", _O + 2 * FOURIER_W, _O + 2 * FOURIER_W + 2 * CONV_W, "glu", CONV_W, BF16, "flat"),
    ("gc", _O + 2 * FOURIER_W + 2 * CONV_W, _O + 2 * FOURIER_W + 3 * CONV_W, "silu", CONV_W, BF16, "flat"),
)
_S0 = _O + 2 * FOURIER_W + 3 * CONV_W
ALL_SECTIONS = (_SEC_Q, _SEC_K, _SEC_V) + _SEC_REST
KV_SECTIONS = (_SEC_K, _SEC_V)


def _merge_sections(d):
    return tuple((n, _S0 + i * d, _S0 + (i + 1) * d, "sigmoid", d, BF16, "flat")
                 for i, n in enumerate(("sa", "sf", "sc")))


INPROJ_MIN_ROWS = 128
INPROJ_MAX_GROUPS = 4


def _inproj_kernel(x_ref, mod_ref, g_ref, w_ref, *o_refs, sections):
    tm, d = x_ref.shape[1], x_ref.shape[2]
    m = mod_ref[0]
    shift, scale = m[:, :d], m[:, d:2 * d]
    epilogue_rank = {"sigmoid": 0, "glu": 1, "silu": 2, "qscale": 3, None: 4}
    order = sorted(range(len(sections)), key=lambda j: epilogue_rank[sections[j][3]])
    groups = max(1, min(INPROJ_MAX_GROUPS, tm // INPROJ_MIN_ROWS))
    sub = tm // groups
    for h in range(groups):
        r = slice(h * sub, (h + 1) * sub)
        x = x_ref[0, r, :]
        y = x * lax.rsqrt(jnp.mean(x * x, axis=-1, keepdims=True) + NORM_EPS) * g_ref[...]
        hb = (y * (1.0 + scale) + shift).astype(BF16)
        for (_, lo, hi, act, _, dtype, layout), o_ref in [(sections[j], o_refs[j]) for j in order]:
            acc = jnp.dot(hb, w_ref[:, lo:hi], preferred_element_type=F32)
            if act == "qscale":
                acc = acc * (LOG2E * HEAD_DIM ** -0.5)
            elif act == "silu":
                acc = _silu(acc)
            elif act == "sigmoid":
                acc = _sigmoid(acc)
            elif act == "glu":
                half = (hi - lo) // 2
                acc = acc[:, :half] * _sigmoid(acc[:, half:])
            if layout == "pairs":
                for p in range(N_PAIRS):
                    o_ref[0, p, r, :] = acc[:, p * LANES:(p + 1) * LANES].astype(dtype)
            elif layout == "pairs_1":
                for p in range(N_PAIRS):
                    tile = acc[:, p * LANES:(p + 1) * LANES]
                    for head in (0, 1):
                        o_ref[0, p, head, r, :] = _pick(head, tile, jnp.ones_like(tile)).astype(dtype)
            elif layout == "pairs_t":
                for p in range(N_PAIRS):
                    for c in range(sub // KT_CHUNK):
                        piece = acc[c * KT_CHUNK:(c + 1) * KT_CHUNK, p * LANES:(p + 1) * LANES]
                        o_ref[0, p, h * (sub // KT_CHUNK) + c] = piece.T.astype(dtype)
            else:
                o_ref[0, r, :] = acc.astype(dtype)


def _inproj(x, mod_l, g_pre_l, w_in_bf, layer, sections, mod_row, tm):
    b, s, d = x.shape
    tm = min(tm, s)
    assert s % tm == 0 and tm % SUBLANES == 0
    in_w = w_in_bf.shape[-1]
    out_shapes, out_specs = [], []
    for (_, _, _, _, width, dtype, layout) in sections:
        if layout == "pairs":
            out_shapes.append(jax.ShapeDtypeStruct((b, N_PAIRS, s, LANES), dtype))
            out_specs.append(pl.BlockSpec((1, N_PAIRS, tm, LANES), lambda bi, i: (bi, 0, i, 0)))
        elif layout == "pairs_1":
            out_shapes.append(jax.ShapeDtypeStruct((b, N_PAIRS, 2, s, LANES), dtype))
            out_specs.append(pl.BlockSpec((1, N_PAIRS, 2, tm, LANES), lambda bi, i: (bi, 0, 0, i, 0)))
        elif layout == "pairs_t":
            assert tm % KT_CHUNK == 0 and (tm // max(1, min(INPROJ_MAX_GROUPS, tm // INPROJ_MIN_ROWS))) % KT_CHUNK == 0
            out_shapes.append(jax.ShapeDtypeStruct((b, N_PAIRS, s // KT_CHUNK, LANES, KT_CHUNK), dtype))
            out_specs.append(pl.BlockSpec((1, N_PAIRS, tm // KT_CHUNK, LANES, KT_CHUNK),
                                          lambda bi, i: (bi, 0, i, 0, 0)))
        else:
            out_shapes.append(jax.ShapeDtypeStruct((b, s, width), dtype))
            out_specs.append(pl.BlockSpec((1, tm, width), lambda bi, i: (bi, i, 0)))
    return pl.pallas_call(
        functools.partial(_inproj_kernel, sections=sections),
        grid=(b, s // tm),
        in_specs=[
            pl.BlockSpec((1, tm, d), lambda bi, i: (bi, i, 0)),
            pl.BlockSpec((1, 1, 3 * d), lambda bi, i: (mod_row(bi), 0, 0)),
            pl.BlockSpec((1, d), lambda bi, i: (0, 0)),
            pl.BlockSpec((None, d, in_w), lambda bi, i: (layer, 0, 0)),
        ],
        out_specs=out_specs,
        out_shape=out_shapes,
        compiler_params=_params(2),
        name="inproj",
    )(x, mod_l.reshape(MOD_ROWS, 1, 3 * d), g_pre_l.reshape(1, d), w_in_bf)


PAIR_STARTS = 2 * WIN_ROWS


def _bias_pair_tables(rpb):
    cols = np.arange(GRID_W)
    col_start = np.clip(cols - WIN_COLS // 2, 0, GRID_W - WIN_COLS)
    valid_c = (cols[None, :] >= col_start[:, None]) & (cols[None, :] < col_start[:, None] + WIN_COLS)
    dc = cols[None, :] - cols[:, None] + (WIN_COLS - 1)
    one_hot = (dc[None] == np.arange(2 * WIN_COLS - 1)[:, None, None]) & valid_c[None]
    per_dr = jnp.einsum("lhdp,pck->lhdck", rpb, jnp.asarray(one_hot, F32), precision=lax.Precision.HIGHEST)
    per_dr = jnp.where(valid_c, per_dr * LOG2E, MASKED)
    masked = jnp.full_like(per_dr[:, :, :1], MASKED)
    ext = jnp.concatenate([masked, per_dr, masked], axis=2)
    return jnp.concatenate([ext[:, :, :-1], ext[:, :, 1:]], axis=-1)


def _always_valid(rows):
    kr = min(WIN_ROWS, rows)
    valid = np.ones((Q_ROWS, K_ROWS // 2), bool)
    for i in range(rows // Q_ROWS):
        ws = int(np.clip(Q_ROWS * i - kr // 2, 0, rows - K_ROWS))
        for ri in range(Q_ROWS):
            rs = int(np.clip(Q_ROWS * i + ri - kr // 2, 0, rows - kr))
            for jj in range(K_ROWS // 2):
                valid[ri, jj] &= rs <= ws + 2 * jj and ws + 2 * jj + 1 < rs + kr
    return valid


def _window_plan(i, ws, rows):
    kr = min(WIN_ROWS, rows)
    low = _head_masks()
    always = _always_valid(rows)
    plan = []
    for ri in range(Q_ROWS):
        r = Q_ROWS * i + ri
        rs = jnp.clip(r - kr // 2, 0, rows - kr)
        for jj in range(K_ROWS // 2):
            krow = ws + 2 * jj
            start = jnp.clip(krow - r + WIN_ROWS, 0, PAIR_STARTS - 1)
            if always[ri, jj]:
                plan.append((None, start))
                continue
            ok_lo = (krow >= rs) & (krow < rs + kr)
            ok_hi = (krow + 1 >= rs) & (krow + 1 < rs + kr)
            ok = jnp.where(low, ok_lo.astype(jnp.int32), ok_hi.astype(jnp.int32)) > 0
            plan.append((ok, start))
    return plan


def _window_bias(tab_ref, head, plan):
    n_jj = K_ROWS // 2
    row_blocks = []
    for ri in range(Q_ROWS):
        lane_blocks = [tab_ref[head, start] if ok is None else jnp.where(ok, tab_ref[head, start], MASKED)
                       for ok, start in plan[ri * n_jj:(ri + 1) * n_jj]]
        row_blocks.append(jnp.concatenate(lane_blocks, axis=-1))
    return jnp.concatenate(row_blocks, axis=0)


def _head_masks():
    lane = lax.broadcasted_iota(jnp.int32, (1, LANES), 1)
    return lane < HEAD_DIM


def _pick(head, own, other):
    low = _head_masks()
    return jnp.where(low, own, other) if head == 0 else jnp.where(low, other, own)


def _pair_scores(q, keys, biases):
    out = []
    for head in (0, 1):
        dims = slice(head * HEAD_DIM, (head + 1) * HEAD_DIM)
        scores = [jnp.dot(q[:, dims], kt[dims, :], preferred_element_type=F32) for kt in keys]
        scores = [s if bias is None else s + bias for s, bias in zip(scores, biases[head])]
        m = functools.reduce(jnp.maximum, [jnp.max(s, axis=-1, keepdims=True) for s in scores])
        out.append((scores, m))
    return out


def _pv(head, scores, m, vals):
    o = functools.reduce(jnp.add, [
        jnp.dot(jnp.exp2(s - m).astype(BF16), v, preferred_element_type=F32) for s, v in zip(scores, vals)])
    return o / pltpu.roll(o, HEAD_DIM, axis=1)


ATTN_BLOCKS_PER_STEP = 4
ATTN_PAIRS_PER_STEP = 4


def _key_chunks(kt_ref, p, first, count):
    return jnp.concatenate([kt_ref[0, p, first + c] for c in range(count)], axis=-1)


def _attn_kernel(q_ref, kt_ref, v_ref, kct_ref, vc_ref, tab_ref, o_ref, *, rows, pairs):
    tq = Q_ROWS * GRID_W
    win_chunks = K_ROWS * GRID_W // KT_CHUNK
    ctx_chunks = kct_ref.shape[2]
    staged, starts = [], []
    for blk in range(ATTN_BLOCKS_PER_STEP):
        i = pl.program_id(2) * ATTN_BLOCKS_PER_STEP + blk
        ws = jnp.clip(Q_ROWS * i - min(WIN_ROWS, rows) // 2, 0, rows - K_ROWS)
        start = pl.multiple_of(ws * GRID_W, GRID_W)
        starts.append(start)
        plan = _window_plan(i, ws, rows)
        for p in range(pairs):
            kw = _key_chunks(kt_ref, p, ws * GRID_W // KT_CHUNK, win_chunks)
            kc = _key_chunks(kct_ref, p, 0, ctx_chunks)
            biases = [(_window_bias(tab_ref, 2 * p + head, plan), None) for head in (0, 1)]
            staged.extend(_pair_scores(q_ref[0, p, blk * tq:(blk + 1) * tq, :], (kw, kc), biases))
    outs = []
    for j, (scores, m) in enumerate(staged):
        blk, jj = divmod(j, 2 * pairs)
        p, head = divmod(jj, 2)
        vw = v_ref[0, p, head, pl.ds(starts[blk], K_ROWS * GRID_W), :]
        outs.append(_pv(head, scores, m, (vw, vc_ref[0, p, head])))
    for blk in range(ATTN_BLOCKS_PER_STEP):
        for p in range(pairs):
            j = blk * 2 * pairs + 2 * p
            o_ref[0, p, blk * tq:(blk + 1) * tq, :] = _pick(0, outs[j], outs[j + 1]).astype(o_ref.dtype)


def _attention(q, kt, v, kct, vc, tab):
    b, _, s, _ = q.shape
    l = vc.shape[3]
    rows = s // GRID_W
    n_blk = rows // Q_ROWS
    tq = Q_ROWS * GRID_W * ATTN_BLOCKS_PER_STEP
    pp = ATTN_PAIRS_PER_STEP
    assert s == rows * GRID_W and n_blk % ATTN_BLOCKS_PER_STEP == 0 and rows >= K_ROWS and N_PAIRS % pp == 0
    chunk_rows = KT_CHUNK // GRID_W
    assert Q_ROWS % chunk_rows == 0 and (min(WIN_ROWS, rows) // 2) % chunk_rows == 0
    assert (rows - K_ROWS) % chunk_rows == 0 and K_ROWS % chunk_rows == 0 and l % KT_CHUNK == 0

    return pl.pallas_call(
        functools.partial(_attn_kernel, rows=rows, pairs=pp),
        grid=(b, N_PAIRS // pp, n_blk // ATTN_BLOCKS_PER_STEP),
        in_specs=[
            pl.BlockSpec((1, pp, tq, LANES), lambda bi, p, i: (bi, p, i, 0)),
            pl.BlockSpec((1, pp, s // KT_CHUNK, LANES, KT_CHUNK), lambda bi, p, i: (bi, p, 0, 0, 0)),
            pl.BlockSpec((1, pp, 2, s, LANES), lambda bi, p, i: (bi, p, 0, 0, 0)),
            pl.BlockSpec((1, pp, l // KT_CHUNK, LANES, KT_CHUNK), lambda bi, p, i: (bi, p, 0, 0, 0)),
            pl.BlockSpec((1, pp, 2, l, LANES), lambda bi, p, i: (bi, p, 0, 0, 0)),
            pl.BlockSpec((2 * pp, PAIR_STARTS, GRID_W, LANES), lambda bi, p, i: (p, 0, 0, 0)),
        ],
        out_specs=pl.BlockSpec((1, pp, tq, LANES), lambda bi, p, i: (bi, p, i, 0)),
        out_shape=jax.ShapeDtypeStruct((b, N_PAIRS, s, LANES), BF16),
        compiler_params=_params(3),
        name="attention",
    )(q, kt, v, kct, vc, tab)


def _ctx_attn_kernel(q_ref, kt_ref, v_ref, o_ref):
    staged = []
    for p in range(N_PAIRS):
        keys = _key_chunks(kt_ref, p, 0, kt_ref.shape[2])
        staged.extend(_pair_scores(q_ref[0, p], (keys,), [(None,), (None,)]))
    outs = [_pv(j % 2, scores, m, (v_ref[0, j // 2, j % 2],)) for j, (scores, m) in enumerate(staged)]
    for p in range(N_PAIRS):
        o_ref[0, p] = _pick(0, outs[2 * p], outs[2 * p + 1]).astype(o_ref.dtype)


def _ctx_attention(q, kt, v):
    b, _, l, _ = q.shape
    spec = pl.BlockSpec((1, N_PAIRS, l, LANES), lambda bi: (bi, 0, 0, 0))
    kt_spec = pl.BlockSpec((1, N_PAIRS, l // KT_CHUNK, LANES, KT_CHUNK), lambda bi: (bi, 0, 0, 0, 0))
    return pl.pallas_call(
        _ctx_attn_kernel,
        grid=(b,),
        in_specs=[spec, kt_spec, pl.BlockSpec((1, N_PAIRS, 2, l, LANES), lambda bi: (bi, 0, 0, 0, 0))],
        out_specs=spec,
        out_shape=jax.ShapeDtypeStruct((b, N_PAIRS, l, LANES), BF16),
        compiler_params=_params(1),
        name="ctx_attention",
    )(q, kt, v)


def _dft_cos_sin(k, n, period):
    ang = (2.0 * np.pi / period) * ((k[:, None] * n[None, :]) % period)
    return np.cos(ang), np.sin(ang)


def _table(a):
    return jnp.asarray(a, F32).astype(BF16)


def _channel_dft():
    m = np.arange(FOURIER_GROUP_DIM)
    c, s = _dft_cos_sin(m, m, FOURIER_GROUP_DIM)
    eye = np.eye(FOURIER_GROUPS)
    scale = FOURIER_GROUP_DIM ** -0.5
    return np.concatenate([np.kron(eye, c.T), np.kron(eye, s.T)], axis=0) * scale


def _channel_weights(cs_ref, wf_ref):
    return jnp.dot(cs_ref[...], wf_ref[...], preferred_element_type=F32).astype(BF16)


def _channel_mix(xr, xi, cw):
    x = jnp.concatenate([xr, xi], axis=-1).astype(BF16)
    return jnp.dot(x, cw, preferred_element_type=F32)


FFT_GROUP = 8


def _fft_kernel(x_ref, w1_ref, m2_ref, cs_ref, wf_ref, pm_ref, o_ref, ar_ref, ai_ref, *, n1):
    g8 = FFT_GROUP
    for g in range(GRID_W // g8):
        xg = x_ref[0, :, g * g8:(g + 1) * g8, :].reshape(n1 * g8, FOURIER_W).astype(BF16)
        a = jnp.dot(w1_ref[...], xg, preferred_element_type=F32)
        ar_ref[:, g * g8:(g + 1) * g8, :] = a[:n1 * g8].reshape(n1, g8, FOURIER_W)
        ai_ref[:, g * g8:(g + 1) * g8, :] = a[n1 * g8:].reshape(n1, g8, FOURIER_W)
    for k1 in range(n1):
        a = jnp.concatenate([ar_ref[k1], ai_ref[k1]], axis=0).astype(BF16)
        x = jnp.dot(m2_ref[k1], a, preferred_element_type=F32)
        ar_ref[k1] = x[:GRID_W]
        ai_ref[k1] = x[GRID_W:]
    cw = _channel_weights(cs_ref, wf_ref)
    for g in range(n1 // g8):
        xr = ar_ref[g * g8:(g + 1) * g8].reshape(g8 * GRID_W, FOURIER_W)
        xi = ai_ref[g * g8:(g + 1) * g8].reshape(g8 * GRID_W, FOURIER_W)
        z = _channel_mix(xr, xi, cw).astype(BF16)
        o = jnp.dot(pm_ref[...], z, preferred_element_type=F32)
        o_ref[0, :, g * g8:(g + 1) * g8, :] = o.reshape(GRID_W, g8, FOURIER_W)


def _fourier(uf, wf_bf):
    b, s, fw = uf.shape
    n1 = s // GRID_W
    g8 = FFT_GROUP
    assert s == n1 * GRID_W and n1 % g8 == 0 and fw == FOURIER_W
    k1 = np.arange(n1)
    c1, s1 = _dft_cos_sin(k1, k1, n1)
    w1 = np.kron(np.concatenate([c1, -s1], axis=0) * n1 ** -0.5, np.eye(g8))
    n2 = np.arange(GRID_W)
    kk = k1[:, None] + n1 * np.arange(GRID_W)[None, :]
    ang = (2.0 * np.pi / s) * ((kk[:, :, None] * n2[None, None, :]) % s)
    c2, s2 = np.cos(ang), np.sin(ang)
    m2 = np.concatenate([np.concatenate([c2, s2], axis=2),
                         np.concatenate([-s2, c2], axis=2)], axis=1) * GRID_W ** -0.5
    perm = np.zeros((GRID_W * g8, g8 * GRID_W))
    k2g, k1g = np.meshgrid(np.arange(GRID_W), np.arange(g8), indexing="ij")
    perm[(k2g * g8 + k1g).ravel(), (k1g * GRID_W + k2g).ravel()] = 1.0
    out = pl.pallas_call(
        functools.partial(_fft_kernel, n1=n1),
        grid=(b,),
        in_specs=[
            pl.BlockSpec((1, n1, GRID_W, fw), lambda bi: (bi, 0, 0, 0)),
            pl.BlockSpec(w1.shape, lambda bi: (0, 0)),
            pl.BlockSpec((n1, 2 * GRID_W, 2 * GRID_W), lambda bi: (0, 0, 0)),
            pl.BlockSpec((2 * fw, fw), lambda bi: (0, 0)),
            pl.BlockSpec((fw, fw), lambda bi: (0, 0)),
            pl.BlockSpec(perm.shape, lambda bi: (0, 0)),
        ],
        out_specs=pl.BlockSpec((1, GRID_W, n1, fw), lambda bi: (bi, 0, 0, 0)),
        out_shape=jax.ShapeDtypeStruct((b, GRID_W, n1, fw), F32),
        scratch_shapes=[pltpu.VMEM((n1, GRID_W, fw), F32), pltpu.VMEM((n1, GRID_W, fw), F32)],
        compiler_params=_params(1),
        name="fourier",
    )(uf.reshape(b, n1, GRID_W, fw), _table(w1), _table(m2), _table(_channel_dft()), wf_bf, _table(perm))
    return out.reshape(b, s, fw)


def _ctx_fft_kernel(x_ref, wd_ref, cs_ref, wf_ref, o_ref):
    n = x_ref.shape[1]
    x = jnp.dot(wd_ref[...], x_ref[0].astype(BF16), preferred_element_type=F32)
    o_ref[0] = _channel_mix(x[:n], x[n:], _channel_weights(cs_ref, wf_ref))


def _ctx_fourier(uf, wf_bf):
    b, n, fw = uf.shape
    k = np.arange(n)
    c, s = _dft_cos_sin(k, k, n)
    wd = np.concatenate([c, -s], axis=0) * n ** -0.5
    return pl.pallas_call(
        _ctx_fft_kernel,
        grid=(b,),
        in_specs=[
            pl.BlockSpec((1, n, fw), lambda bi: (bi, 0, 0)),
            pl.BlockSpec((2 * n, n), lambda bi: (0, 0)),
            pl.BlockSpec((2 * fw, fw), lambda bi: (0, 0)),
            pl.BlockSpec((fw, fw), lambda bi: (0, 0)),
        ],
        out_specs=pl.BlockSpec((1, n, fw), lambda bi: (bi, 0, 0)),
        out_shape=jax.ShapeDtypeStruct((b, n, fw), F32),
        compiler_params=_params(1),
        name="ctx_fourier",
    )(uf, _table(wd), _table(_channel_dft()), wf_bf)


CONV_PAD = 16
CONV_TILE = 64
TAIL_CHUNK = 256
SUBLANES = 8


def _conv_tile(up_ref, base, dw_ref):
    first = CONV_PAD - CONV_K // 2
    halo = CONV_TILE + 2 * CONV_PAD
    win = up_ref[pl.ds(base, halo), :]
    acc = jnp.zeros((CONV_TILE, CONV_W), F32)
    for ph in range(SUBLANES):
        taps = [t for t in range(CONV_K) if (first + t) % SUBLANES == ph]
        if not taps:
            continue
        shifted = win if ph == 0 else pltpu.roll(win, halo - ph, axis=0)
        for t in taps:
            off = first + t - ph
            acc = acc + shifted[off:off + CONV_TILE] * dw_ref[t:t + 1, :]
    return acc


def _tail_kernel(x_ref, ao_ref, ga_ref, fw_ref, gf_ref, u_ref, gc_ref, sa_ref, sf_ref, sc_ref, mod_ref, gp_ref,
                 pa_ref, pf_ref, pc_ref, wo_ref, dw_ref, db_ref, lg_ref, lb_ref, pw_ref, o_ref, up_ref):
    d = x_ref.shape[-1]
    tm = x_ref.shape[1]
    n = u_ref.shape[1]
    i = pl.program_id(1)

    @pl.when(i == 0)
    def _():
        zeros = jnp.zeros((CONV_PAD, CONV_W), F32)
        up_ref[0:CONV_PAD, :] = zeros
        up_ref[CONV_PAD + n:CONV_PAD + n + CONV_PAD, :] = zeros
        up_ref[CONV_PAD:CONV_PAD + n, :] = u_ref[0].astype(F32)

    ao = jnp.concatenate([ao_ref[0, p] for p in range(N_PAIRS)], axis=-1)
    a_in = ao * ga_ref[0]
    f_in = (fw_ref[0] * gf_ref[0].astype(F32)).astype(BF16)
    n_pieces = tm // CONV_TILE
    n_chunks = d // TAIL_CHUNK
    per_dot = -(-n_pieces // (2 * n_chunks))
    conv_pieces, ya, yf = [], [], []

    def some_conv():
        for j in range(len(conv_pieces), min(len(conv_pieces) + per_dot, n_pieces)):
            conv_pieces.append(_conv_tile(up_ref, pl.multiple_of(i * tm + j * CONV_TILE, CONV_TILE), dw_ref))

    for c in range(n_chunks):
        cols = slice(c * TAIL_CHUNK, (c + 1) * TAIL_CHUNK)
        ya.append(jnp.dot(a_in, pa_ref[:, cols], preferred_element_type=F32))
        some_conv()
        yf.append(jnp.dot(f_in, pf_ref[:, cols], preferred_element_type=F32))
        some_conv()
    ya, yf = jnp.concatenate(ya, axis=-1), jnp.concatenate(yf, axis=-1)
    conv = jnp.concatenate(conv_pieces, axis=0) + db_ref[...]
    xc = conv - jnp.mean(conv, axis=-1, keepdims=True)
    var = jnp.mean(xc * xc, axis=-1, keepdims=True)
    cn = xc * lax.rsqrt(var + NORM_EPS) * lg_ref[...] + lb_ref[...]
    cw = jnp.dot(_silu(cn).astype(BF16), pw_ref[...], preferred_element_type=F32) * gc_ref[0].astype(F32)
    yc = jnp.dot(cw.astype(BF16), pc_ref[...], preferred_element_type=F32)
    merged = (sa_ref[0].astype(F32) * ya + sf_ref[0].astype(F32) * yf + sc_ref[0].astype(F32) * yc)
    y = jnp.dot(merged.astype(BF16), wo_ref[...], preferred_element_type=F32)
    yn = y * lax.rsqrt(jnp.mean(y * y, axis=-1, keepdims=True) + NORM_EPS) * gp_ref[...]
    gate = mod_ref[0][:, 2 * d:]
    o_ref[0] = x_ref[0] + gate * yn


def _tail(x, ao, ga, fw, gf, u, gc, sa, sf, sc, mod_l, g_post_l, pa_bf, pf_bf, pc_bf, wo_bf,
          conv_dw_l, conv_db_l, ln_g_l, ln_b_l, w_pw_bf, mod_row, tm):
    b, s, d = x.shape
    tm = min(tm, s)
    assert s % tm == 0 and tm % CONV_TILE == 0 and d % TAIL_CHUNK == 0

    def tok(width):
        return pl.BlockSpec((1, tm, width), lambda bi, i: (bi, i, 0))

    def full(shape):
        return pl.BlockSpec(shape, lambda bi, i: (0,) * len(shape))

    return pl.pallas_call(
        _tail_kernel,
        grid=(b, s // tm),
        in_specs=[
            tok(d),
            pl.BlockSpec((1, N_PAIRS, tm, LANES), lambda bi, i: (bi, 0, i, 0)),
            tok(ATTN_W), tok(FOURIER_W), tok(FOURIER_W),
            pl.BlockSpec((1, s, CONV_W), lambda bi, i: (bi, 0, 0)),
            tok(CONV_W), tok(d), tok(d), tok(d),
            pl.BlockSpec((1, 1, 3 * d), lambda bi, i: (mod_row(bi), 0, 0)),
            full((1, d)),
            full(pa_bf.shape), full(pf_bf.shape), full(pc_bf.shape), full(wo_bf.shape),
            full((CONV_K, CONV_W)), full((1, CONV_W)), full((1, CONV_W)), full((1, CONV_W)), full(w_pw_bf.shape),
        ],
        out_specs=tok(d),
        out_shape=jax.ShapeDtypeStruct((b, s, d), F32),
        scratch_shapes=[pltpu.VMEM((s + 2 * CONV_PAD, CONV_W), F32)],
        compiler_params=_params(2),
        name="tail",
    )(x, ao, ga, fw, gf, u, gc, sa, sf, sc, mod_l.reshape(MOD_ROWS, 1, 3 * d), g_post_l.reshape(1, d),
      pa_bf, pf_bf, pc_bf, wo_bf, conv_dw_l, conv_db_l.reshape(1, CONV_W), ln_g_l.reshape(1, CONV_W),
      ln_b_l.reshape(1, CONV_W), w_pw_bf)


INPROJ_TILE = 512
TAIL_TILE = 512


def kernel(x, c, ctx, c_ctx, w_mod, b_mod, g_pre, g_post, w_in, rpb, w_four, conv_dw, conv_db,
           conv_ln_g, conv_ln_b, w_pw, p_attn, p_four, p_conv, w_out):
    batch, seq, d = x.shape
    depth = w_mod.shape[0]
    rows = seq // GRID_W
    assert batch < MOD_ROWS and rows % (Q_ROWS * ATTN_BLOCKS_PER_STEP) == 0 and rows >= K_ROWS
    ctx_row = batch
    c_all = jnp.zeros((MOD_ROWS, d), F32).at[:batch].set(c).at[ctx_row].set(c_ctx)
    mod = _modulation(c_all, w_mod, b_mod)
    merge_sections = _merge_sections(d)
    bias_tabs = _bias_pair_tables(rpb)

    def latent_row(bi):
        return bi

    def context_row(bi):
        return ctx_row

    w_in_bf = w_in.astype(BF16)
    for l in range(depth):
        wf_bf, pw_bf = w_four[l].astype(BF16), w_pw[l].astype(BF16)
        pa_bf, pf_bf, pc_bf, wo_bf = (p_attn[l].astype(BF16), p_four[l].astype(BF16),
                                      p_conv[l].astype(BF16), w_out[l].astype(BF16))
        conv_w = (conv_dw[l], conv_db[l], conv_ln_g[l], conv_ln_b[l], pw_bf)
        tail_w = (mod[l], g_post[l], pa_bf, pf_bf, pc_bf, wo_bf)
        update_ctx = l < depth - 1

        if update_ctx:
            ctx_sections, ctx_w, ctx_layer = ALL_SECTIONS + merge_sections, w_in_bf, l
        else:
            lo, hi = KV_SECTIONS[0][1], KV_SECTIONS[-1][2]
            ctx_sections = tuple((sec[0], sec[1] - lo, sec[2] - lo) + sec[3:] for sec in KV_SECTIONS)
            ctx_w, ctx_layer = w_in_bf[l:l + 1, :, lo:hi], 0
        ctx_out = dict(zip([sec[0] for sec in ctx_sections],
                           _inproj(ctx, mod[l], g_pre[l], ctx_w, ctx_layer, ctx_sections, context_row,
                                   INPROJ_TILE)))

        lat = dict(zip([sec[0] for sec in ALL_SECTIONS + merge_sections],
                       _inproj(x, mod[l], g_pre[l], w_in_bf, l, ALL_SECTIONS + merge_sections, latent_row,
                               INPROJ_TILE)))
        ao = _attention(lat["q"], lat["k"], lat["v"], ctx_out["k"], ctx_out["v"], bias_tabs[l])
        fw = _fourier(lat["uf"], wf_bf)
        x = _tail(x, ao, lat["ga"], fw, lat["gf"], lat["u"], lat["gc"], lat["sa"], lat["sf"], lat["sc"],
                  *tail_w, *conv_w, latent_row, TAIL_TILE)

        if update_ctx:
            co = ctx_out
            ao_c = _ctx_attention(co["q"], co["k"], co["v"])
            fw_c = _ctx_fourier(co["uf"], wf_bf)
            ctx = _tail(ctx, ao_c, co["ga"], fw_c, co["gf"], co["u"], co["gc"], co["sa"], co["sf"], co["sc"],
                        *tail_w, *conv_w, context_row, TAIL_TILE)
    return x
```

```python
import functools

import jax
import jax.numpy as jnp
import numpy as np
from jax import lax
from jax.experimental import pallas as pl
from jax.experimental.pallas import tpu as pltpu

F32 = jnp.float32
BF16 = jnp.bfloat16

GRID_W = 64
N_HEADS = 8
HEAD_DIM = 64
ATTN_W = N_HEADS * HEAD_DIM
LANES = 128
N_PAIRS = ATTN_W // LANES
WIN_ROWS = 8
WIN_COLS = 16
FOURIER_GROUPS = 4
FOURIER_GROUP_DIM = 64
FOURIER_W = FOURIER_GROUPS * FOURIER_GROUP_DIM
CONV_W = 256
CONV_K = 31
NORM_EPS = 1e-6
MASKED = -1e30
LOG2E = 1.4426950408889634

Q_ROWS = 2
K_ROWS = 10
MOD_ROWS = 8

VMEM_LIMIT = 56 * 1024 * 1024


def _sigmoid(x):
    return 1.0 / (1.0 + jnp.exp(-x))


def _silu(x):
    return x * _sigmoid(x)


def _params(n_axes):
    return pltpu.CompilerParams(
        dimension_semantics=("arbitrary",) * n_axes, vmem_limit_bytes=VMEM_LIMIT)


def _mod_kernel(c_ref, w_ref, b_ref, o_ref):
    sc = _silu(c_ref[...]).astype(BF16)
    o_ref[0] = jnp.dot(sc, w_ref[0].astype(BF16), preferred_element_type=F32) + b_ref[0]


def _modulation(c_all, w_mod, b_mod):
    depth, d, d3 = w_mod.shape
    tn = d
    return pl.pallas_call(
        _mod_kernel,
        grid=(depth, d3 // tn),
        in_specs=[
            pl.BlockSpec((MOD_ROWS, d), lambda l, j: (0, 0)),
            pl.BlockSpec((1, d, tn), lambda l, j: (l, 0, j)),
            pl.BlockSpec((1, 1, tn), lambda l, j: (l, 0, j)),
        ],
        out_specs=pl.BlockSpec((1, MOD_ROWS, tn), lambda l, j: (l, 0, j)),
        out_shape=jax.ShapeDtypeStruct((depth, MOD_ROWS, d3), F32),
        compiler_params=_params(2),
        name="modulation",
    )(c_all, w_mod, b_mod.reshape(depth, 1, d3))


KT_CHUNK = LANES
_SEC_Q = ("q", 0, ATTN_W, "qscale", ATTN_W, BF16, "pairs")
_SEC_K = ("k", ATTN_W, 2 * ATTN_W, None, ATTN_W, BF16, "pairs_t")
_SEC_V = ("v", 2 * ATTN_W, 3 * ATTN_W, None, ATTN_W, BF16, "pairs_1")
_O = 4 * ATTN_W
_SEC_REST = (
    ("ga", 3 * ATTN_W, _O, "silu", ATTN_W, BF16, "flat"),
    ("uf", _O, _O + FOURIER_W, None, FOURIER_W, F32, "flat"),
    ("gf", _O + FOURIER_W, _O + 2 * FOURIER_W, "silu", FOURIER_W, BF16, "flat"),
    ("u", _O + 2 * FOURIER_W, _O + 2 * FOURIER_W + 2 * CONV_W, "glu", CONV_W, BF16, "flat"),
    ("gc", _O + 2 * FOURIER_W + 2 * CONV_W, _O + 2 * FOURIER_W + 3 * CONV_W, "silu", CONV_W, BF16, "flat"),
)
_S0 = _O + 2 * FOURIER_W + 3 * CONV_W
ALL_SECTIONS = (_SEC_Q, _SEC_K, _SEC_V) + _SEC_REST
KV_SECTIONS = (_SEC_K, _SEC_V)


def _merge_sections(d):
    return tuple((n, _S0 + i * d, _S0 + (i + 1) * d, "sigmoid", d, BF16, "flat")
                 for i, n in enumerate(("sa", "sf", "sc")))


INPROJ_MIN_ROWS = 128
INPROJ_MAX_GROUPS = 4


def _inproj_kernel(x_ref, mod_ref, g_ref, w_ref, *o_refs, sections):
    tm, d = x_ref.shape[1], x_ref.shape[2]
    m = mod_ref[0]
    shift, scale = m[:, :d], m[:, d:2 * d]
    epilogue_rank = {"sigmoid": 0, "glu": 1, "silu": 2, "qscale": 3, None: 4}
    order = sorted(range(len(sections)), key=lambda j: epilogue_rank[sections[j][3]])
    groups = max(1, min(INPROJ_MAX_GROUPS, tm // INPROJ_MIN_ROWS))
    sub = tm // groups
    for h in range(groups):
        r = slice(h * sub, (h + 1) * sub)
        x = x_ref[0, r, :]
        y = x * lax.rsqrt(jnp.mean(x * x, axis=-1, keepdims=True) + NORM_EPS) * g_ref[...]
        hb = (y * (1.0 + scale) + shift).astype(BF16)
        for (_, lo, hi, act, _, dtype, layout), o_ref in [(sections[j], o_refs[j]) for j in order]:
            acc = jnp.dot(hb, w_ref[:, lo:hi], preferred_element_type=F32)
            if act == "qscale":
                acc = acc * (LOG2E * HEAD_DIM ** -0.5)
            elif act == "silu":
                acc = _silu(acc)
            elif act == "sigmoid":
                acc = _sigmoid(acc)
            elif act == "glu":
                half = (hi - lo) // 2
                acc = acc[:, :half] * _sigmoid(acc[:, half:])
            if layout == "pairs":
                for p in range(N_PAIRS):
                    o_ref[0, p, r, :] = acc[:, p * LANES:(p + 1) * LANES].astype(dtype)
            elif layout == "pairs_1":
                for p in range(N_PAIRS):
                    tile = acc[:, p * LANES:(p + 1) * LANES]
                    for head in (0, 1):
                        o_ref[0, p, head, r, :] = _pick(head, tile, jnp.ones_like(tile)).astype(dtype)
            elif layout == "pairs_t":
                for p in range(N_PAIRS):
                    for c in range(sub // KT_CHUNK):
                        piece = acc[c * KT_CHUNK:(c + 1) * KT_CHUNK, p * LANES:(p + 1) * LANES]
                        o_ref[0, p, h * (sub // KT_CHUNK) + c] = piece.T.astype(dtype)
            else:
                o_ref[0, r, :] = acc.astype(dtype)


def _inproj(x, mod_l, g_pre_l, w_in_bf, layer, sections, mod_row, tm):
    b, s, d = x.shape
    tm = min(tm, s)
    assert s % tm == 0 and tm % SUBLANES == 0
    in_w = w_in_bf.shape[-1]
    out_shapes, out_specs = [], []
    for (_, _, _, _, width, dtype, layout) in sections:
        if layout == "pairs":
            out_shapes.append(jax.ShapeDtypeStruct((b, N_PAIRS, s, LANES), dtype))
            out_specs.append(pl.BlockSpec((1, N_PAIRS, tm, LANES), lambda bi, i: (bi, 0, i, 0)))
        elif layout == "pairs_1":
            out_shapes.append(jax.ShapeDtypeStruct((b, N_PAIRS, 2, s, LANES), dtype))
            out_specs.append(pl.BlockSpec((1, N_PAIRS, 2, tm, LANES), lambda bi, i: (bi, 0, 0, i, 0)))
        elif layout == "pairs_t":
            assert tm % KT_CHUNK == 0 and (tm // max(1, min(INPROJ_MAX_GROUPS, tm // INPROJ_MIN_ROWS))) % KT_CHUNK == 0
            out_shapes.append(jax.ShapeDtypeStruct((b, N_PAIRS, s // KT_CHUNK, LANES, KT_CHUNK), dtype))
            out_specs.append(pl.BlockSpec((1, N_PAIRS, tm // KT_CHUNK, LANES, KT_CHUNK),
                                          lambda bi, i: (bi, 0, i, 0, 0)))
        else:
            out_shapes.append(jax.ShapeDtypeStruct((b, s, width), dtype))
            out_specs.append(pl.BlockSpec((1, tm, width), lambda bi, i: (bi, i, 0)))
    return pl.pallas_call(
        functools.partial(_inproj_kernel, sections=sections),
        grid=(b, s // tm),
        in_specs=[
            pl.BlockSpec((1, tm, d), lambda bi, i: (bi, i, 0)),
            pl.BlockSpec((1, 1, 3 * d), lambda bi, i: (mod_row(bi), 0, 0)),
            pl.BlockSpec((1, d), lambda bi, i: (0, 0)),
            pl.BlockSpec((None, d, in_w), lambda bi, i: (layer, 0, 0)),
        ],
        out_specs=out_specs,
        out_shape=out_shapes,
        compiler_params=_params(2),
        name="inproj",
    )(x, mod_l.reshape(MOD_ROWS, 1, 3 * d), g_pre_l.reshape(1, d), w_in_bf)


PAIR_STARTS = 2 * WIN_ROWS


def _bias_pair_tables(rpb):
    cols = np.arange(GRID_W)
    col_start = np.clip(cols - WIN_COLS // 2, 0, GRID_W - WIN_COLS)
    valid_c = (cols[None, :] >= col_start[:, None]) & (cols[None, :] < col_start[:, None] + WIN_COLS)
    dc = cols[None, :] - cols[:, None] + (WIN_COLS - 1)
    one_hot = (dc[None] == np.arange(2 * WIN_COLS - 1)[:, None, None]) & valid_c[None]
    per_dr = jnp.einsum("lhdp,pck->lhdck", rpb, jnp.asarray(one_hot, F32), precision=lax.Precision.HIGHEST)
    per_dr = jnp.where(valid_c, per_dr * LOG2E, MASKED)
    masked = jnp.full_like(per_dr[:, :, :1], MASKED)
    ext = jnp.concatenate([masked, per_dr, masked], axis=2)
    return jnp.concatenate([ext[:, :, :-1], ext[:, :, 1:]], axis=-1)


def _always_valid(rows):
    kr = min(WIN_ROWS, rows)
    valid = np.ones((Q_ROWS, K_ROWS // 2), bool)
    for i in range(rows // Q_ROWS):
        ws = int(np.clip(Q_ROWS * i - kr // 2, 0, rows - K_ROWS))
        for ri in range(Q_ROWS):
            rs = int(np.clip(Q_ROWS * i + ri - kr // 2, 0, rows - kr))
            for jj in range(K_ROWS // 2):
                valid[ri, jj] &= rs <= ws + 2 * jj and ws + 2 * jj + 1 < rs + kr
    return valid


def _window_plan(i, ws, rows):
    kr = min(WIN_ROWS, rows)
    low = _head_masks()
    always = _always_valid(rows)
    plan = []
    for ri in range(Q_ROWS):
        r = Q_ROWS * i + ri
        rs = jnp.clip(r - kr // 2, 0, rows - kr)
        for jj in range(K_ROWS // 2):
            krow = ws + 2 * jj
            start = jnp.clip(krow - r + WIN_ROWS, 0, PAIR_STARTS - 1)
            if always[ri, jj]:
                plan.append((None, start))
                continue
            ok_lo = (krow >= rs) & (krow < rs + kr)
            ok_hi = (krow + 1 >= rs) & (krow + 1 < rs + kr)
            ok = jnp.where(low, ok_lo.astype(jnp.int32), ok_hi.astype(jnp.int32)) > 0
            plan.append((ok, start))
    return plan


def _window_bias(tab_ref, head, plan):
    n_jj = K_ROWS // 2
    row_blocks = []
    for ri in range(Q_ROWS):
        lane_blocks = [tab_ref[head, start] if ok is None else jnp.where(ok, tab_ref[head, start], MASKED)
                       for ok, start in plan[ri * n_jj:(ri + 1) * n_jj]]
        row_blocks.append(jnp.concatenate(lane_blocks, axis=-1))
    return jnp.concatenate(row_blocks, axis=0)


def _head_masks():
    lane = lax.broadcasted_iota(jnp.int32, (1, LANES), 1)
    return lane < HEAD_DIM


def _pick(head, own, other):
    low = _head_masks()
    return jnp.where(low, own, other) if head == 0 else jnp.where(low, other, own)


def _pair_scores(q, keys, biases):
    out = []
    for head in (0, 1):
        dims = slice(head * HEAD_DIM, (head + 1) * HEAD_DIM)
        scores = [jnp.dot(q[:, dims], kt[dims, :], preferred_element_type=F32) for kt in keys]
        scores = [s if bias is None else s + bias for s, bias in zip(scores, biases[head])]
        m = functools.reduce(jnp.maximum, [jnp.max(s, axis=-1, keepdims=True) for s in scores])
        out.append((scores, m))
    return out


def _pv(head, scores, m, vals):
    o = functools.reduce(jnp.add, [
        jnp.dot(jnp.exp2(s - m).astype(BF16), v, preferred_element_type=F32) for s, v in zip(scores, vals)])
    return o / pltpu.roll(o, HEAD_DIM, axis=1)


ATTN_BLOCKS_PER_STEP = 4
ATTN_PAIRS_PER_STEP = 4


def _key_chunks(kt_ref, p, first, count):
    return jnp.concatenate([kt_ref[0, p, first + c] for c in range(count)], axis=-1)


def _attn_kernel(q_ref, kt_ref, v_ref, kct_ref, vc_ref, tab_ref, o_ref, *, rows, pairs):
    tq = Q_ROWS * GRID_W
    win_chunks = K_ROWS * GRID_W // KT_CHUNK
    ctx_chunks = kct_ref.shape[2]
    staged, starts = [], []
    for blk in range(ATTN_BLOCKS_PER_STEP):
        i = pl.program_id(2) * ATTN_BLOCKS_PER_STEP + blk
        ws = jnp.clip(Q_ROWS * i - min(WIN_ROWS, rows) // 2, 0, rows - K_ROWS)
        start = pl.multiple_of(ws * GRID_W, GRID_W)
        starts.append(start)
        plan = _window_plan(i, ws, rows)
        for p in range(pairs):
            kw = _key_chunks(kt_ref, p, ws * GRID_W // KT_CHUNK, win_chunks)
            kc = _key_chunks(kct_ref, p, 0, ctx_chunks)
            biases = [(_window_bias(tab_ref, 2 * p + head, plan), None) for head in (0, 1)]
            staged.extend(_pair_scores(q_ref[0, p, blk * tq:(blk + 1) * tq, :], (kw, kc), biases))
    outs = []
    for j, (scores, m) in enumerate(staged):
        blk, jj = divmod(j, 2 * pairs)
        p, head = divmod(jj, 2)
        vw = v_ref[0, p, head, pl.ds(starts[blk], K_ROWS * GRID_W), :]
        outs.append(_pv(head, scores, m, (vw, vc_ref[0, p, head])))
    for blk in range(ATTN_BLOCKS_PER_STEP):
        for p in range(pairs):
            j = blk * 2 * pairs + 2 * p
            o_ref[0, p, blk * tq:(blk + 1) * tq, :] = _pick(0, outs[j], outs[j + 1]).astype(o_ref.dtype)


def _attention(q, kt, v, kct, vc, tab):
    b, _, s, _ = q.shape
    l = vc.shape[3]
    rows = s // GRID_W
    n_blk = rows // Q_ROWS
    tq = Q_ROWS * GRID_W * ATTN_BLOCKS_PER_STEP
    pp = ATTN_PAIRS_PER_STEP
    assert s == rows * GRID_W and n_blk % ATTN_BLOCKS_PER_STEP == 0 and rows >= K_ROWS and N_PAIRS % pp == 0
    chunk_rows = KT_CHUNK // GRID_W
    assert Q_ROWS % chunk_rows == 0 and (min(WIN_ROWS, rows) // 2) % chunk_rows == 0
    assert (rows - K_ROWS) % chunk_rows == 0 and K_ROWS % chunk_rows == 0 and l % KT_CHUNK == 0

    return pl.pallas_call(
        functools.partial(_attn_kernel, rows=rows, pairs=pp),
        grid=(b, N_PAIRS // pp, n_blk // ATTN_BLOCKS_PER_STEP),
        in_specs=[
            pl.BlockSpec((1, pp, tq, LANES), lambda bi, p, i: (bi, p, i, 0)),
            pl.BlockSpec((1, pp, s // KT_CHUNK, LANES, KT_CHUNK), lambda bi, p, i: (bi, p, 0, 0, 0)),
            pl.BlockSpec((1, pp, 2, s, LANES), lambda bi, p, i: (bi, p, 0, 0, 0)),
            pl.BlockSpec((1, pp, l // KT_CHUNK, LANES, KT_CHUNK), lambda bi, p, i: (bi, p, 0, 0, 0)),
            pl.BlockSpec((1, pp, 2, l, LANES), lambda bi, p, i: (bi, p, 0, 0, 0)),
            pl.BlockSpec((2 * pp, PAIR_STARTS, GRID_W, LANES), lambda bi, p, i: (p, 0, 0, 0)),
        ],
        out_specs=pl.BlockSpec((1, pp, tq, LANES), lambda bi, p, i: (bi, p, i, 0)),
        out_shape=jax.ShapeDtypeStruct((b, N_PAIRS, s, LANES), BF16),
        compiler_params=_params(3),
        name="attention",
    )(q, kt, v, kct, vc, tab)


def _ctx_attn_kernel(q_ref, kt_ref, v_ref, o_ref):
    staged = []
    for p in range(N_PAIRS):
        keys = _key_chunks(kt_ref, p, 0, kt_ref.shape[2])
        staged.extend(_pair_scores(q_ref[0, p], (keys,), [(None,), (None,)]))
    outs = [_pv(j % 2, scores, m, (v_ref[0, j // 2, j % 2],)) for j, (scores, m) in enumerate(staged)]
    for p in range(N_PAIRS):
        o_ref[0, p] = _pick(0, outs[2 * p], outs[2 * p + 1]).astype(o_ref.dtype)


def _ctx_attention(q, kt, v):
    b, _, l, _ = q.shape
    spec = pl.BlockSpec((1, N_PAIRS, l, LANES), lambda bi: (bi, 0, 0, 0))
    kt_spec = pl.BlockSpec((1, N_PAIRS, l // KT_CHUNK, LANES, KT_CHUNK), lambda bi: (bi, 0, 0, 0, 0))
    return pl.pallas_call(
        _ctx_attn_kernel,
        grid=(b,),
        in_specs=[spec, kt_spec, pl.BlockSpec((1, N_PAIRS, 2, l, LANES), lambda bi: (bi, 0, 0, 0, 0))],
        out_specs=spec,
        out_shape=jax.ShapeDtypeStruct((b, N_PAIRS, l, LANES), BF16),
        compiler_params=_params(1),
        name="ctx_attention",
    )(q, kt, v)


def _dft_cos_sin(k, n, period):
    ang = (2.0 * np.pi / period) * ((k[:, None] * n[None, :]) % period)
    return np.cos(ang), np.sin(ang)


def _table(a):
    return jnp.asarray(a, F32).astype(BF16)


def _channel_dft():
    m = np.arange(FOURIER_GROUP_DIM)
    c, s = _dft_cos_sin(m, m, FOURIER_GROUP_DIM)
    eye = np.eye(FOURIER_GROUPS)
    scale = FOURIER_GROUP_DIM ** -0.5
    return np.concatenate([np.kron(eye, c.T), np.kron(eye, s.T)], axis=0) * scale


def _channel_weights(cs_ref, wf_ref):
    return jnp.dot(cs_ref[...], wf_ref[...], preferred_element_type=F32).astype(BF16)


def _channel_mix(xr, xi, cw):
    x = jnp.concatenate([xr, xi], axis=-1).astype(BF16)
    return jnp.dot(x, cw, preferred_element_type=F32)


FFT_GROUP = 8


def _fft_kernel(x_ref, w1_ref, m2_ref, cs_ref, wf_ref, pm_ref, o_ref, ar_ref, ai_ref, *, n1):
    g8 = FFT_GROUP
    for g in range(GRID_W // g8):
        xg = x_ref[0, :, g * g8:(g + 1) * g8, :].reshape(n1 * g8, FOURIER_W).astype(BF16)
        a = jnp.dot(w1_ref[...], xg, preferred_element_type=F32)
        ar_ref[:, g * g8:(g + 1) * g8, :] = a[:n1 * g8].reshape(n1, g8, FOURIER_W)
        ai_ref[:, g * g8:(g + 1) * g8, :] = a[n1 * g8:].reshape(n1, g8, FOURIER_W)
    for k1 in range(n1):
        a = jnp.concatenate([ar_ref[k1], ai_ref[k1]], axis=0).astype(BF16)
        x = jnp.dot(m2_ref[k1], a, preferred_element_type=F32)
        ar_ref[k1] = x[:GRID_W]
        ai_ref[k1] = x[GRID_W:]
    cw = _channel_weights(cs_ref, wf_ref)
    for g in range(n1 // g8):
        xr = ar_ref[g * g8:(g + 1) * g8].reshape(g8 * GRID_W, FOURIER_W)
        xi = ai_ref[g * g8:(g + 1) * g8].reshape(g8 * GRID_W, FOURIER_W)
        z = _channel_mix(xr, xi, cw).astype(BF16)
        o = jnp.dot(pm_ref[...], z, preferred_element_type=F32)
        o_ref[0, :, g * g8:(g + 1) * g8, :] = o.reshape(GRID_W, g8, FOURIER_W)


def _fourier(uf, wf_bf):
    b, s, fw = uf.shape
    n1 = s // GRID_W
    g8 = FFT_GROUP
    assert s == n1 * GRID_W and n1 % g8 == 0 and fw == FOURIER_W
    k1 = np.arange(n1)
    c1, s1 = _dft_cos_sin(k1, k1, n1)
    w1 = np.kron(np.concatenate([c1, -s1], axis=0) * n1 ** -0.5, np.eye(g8))
    n2 = np.arange(GRID_W)
    kk = k1[:, None] + n1 * np.arange(GRID_W)[None, :]
    ang = (2.0 * np.pi / s) * ((kk[:, :, None] * n2[None, None, :]) % s)
    c2, s2 = np.cos(ang), np.sin(ang)
    m2 = np.concatenate([np.concatenate([c2, s2], axis=2),
                         np.concatenate([-s2, c2], axis=2)], axis=1) * GRID_W ** -0.5
    perm = np.zeros((GRID_W * g8, g8 * GRID_W))
    k2g, k1g = np.meshgrid(np.arange(GRID_W), np.arange(g8), indexing="ij")
    perm[(k2g * g8 + k1g).ravel(), (k1g * GRID_W + k2g).ravel()] = 1.0
    out = pl.pallas_call(
        functools.partial(_fft_kernel, n1=n1),
        grid=(b,),
        in_specs=[
            pl.BlockSpec((1, n1, GRID_W, fw), lambda bi: (bi, 0, 0, 0)),
            pl.BlockSpec(w1.shape, lambda bi: (0, 0)),
            pl.BlockSpec((n1, 2 * GRID_W, 2 * GRID_W), lambda bi: (0, 0, 0)),
            pl.BlockSpec((2 * fw, fw), lambda bi: (0, 0)),
            pl.BlockSpec((fw, fw), lambda bi: (0, 0)),
            pl.BlockSpec(perm.shape, lambda bi: (0, 0)),
        ],
        out_specs=pl.BlockSpec((1, GRID_W, n1, fw), lambda bi: (bi, 0, 0, 0)),
        out_shape=jax.ShapeDtypeStruct((b, GRID_W, n1, fw), F32),
        scratch_shapes=[pltpu.VMEM((n1, GRID_W, fw), F32), pltpu.VMEM((n1, GRID_W, fw), F32)],
        compiler_params=_params(1),
        name="fourier",
    )(uf.reshape(b, n1, GRID_W, fw), _table(w1), _table(m2), _table(_channel_dft()), wf_bf, _table(perm))
    return out.reshape(b, s, fw)


def _ctx_fft_kernel(x_ref, wd_ref, cs_ref, wf_ref, o_ref):
    n = x_ref.shape[1]
    x = jnp.dot(wd_ref[...], x_ref[0].astype(BF16), preferred_element_type=F32)
    o_ref[0] = _channel_mix(x[:n], x[n:], _channel_weights(cs_ref, wf_ref))


def _ctx_fourier(uf, wf_bf):
    b, n, fw = uf.shape
    k = np.arange(n)
    c, s = _dft_cos_sin(k, k, n)
    wd = np.concatenate([c, -s], axis=0) * n ** -0.5
    return pl.pallas_call(
        _ctx_fft_kernel,
        grid=(b,),
        in_specs=[
            pl.BlockSpec((1, n, fw), lambda bi: (bi, 0, 0)),
            pl.BlockSpec((2 * n, n), lambda bi: (0, 0)),
            pl.BlockSpec((2 * fw, fw), lambda bi: (0, 0)),
            pl.BlockSpec((fw, fw), lambda bi: (0, 0)),
        ],
        out_specs=pl.BlockSpec((1, n, fw), lambda bi: (bi, 0, 0)),
        out_shape=jax.ShapeDtypeStruct((b, n, fw), F32),
        compiler_params=_params(1),
        name="ctx_fourier",
    )(uf, _table(wd), _table(_channel_dft()), wf_bf)


CONV_PAD = 16
CONV_TILE = 64
TAIL_CHUNK = 256
SUBLANES = 8


def _conv_tile(up_ref, base, dw_ref):
    first = CONV_PAD - CONV_K // 2
    halo = CONV_TILE + 2 * CONV_PAD
    win = up_ref[pl.ds(base, halo), :]
    acc = jnp.zeros((CONV_TILE, CONV_W), F32)
    for ph in range(SUBLANES):
        taps = [t for t in range(CONV_K) if (first + t) % SUBLANES == ph]
        if not taps:
            continue
        shifted = win if ph == 0 else pltpu.roll(win, halo - ph, axis=0)
        for t in taps:
            off = first + t - ph
            acc = acc + shifted[off:off + CONV_TILE] * dw_ref[t:t + 1, :]
    return acc


def _tail_kernel(x_ref, ao_ref, ga_ref, fw_ref, gf_ref, u_ref, gc_ref, sa_ref, sf_ref, sc_ref, mod_ref, gp_ref,
                 pa_ref, pf_ref, pc_ref, wo_ref, dw_ref, db_ref, lg_ref, lb_ref, pw_ref, o_ref, up_ref):
    d = x_ref.shape[-1]
    tm = x_ref.shape[1]
    n = u_ref.shape[1]
    i = pl.program_id(1)

    @pl.when(i == 0)
    def _():
        zeros = jnp.zeros((CONV_PAD, CONV_W), F32)
        up_ref[0:CONV_PAD, :] = zeros
        up_ref[CONV_PAD + n:CONV_PAD + n + CONV_PAD, :] = zeros
        up_ref[CONV_PAD:CONV_PAD + n, :] = u_ref[0].astype(F32)

    ao = jnp.concatenate([ao_ref[0, p] for p in range(N_PAIRS)], axis=-1)
    a_in = ao * ga_ref[0]
    f_in = (fw_ref[0] * gf_ref[0].astype(F32)).astype(BF16)
    n_pieces = tm // CONV_TILE
    n_chunks = d // TAIL_CHUNK
    per_dot = -(-n_pieces // (2 * n_chunks))
    conv_pieces, ya, yf = [], [], []

    def some_conv():
        for j in range(len(conv_pieces), min(len(conv_pieces) + per_dot, n_pieces)):
            conv_pieces.append(_conv_tile(up_ref, pl.multiple_of(i * tm + j * CONV_TILE, CONV_TILE), dw_ref))

    for c in range(n_chunks):
        cols = slice(c * TAIL_CHUNK, (c + 1) * TAIL_CHUNK)
        ya.append(jnp.dot(a_in, pa_ref[:, cols], preferred_element_type=F32))
        some_conv()
        yf.append(jnp.dot(f_in, pf_ref[:, cols], preferred_element_type=F32))
        some_conv()
    ya, yf = jnp.concatenate(ya, axis=-1), jnp.concatenate(yf, axis=-1)
    conv = jnp.concatenate(conv_pieces, axis=0) + db_ref[...]
    xc = conv - jnp.mean(conv, axis=-1, keepdims=True)
    var = jnp.mean(xc * xc, axis=-1, keepdims=True)
    cn = xc * lax.rsqrt(var + NORM_EPS) * lg_ref[...] + lb_ref[...]
    cw = jnp.dot(_silu(cn).astype(BF16), pw_ref[...], preferred_element_type=F32) * gc_ref[0].astype(F32)
    yc = jnp.dot(cw.astype(BF16), pc_ref[...], preferred_element_type=F32)
    merged = (sa_ref[0].astype(F32) * ya + sf_ref[0].astype(F32) * yf + sc_ref[0].astype(F32) * yc)
    y = jnp.dot(merged.astype(BF16), wo_ref[...], preferred_element_type=F32)
    yn = y * lax.rsqrt(jnp.mean(y * y, axis=-1, keepdims=True) + NORM_EPS) * gp_ref[...]
    gate = mod_ref[0][:, 2 * d:]
    o_ref[0] = x_ref[0] + gate * yn


def _tail(x, ao, ga, fw, gf, u, gc, sa, sf, sc, mod_l, g_post_l, pa_bf, pf_bf, pc_bf, wo_bf,
          conv_dw_l, conv_db_l, ln_g_l, ln_b_l, w_pw_bf, mod_row, tm):
    b, s, d = x.shape
    tm = min(tm, s)
    assert s % tm == 0 and tm % CONV_TILE == 0 and d % TAIL_CHUNK == 0

    def tok(width):
        return pl.BlockSpec((1, tm, width), lambda bi, i: (bi, i, 0))

    def full(shape):
        return pl.BlockSpec(shape, lambda bi, i: (0,) * len(shape))

    return pl.pallas_call(
        _tail_kernel,
        grid=(b, s // tm),
        in_specs=[
            tok(d),
            pl.BlockSpec((1, N_PAIRS, tm, LANES), lambda bi, i: (bi, 0, i, 0)),
            tok(ATTN_W), tok(FOURIER_W), tok(FOURIER_W),
            pl.BlockSpec((1, s, CONV_W), lambda bi, i: (bi, 0, 0)),
            tok(CONV_W), tok(d), tok(d), tok(d),
            pl.BlockSpec((1, 1, 3 * d), lambda bi, i: (mod_row(bi), 0, 0)),
            full((1, d)),
            full(pa_bf.shape), full(pf_bf.shape), full(pc_bf.shape), full(wo_bf.shape),
            full((CONV_K, CONV_W)), full((1, CONV_W)), full((1, CONV_W)), full((1, CONV_W)), full(w_pw_bf.shape),
        ],
        out_specs=tok(d),
        out_shape=jax.ShapeDtypeStruct((b, s, d), F32),
        scratch_shapes=[pltpu.VMEM((s + 2 * CONV_PAD, CONV_W), F32)],
        compiler_params=_params(2),
        name="tail",
    )(x, ao, ga, fw, gf, u, gc, sa, sf, sc, mod_l.reshape(MOD_ROWS, 1, 3 * d), g_post_l.reshape(1, d),
      pa_bf, pf_bf, pc_bf, wo_bf, conv_dw_l, conv_db_l.reshape(1, CONV_W), ln_g_l.reshape(1, CONV_W),
      ln_b_l.reshape(1, CONV_W), w_pw_bf)


INPROJ_TILE = 512
TAIL_TILE = 1024


def kernel(x, c, ctx, c_ctx, w_mod, b_mod, g_pre, g_post, w_in, rpb, w_four, conv_dw, conv_db,
           conv_ln_g, conv_ln_b, w_pw, p_attn, p_four, p_conv, w_out):
    batch, seq, d = x.shape
    depth = w_mod.shape[0]
    rows = seq // GRID_W
    assert batch < MOD_ROWS and rows % (Q_ROWS * ATTN_BLOCKS_PER_STEP) == 0 and rows >= K_ROWS
    ctx_row = batch
    c_all = jnp.zeros((MOD_ROWS, d), F32).at[:batch].set(c).at[ctx_row].set(c_ctx)
    mod = _modulation(c_all, w_mod, b_mod)
    merge_sections = _merge_sections(d)
    bias_tabs = _bias_pair_tables(rpb)

    def latent_row(bi):
        return bi

    def context_row(bi):
        return ctx_row

    w_in_bf = w_in.astype(BF16)
    for l in range(depth):
        wf_bf, pw_bf = w_four[l].astype(BF16), w_pw[l].astype(BF16)
        pa_bf, pf_bf, pc_bf, wo_bf = (p_attn[l].astype(BF16), p_four[l].astype(BF16),
                                      p_conv[l].astype(BF16), w_out[l].astype(BF16))
        conv_w = (conv_dw[l], conv_db[l], conv_ln_g[l], conv_ln_b[l], pw_bf)
        tail_w = (mod[l], g_post[l], pa_bf, pf_bf, pc_bf, wo_bf)
        update_ctx = l < depth - 1

        if update_ctx:
            ctx_sections, ctx_w, ctx_layer = ALL_SECTIONS + merge_sections, w_in_bf, l
        else:
            lo, hi = KV_SECTIONS[0][1], KV_SECTIONS[-1][2]
            ctx_sections = tuple((sec[0], sec[1] - lo, sec[2] - lo) + sec[3:] for sec in KV_SECTIONS)
            ctx_w, ctx_layer = w_in_bf[l:l + 1, :, lo:hi], 0
        ctx_out = dict(zip([sec[0] for sec in ctx_sections],
                           _inproj(ctx, mod[l], g_pre[l], ctx_w, ctx_layer, ctx_sections, context_row,
                                   INPROJ_TILE)))

        lat = dict(zip([sec[0] for sec in ALL_SECTIONS + merge_sections],
                       _inproj(x, mod[l], g_pre[l], w_in_bf, l, ALL_SECTIONS + merge_sections, latent_row,
                               INPROJ_TILE)))
        ao = _attention(lat["q"], lat["k"], lat["v"], ctx_out["k"], ctx_out["v"], bias_tabs[l])
        fw = _fourier(lat["uf"], wf_bf)
        x = _tail(x, ao, lat["ga"], fw, lat["gf"], lat["u"], lat["gc"], lat["sa"], lat["sf"], lat["sc"],
                  *tail_w, *conv_w, latent_row, TAIL_TILE)

        if update_ctx:
            co = ctx_out
            ao_c = _ctx_attention(co["q"], co["k"], co["v"])
            fw_c = _ctx_fourier(co["uf"], wf_bf)
            ctx = _tail(ctx, ao_c, co["ga"], fw_c, co["gf"], co["u"], co["gc"], co["sa"], co["sf"], co["sc"],
                        *tail_w, *conv_w, context_row, TAIL_TILE)
    return x
```
